```python
import math
import jax, jax.numpy as jnp
from jax import lax
import numpy as np

D_MODEL = 1024
BATCH = 2
SEQ = 8192
DEPTH = 4
DEC_BATCH = 128
DEC_SEQ = 8
PAST_LEN = 8192
PAGE_SIZE = 128

N_SSD_LAYERS = (DEPTH + 1) // 2
N_MLA_LAYERS = DEPTH // 2
EPS = 1e-6

SSD_EXPAND = 2
D_INNER = SSD_EXPAND * D_MODEL
SSD_HEADDIM = 64
SSD_HEADS = D_INNER // SSD_HEADDIM
SSD_GROUPS = 8
SSD_HPG = SSD_HEADS // SSD_GROUPS
D_STATE = 128
CONV_W = 4
CONV_DIM = D_INNER + 2 * SSD_GROUPS * D_STATE
SSD_IN_DIM = D_INNER + CONV_DIM + SSD_HEADS
SSD_CHUNK = 128

MLA_HEADS = 16
Q_LORA = 384
KV_LORA = 256
QK_NOPE = 64
QK_ROPE = 32
V_HEAD = 64
MLA_DOWN_DIM = Q_LORA + KV_LORA + QK_ROPE
MLA_SCALE = 1.0 / math.sqrt(QK_NOPE + QK_ROPE)
ROPE_THETA = 10000.0
Q_BLOCK = 128

N_MEM = 256
MEM_HEADS = 4
MEM_HEAD_DIM = D_MODEL // MEM_HEADS
MEM_SCALE = 1.0 / math.sqrt(MEM_HEAD_DIM)

D_FF = 4 * D_MODEL

kernel_name = "hybrid_ssd_mla_memxattn_decode_step"


def rmsnorm(x, g):
    xf = x.astype(jnp.float32)
    y = xf * lax.rsqrt(jnp.mean(xf * xf, axis=-1, keepdims=True) + EPS)
    return (y * g.astype(jnp.float32)).astype(x.dtype)


def rope(x, pos):
    d = x.shape[-1]
    half = d // 2
    inv = ROPE_THETA ** (-jnp.arange(half, dtype=jnp.float32) * (2.0 / d))
    ang = pos.astype(jnp.float32)[:, None] * inv[None, :]
    ang = ang.reshape(ang.shape[:1] + (1,) * (x.ndim - 3) + ang.shape[1:])
    cos, sin = jnp.cos(ang), jnp.sin(ang)
    xf = x.astype(jnp.float32)
    x1, x2 = xf[..., :half], xf[..., half:]
    return jnp.concatenate([x1 * cos - x2 * sin, x2 * cos + x1 * sin], axis=-1).astype(x.dtype)


def causal_conv(xbc, conv_state, w, b):
    full = jnp.concatenate([conv_state.astype(xbc.dtype), xbc], axis=1)
    L = xbc.shape[1]
    out = b + full[:, 0:L] * w[0]
    for k in range(1, CONV_W):
        out = out + full[:, k:k + L] * w[k]
    return jax.nn.silu(out), full[:, -(CONV_W - 1):]


def ssd_scan(xdt, a, Bm, Cm, s0):
    b, l = xdt.shape[:2]
    T = math.gcd(SSD_CHUNK, l)
    c = l // T
    x = xdt.reshape(b, c, T, SSD_GROUPS, SSD_HPG, SSD_HEADDIM)
    a = a.reshape(b, c, T, SSD_GROUPS, SSD_HPG)
    Bc = Bm.reshape(b, c, T, SSD_GROUPS, D_STATE)
    Cc = Cm.reshape(b, c, T, SSD_GROUPS, D_STATE)
    a_cs = jnp.cumsum(a, axis=2)
    causal = jnp.tril(jnp.ones((T, T), dtype=bool))[None, None, :, :, None, None]
    seg = a_cs[:, :, :, None] - a_cs[:, :, None, :]
    decay = jnp.exp(jnp.where(causal, seg, -jnp.inf))
    scores = jnp.einsum('bctgn,bcsgn->bctsg', Cc, Bc)
    y_diag = jnp.einsum('bctsgj,bcsgjp->bctgjp', scores[..., None] * decay, x)
    decay_end = jnp.exp(a_cs[:, :, -1:] - a_cs)
    chunk_states = jnp.einsum('bcsgn,bcsgj,bcsgjp->bcgjpn', Bc, decay_end, x).astype(jnp.float32)
    chunk_decay = jnp.exp(a_cs[:, :, -1])

    def step(s, inp):
        dec, cs = inp
        return dec[..., None, None] * s + cs, s

    s_init = s0.reshape(b, SSD_GROUPS, SSD_HPG, SSD_HEADDIM, D_STATE).astype(jnp.float32)
    s_final, s_in = lax.scan(step, s_init, (jnp.moveaxis(chunk_decay, 1, 0), jnp.moveaxis(chunk_states, 1, 0)))
    s_in = jnp.moveaxis(s_in, 0, 1)
    y_off = jnp.einsum('bctgn,bcgjpn,bctgj->bctgjp', Cc, s_in, jnp.exp(a_cs))
    y = (y_diag + y_off).reshape(b, l, SSD_HEADS, SSD_HEADDIM)
    return y, s_final.reshape(b, SSD_HEADS, SSD_HEADDIM, D_STATE)


def ssd_mixer(h, conv_state, ssm_state, w_in, conv_w, conv_b, dt_bias, a_log, d_skip, norm_g, w_out):
    b, l, _ = h.shape
    zxbcdt = h @ w_in
    z, xbc, dt = jnp.split(zxbcdt, [D_INNER, D_INNER + CONV_DIM], axis=-1)
    xbc, new_conv = causal_conv(xbc, conv_state, conv_w, conv_b)
    xs, Bm, Cm = jnp.split(xbc, [D_INNER, D_INNER + SSD_GROUPS * D_STATE], axis=-1)
    xs = xs.reshape(b, l, SSD_HEADS, SSD_HEADDIM)
    Bm = Bm.reshape(b, l, SSD_GROUPS, D_STATE)
    Cm = Cm.reshape(b, l, SSD_GROUPS, D_STATE)
    dt = jax.nn.softplus((dt + dt_bias).astype(jnp.float32))
    a = dt * (-jnp.exp(a_log.astype(jnp.float32)))
    y, new_ssm = ssd_scan(xs * dt[..., None].astype(xs.dtype), a, Bm, Cm, ssm_state)
    y = y + xs * d_skip[:, None]
    y = y.reshape(b, l, D_INNER).astype(jnp.float32) * jax.nn.silu(z.astype(jnp.float32))
    yg = y.reshape(b, l, SSD_GROUPS, D_INNER // SSD_GROUPS)
    yg = yg * lax.rsqrt(jnp.mean(yg * yg, axis=-1, keepdims=True) + EPS)
    y = (yg.reshape(b, l, D_INNER) * norm_g.astype(jnp.float32)).astype(h.dtype)
    return y @ w_out, new_conv, new_ssm


def mla_project(h, pos, w_down, q_norm, kv_norm, w_uq):
    down = h @ w_down
    cq, ckv, kr = jnp.split(down, [Q_LORA, Q_LORA + KV_LORA], axis=-1)
    q = jnp.einsum('blr,rhd->blhd', rmsnorm(cq, q_norm), w_uq)
    q_nope = q[..., :QK_NOPE]
    q_rope = rope(q[..., QK_NOPE:], pos)
    return q_nope, q_rope, rmsnorm(ckv, kv_norm), rope(kr, pos)


def mla_attend_prompt(q_nope, q_rope, c_kv, k_rope, w_uk, w_uv):
    b, l = q_nope.shape[:2]
    k_nope = jnp.einsum('blc,chd->blhd', c_kv, w_uk)
    v = jnp.einsum('blc,chd->blhd', c_kv, w_uv)
    qb = math.gcd(Q_BLOCK, l)
    nb = l // qb
    k_pos = jnp.arange(l)

    def block(args):
        qn, qr, start = args
        s = (jnp.einsum('bqhd,bkhd->bhqk', qn, k_nope)
             + jnp.einsum('bqhd,bkd->bhqk', qr, k_rope)).astype(jnp.float32) * MLA_SCALE
        q_pos = start + jnp.arange(qb)
        s = jnp.where(q_pos[:, None] >= k_pos[None, :], s, -jnp.inf)
        p = jax.nn.softmax(s, axis=-1).astype(v.dtype)
        return jnp.einsum('bhqk,bkhd->bqhd', p, v)

    qn_b = q_nope.reshape(b, nb, qb, MLA_HEADS, QK_NOPE).swapaxes(0, 1)
    qr_b = q_rope.reshape(b, nb, qb, MLA_HEADS, QK_ROPE).swapaxes(0, 1)
    o = lax.map(block, (qn_b, qr_b, jnp.arange(nb) * qb))
    return o.swapaxes(0, 1).reshape(b, l, MLA_HEADS, V_HEAD)


def mla_attend_sample(q_nope, q_rope, c_kv, k_rope, lat_pool, rope_pool, layer, page_table, w_uk, w_uv):
    b, l = q_nope.shape[:2]
    c_past = lat_pool[layer, page_table].reshape(b, -1, KV_LORA)
    r_past = rope_pool[layer, page_table].reshape(b, -1, QK_ROPE)
    n_past = c_past.shape[1]
    q_lat = jnp.einsum('blhd,chd->blhc', q_nope, w_uk)
    s_past = (jnp.einsum('blhc,bpc->bhlp', q_lat, c_past)
              + jnp.einsum('blhd,bpd->bhlp', q_rope, r_past)).astype(jnp.float32)
    s_new = (jnp.einsum('blhc,bkc->bhlk', q_lat, c_kv)
             + jnp.einsum('blhd,bkd->bhlk', q_rope, k_rope)).astype(jnp.float32)
    s_new = jnp.where(jnp.tril(jnp.ones((l, l), dtype=bool)), s_new, -jnp.inf)
    p = jax.nn.softmax(jnp.concatenate([s_past, s_new], axis=-1) * MLA_SCALE, axis=-1).astype(c_kv.dtype)
    o_lat = (jnp.einsum('bhlp,bpc->blhc', p[..., :n_past], c_past)
             + jnp.einsum('bhlk,bkc->blhc', p[..., n_past:], c_kv))
    return jnp.einsum('blhc,chd->blhd', o_lat, w_uv)


def mem_kv(mem, g, w_kv):
    b = mem.shape[0]
    k, v = jnp.split(rmsnorm(mem, g) @ w_kv, 2, axis=-1)
    return (k.reshape(b, N_MEM, MEM_HEADS, MEM_HEAD_DIM), v.reshape(b, N_MEM, MEM_HEADS, MEM_HEAD_DIM))


def mem_attend(h, k, v, w_q, w_o):
    b, l, _ = h.shape
    q = (h @ w_q).reshape(b, l, MEM_HEADS, MEM_HEAD_DIM)
    s = jnp.einsum('blhd,bmhd->bhlm', q, k).astype(jnp.float32) * MEM_SCALE
    p = jax.nn.softmax(s, axis=-1).astype(v.dtype)
    o = jnp.einsum('bhlm,bmhd->blhd', p, v).reshape(b, l, D_MODEL)
    return o @ w_o


def sq_relu_mlp(h, w_up, w_down):
    return jnp.square(jax.nn.relu(h @ w_up)) @ w_down


def setup_inputs(seed: int = 0) -> dict:
    key = jax.random.key(seed)
    ks = iter(jax.random.split(key, 48))

    def nrm(shape, scale=1.0):
        return jax.random.normal(next(ks), shape, jnp.float32) * scale

    def gain(shape):
        return 1.0 + nrm(shape, 0.01)

    n_pages = PAST_LEN // PAGE_SIZE
    n_used = DEC_BATCH * n_pages
    n_pool = n_used + (n_used + 3) // 4
    page_table = jax.random.permutation(next(ks), n_pool)[:n_used].reshape(DEC_BATCH, n_pages).astype(jnp.int32)
    dt0 = jnp.exp(jax.random.uniform(next(ks), (N_SSD_LAYERS, SSD_HEADS), jnp.float32,
                                     minval=math.log(1e-3), maxval=math.log(1e-1)))
    ssd_dt_bias = dt0 + jnp.log(-jnp.expm1(-dt0))
    ssd_a_log = jnp.log(jax.random.uniform(next(ks), (N_SSD_LAYERS, SSD_HEADS), jnp.float32, minval=1.0, maxval=16.0))
    return {
        "x_prompt": nrm((BATCH, SEQ, D_MODEL)),
        "x_sample": nrm((DEC_BATCH, DEC_SEQ, D_MODEL)),
        "mem_prompt": nrm((BATCH, N_MEM, D_MODEL)),
        "state_ssm": nrm((N_SSD_LAYERS, DEC_BATCH, SSD_HEADS, SSD_HEADDIM, D_STATE), 0.1),
        "state_conv": nrm((N_SSD_LAYERS, DEC_BATCH, CONV_W - 1, CONV_DIM)),
        "cache_mla_latent": nrm((N_MLA_LAYERS, n_pool, PAGE_SIZE, KV_LORA)),
        "cache_mla_rope_k": nrm((N_MLA_LAYERS, n_pool, PAGE_SIZE, QK_ROPE)),
        "cache_mem_k": nrm((DEPTH, DEC_BATCH, N_MEM, MEM_HEADS, MEM_HEAD_DIM)),
        "cache_mem_v": nrm((DEPTH, DEC_BATCH, N_MEM, MEM_HEADS, MEM_HEAD_DIM)),
        "page_table": page_table,
        "norm_mix": gain((DEPTH, D_MODEL)),
        "norm_mem": gain((DEPTH, D_MODEL)),
        "norm_memkv": gain((DEPTH, D_MODEL)),
        "norm_ffn": gain((DEPTH, D_MODEL)),
        "norm_final": gain((D_MODEL,)),
        "ssd_w_in": nrm((N_SSD_LAYERS, D_MODEL, SSD_IN_DIM), D_MODEL ** -0.5),
        "ssd_conv_w": nrm((N_SSD_LAYERS, CONV_W, CONV_DIM), CONV_W ** -0.5),
        "ssd_conv_b": nrm((N_SSD_LAYERS, CONV_DIM), 0.02),
        "ssd_dt_bias": ssd_dt_bias,
        "ssd_a_log": ssd_a_log,
        "ssd_d": gain((N_SSD_LAYERS, SSD_HEADS)),
        "ssd_norm": gain((N_SSD_LAYERS, D_INNER)),
        "ssd_w_out": nrm((N_SSD_LAYERS, D_INNER, D_MODEL), D_INNER ** -0.5),
        "mla_w_down": nrm((N_MLA_LAYERS, D_MODEL, MLA_DOWN_DIM), D_MODEL ** -0.5),
        "mla_q_norm": gain((N_MLA_LAYERS, Q_LORA)),
        "mla_kv_norm": gain((N_MLA_LAYERS, KV_LORA)),
        "mla_w_uq": nrm((N_MLA_LAYERS, Q_LORA, MLA_HEADS, QK_NOPE + QK_ROPE), Q_LORA ** -0.5),
        "mla_w_uk": nrm((N_MLA_LAYERS, KV_LORA, MLA_HEADS, QK_NOPE), KV_LORA ** -0.5),
        "mla_w_uv": nrm((N_MLA_LAYERS, KV_LORA, MLA_HEADS, V_HEAD), KV_LORA ** -0.5),
        "mla_w_o": nrm((N_MLA_LAYERS, MLA_HEADS * V_HEAD, D_MODEL), (MLA_HEADS * V_HEAD) ** -0.5),
        "mem_w_q": nrm((DEPTH, D_MODEL, D_MODEL), D_MODEL ** -0.5),
        "mem_w_kv": nrm((DEPTH, D_MODEL, 2 * D_MODEL), D_MODEL ** -0.5),
        "mem_w_o": nrm((DEPTH, D_MODEL, D_MODEL), D_MODEL ** -0.5),
        "mlp_w_up": nrm((DEPTH, D_MODEL, D_FF), D_MODEL ** -0.5),
        "mlp_w_down": nrm((DEPTH, D_FF, D_MODEL), D_FF ** -0.5),
    }


def reference(x_prompt, x_sample, mem_prompt, state_ssm, state_conv, cache_mla_latent, cache_mla_rope_k,
              cache_mem_k, cache_mem_v, page_table, norm_mix, norm_mem, norm_memkv, norm_ffn, norm_final,
              ssd_w_in, ssd_conv_w, ssd_conv_b, ssd_dt_bias, ssd_a_log, ssd_d, ssd_norm, ssd_w_out,
              mla_w_down, mla_q_norm, mla_kv_norm, mla_w_uq, mla_w_uk, mla_w_uv, mla_w_o,
              mem_w_q, mem_w_kv, mem_w_o, mlp_w_up, mlp_w_down):
    bp, lp, _ = x_prompt.shape
    bs, ls, _ = x_sample.shape
    past_len = page_table.shape[1] * cache_mla_latent.shape[2]
    pos_p = jnp.arange(lp)
    pos_s = past_len + jnp.arange(ls)
    xp, xs = x_prompt, x_sample
    p_ssm, p_conv, p_lat, p_rk, p_mk, p_mv = [], [], [], [], [], []
    s_ssm, s_conv, s_lat, s_rk = [], [], [], []
    for i in range(DEPTH):
        j = i // 2
        hp = rmsnorm(xp, norm_mix[i])
        hs = rmsnorm(xs, norm_mix[i])
        if i % 2 == 0:
            w = (ssd_w_in[j], ssd_conv_w[j], ssd_conv_b[j], ssd_dt_bias[j], ssd_a_log[j], ssd_d[j], ssd_norm[j], ssd_w_out[j])
            conv0 = jnp.zeros((bp, CONV_W - 1, CONV_DIM), xp.dtype)
            ssm0 = jnp.zeros((bp, SSD_HEADS, SSD_HEADDIM, D_STATE), jnp.float32)
            op, cp, sp = ssd_mixer(hp, conv0, ssm0, *w)
            os_, cs, ss = ssd_mixer(hs, state_conv[j], state_ssm[j], *w)
            p_conv.append(cp); p_ssm.append(sp)
            s_conv.append(cs); s_ssm.append(ss)
        else:
            qn, qr, ckv, kr = mla_project(hp, pos_p, mla_w_down[j], mla_q_norm[j], mla_kv_norm[j], mla_w_uq[j])
            op = mla_attend_prompt(qn, qr, ckv, kr, mla_w_uk[j], mla_w_uv[j]).reshape(bp, lp, -1) @ mla_w_o[j]
            p_lat.append(ckv); p_rk.append(kr)
            qn, qr, ckv, kr = mla_project(hs, pos_s, mla_w_down[j], mla_q_norm[j], mla_kv_norm[j], mla_w_uq[j])
            os_ = mla_attend_sample(qn, qr, ckv, kr, cache_mla_latent, cache_mla_rope_k, j, page_table,
                                    mla_w_uk[j], mla_w_uv[j]).reshape(bs, ls, -1) @ mla_w_o[j]
            s_lat.append(ckv); s_rk.append(kr)
        xp = xp + op
        xs = xs + os_
        kp, vp = mem_kv(mem_prompt, norm_memkv[i], mem_w_kv[i])
        p_mk.append(kp); p_mv.append(vp)
        xp = xp + mem_attend(rmsnorm(xp, norm_mem[i]), kp, vp, mem_w_q[i], mem_w_o[i])
        xs = xs + mem_attend(rmsnorm(xs, norm_mem[i]), cache_mem_k[i], cache_mem_v[i], mem_w_q[i], mem_w_o[i])
        xp = xp + sq_relu_mlp(rmsnorm(xp, norm_ffn[i]), mlp_w_up[i], mlp_w_down[i])
        xs = xs + sq_relu_mlp(rmsnorm(xs, norm_ffn[i]), mlp_w_up[i], mlp_w_down[i])
    y_prompt = rmsnorm(xp, norm_final)
    y_sample = rmsnorm(xs, norm_final)
    new_p_ssm = jnp.stack(p_ssm)
    new_p_conv = jnp.stack(p_conv)
    new_p_lat = jnp.stack(p_lat)
    new_p_rk = jnp.stack(p_rk)
    new_p_mk = jnp.stack(p_mk)
    new_p_mv = jnp.stack(p_mv)
    new_s_ssm = jnp.stack(s_ssm)
    new_s_conv = jnp.stack(s_conv)
    new_s_lat = jnp.stack(s_lat)
    new_s_rk = jnp.stack(s_rk)
    return (y_prompt, y_sample, new_p_ssm, new_p_conv, new_p_lat, new_p_rk, new_p_mk, new_p_mv,
            new_s_ssm, new_s_conv, new_s_lat, new_s_rk)
```

```python
import functools
import math

import jax
import jax.numpy as jnp
from jax import lax
from jax.experimental import pallas as pl
from jax.experimental.pallas import tpu as pltpu

F32 = jnp.float32
BF16 = jnp.bfloat16

EPS = 1e-6
ROPE_THETA = 10000.0

V7X_LANES = 128
V7X_SUBLANES = 8
V7X_VMEM_LIMIT_BYTES = 56 * 1024 * 1024

SSD_HEADDIM = 64
SSD_GROUPS = 8
D_STATE = 128
SSD_CHUNK = 128
QK_NOPE = 64
V_HEAD = 64
MEM_HEADS = 4


def _cparams(*sem):
    return pltpu.CompilerParams(dimension_semantics=sem, vmem_limit_bytes=V7X_VMEM_LIMIT_BYTES)


def _rms(x, g):
    return x * lax.rsqrt(jnp.mean(x * x, axis=-1, keepdims=True) + EPS) * g


def _row_tile(m, pref):
    t = min(pref, m)
    assert m % t == 0, (m, t)
    return t


def _mm_body(*refs, norm, res, act):
    it = iter(refs)
    x_ref = next(it)
    g_ref = next(it) if norm else None
    w_ref = next(it)
    r_ref = next(it) if res else None
    o_ref = next(it)
    xn_ref = next(it)

    @pl.when(pl.program_id(1) == 0)
    def _():
        x = x_ref[...].astype(F32)
        if norm:
            x = _rms(x, g_ref[...])
        xn_ref[...] = x.astype(BF16)

    acc = jnp.dot(xn_ref[...], w_ref[...], preferred_element_type=F32)
    if act == "relu2":
        acc = jnp.square(jnp.maximum(acc, 0.0))
    if res:
        acc = r_ref[...] + acc
    o_ref[...] = acc.astype(o_ref.dtype)


def _mm(x, w, *, g=None, res=None, act=None, out_dtype=F32, tm=1024, tn=1024, name="mm"):
    m, k = x.shape
    k2, n = w.shape
    assert k == k2
    tm = _row_tile(m, tm)
    tn = _row_tile(n, tn)
    in_specs = [pl.BlockSpec((tm, k), lambda i, j: (i, 0))]
    args = [x]
    if g is not None:
        in_specs.append(pl.BlockSpec((1, k), lambda i, j: (0, 0)))
        args.append(g.reshape(1, k).astype(F32))
    in_specs.append(pl.BlockSpec((k, tn), lambda i, j: (0, j)))
    args.append(w)
    if res is not None:
        in_specs.append(pl.BlockSpec((tm, tn), lambda i, j: (i, j)))
        args.append(res)
    return pl.pallas_call(
        functools.partial(_mm_body, norm=g is not None, res=res is not None, act=act),
        out_shape=jax.ShapeDtypeStruct((m, n), out_dtype),
        grid=(m // tm, n // tn),
        in_specs=in_specs,
        out_specs=pl.BlockSpec((tm, tn), lambda i, j: (i, j)),
        scratch_shapes=[pltpu.VMEM((tm, k), BF16)],
        compiler_params=_cparams("parallel", "arbitrary"),
        name=name,
    )(*args)


def _mlp_body(x_ref, g_ref, wu_ref, wd_ref, o_ref, xn_ref, acc_ref, *, final_norm):
    k = pl.program_id(1)

    @pl.when(k == 0)
    def _():
        x = x_ref[...]
        xn_ref[...] = _rms(x, g_ref[...]).astype(BF16)
        acc_ref[...] = x

    h = jnp.dot(xn_ref[...], wu_ref[...], preferred_element_type=F32)
    h = jnp.square(jnp.maximum(h, 0.0)).astype(BF16)
    acc_ref[...] += jnp.dot(h, wd_ref[...], preferred_element_type=F32)

    @pl.when(k == pl.num_programs(1) - 1)
    def _():
        o_ref[...] = acc_ref[...]


def _mlp(x, g, w_up, w_down, *, tm=1024, tf=512):
    m, d = x.shape
    ff = w_up.shape[1]
    tm = _row_tile(m, tm)
    tf = _row_tile(ff, tf)
    return pl.pallas_call(
        functools.partial(_mlp_body, final_norm=False),
        out_shape=jax.ShapeDtypeStruct((m, d), F32),
        grid=(m // tm, ff // tf),
        in_specs=[
            pl.BlockSpec((tm, d), lambda i, k: (i, 0)),
            pl.BlockSpec((1, d), lambda i, k: (0, 0)),
            pl.BlockSpec((d, tf), lambda i, k: (0, k)),
            pl.BlockSpec((tf, d), lambda i, k: (k, 0)),
        ],
        out_specs=pl.BlockSpec((tm, d), lambda i, k: (i, 0)),
        scratch_shapes=[pltpu.VMEM((tm, d), BF16), pltpu.VMEM((tm, d), F32)],
        compiler_params=_cparams("parallel", "arbitrary"),
        name="mlp",
    )(x, g.reshape(1, d).astype(F32), w_up, w_down)


def _norm_body(x_ref, g_ref, o_ref):
    o_ref[...] = _rms(x_ref[...], g_ref[...])


def _norm(x, g, *, tm=1024):
    m, d = x.shape
    tm = _row_tile(m, tm)
    return pl.pallas_call(
        _norm_body,
        out_shape=jax.ShapeDtypeStruct((m, d), F32),
        grid=(m // tm,),
        in_specs=[pl.BlockSpec((tm, d), lambda i: (i, 0)), pl.BlockSpec((1, d), lambda i: (0, 0))],
        out_specs=pl.BlockSpec((tm, d), lambda i: (i, 0)),
        compiler_params=_cparams("parallel"),
        name="final_norm",
    )(x, g.reshape(1, d).astype(F32))


def _memattn_body(q_ref, k_ref, v_ref, o_ref, *, nb, rows, heads, scale):
    dh = q_ref.shape[-1] // heads
    for i in range(nb):
        for h in range(heads):
            q = q_ref[i * rows:(i + 1) * rows, h * dh:(h + 1) * dh].astype(BF16)
            k = k_ref[i, :, h * dh:(h + 1) * dh].astype(BF16)
            v = v_ref[i, :, h * dh:(h + 1) * dh].astype(BF16)
            s = lax.dot_general(q, k, (((1,), (1,)), ((), ())), preferred_element_type=F32) * scale
            s = s - jnp.max(s, axis=-1, keepdims=True)
            p = jnp.exp(s)
            p = p / jnp.sum(p, axis=-1, keepdims=True)
            o = jnp.dot(p.astype(BF16), v, preferred_element_type=F32)
            o_ref[i * rows:(i + 1) * rows, h * dh:(h + 1) * dh] = o.astype(o_ref.dtype)


def _memattn(q, k, v, *, row_offset, n_groups, rows, nb, kv_index):
    d = q.shape[-1]
    n_mem = k.shape[-2]
    steps = n_groups // nb
    blk = nb * rows
    assert row_offset % blk == 0
    lead = k.ndim - 3
    kv_block = (None,) * lead + (nb, n_mem, d)
    return pl.pallas_call(
        functools.partial(_memattn_body, nb=nb, rows=rows, heads=MEM_HEADS,
                          scale=1.0 / math.sqrt(d // MEM_HEADS)),
        out_shape=jax.ShapeDtypeStruct((n_groups * rows, d), F32),
        grid=(steps,),
        in_specs=[
            pl.BlockSpec((blk, d), lambda s: (row_offset // blk + s, 0)),
            pl.BlockSpec(kv_block, kv_index),
            pl.BlockSpec(kv_block, kv_index),
        ],
        out_specs=pl.BlockSpec((blk, d), lambda s: (s, 0)),
        compiler_params=_cparams("parallel"),
        name="memattn",
    )(q, k, v)


def _split3_lanes(v, heads):
    lane = lax.broadcasted_iota(jnp.int32, v.shape, 1)
    hi = v.astype(BF16).astype(F32)
    r1 = v - hi
    mid = r1.astype(BF16).astype(F32)
    lo = r1 - mid
    out = jnp.where(lane < heads, hi, jnp.where(lane < 2 * heads, mid, jnp.where(lane < 3 * heads, lo, 0.0)))
    return out.astype(BF16)


def _ssd_body(z_ref, x_ref, bc_ref, dt_ref, cw_ref, cb_ref, dtb_ref, alog_ref, dsk_ref, ng_ref, e3_ref,
              conv0_ref, s0_ref,
              y_ref, sfin_ref, ctail_ref,
              extx_ref, extbc_ref, st_ref, *, T, heads, groups, has_init):
    c = pl.program_id(1)
    nc = pl.num_programs(1)
    d_inner = x_ref.shape[-1]
    hpg = heads // groups
    gw = d_inner // groups
    hd = d_inner // heads
    n = D_STATE

    @pl.when(c == 0)
    def _():
        if has_init:
            extx_ref[0:8, :] = conv0_ref[0, :, 0:d_inner]
            extbc_ref[0:8, :] = conv0_ref[0, :, d_inner:]
            st_ref[...] = s0_ref[0].T
        else:
            extx_ref[0:8, :] = jnp.zeros((8, d_inner), F32)
            extbc_ref[0:8, :] = jnp.zeros((8, extbc_ref.shape[-1]), F32)
            st_ref[...] = jnp.zeros(st_ref.shape, F32)

    extx_ref[8:8 + T, :] = x_ref[...]
    extbc_ref[8:8 + T, :] = bc_ref[...]

    def conv(ext_ref, lo, width):
        acc = cb_ref[:, lo:lo + width] + ext_ref[5:5 + T, :] * cw_ref[0:1, lo:lo + width]
        for k in range(1, 4):
            acc = acc + ext_ref[5 + k:5 + k + T, :] * cw_ref[k:k + 1, lo:lo + width]
        return acc * jax.nn.sigmoid(acc)

    xc = conv(extx_ref, 0, d_inner)
    bcc = conv(extbc_ref, d_inner, extbc_ref.shape[-1])
    tail_x = extx_ref[T:T + 8, :]
    tail_bc = extbc_ref[T:T + 8, :]
    extx_ref[0:8, :] = tail_x
    extbc_ref[0:8, :] = tail_bc

    @pl.when(c == nc - 1)
    def _():
        ctail_ref[0, :, 0:d_inner] = tail_x
        ctail_ref[0, :, d_inner:] = tail_bc

    dtr = dt_ref[...] + dtb_ref[...]
    dtv = jnp.maximum(dtr, 0.0) + jnp.log1p(jnp.exp(-jnp.abs(dtr)))
    a = dtv * (-jnp.exp(alog_ref[...]))
    row = lax.broadcasted_iota(jnp.int32, a.shape, 0)
    acs = a
    sh = 1
    while sh < T:
        acs = acs + jnp.where(row >= sh, pltpu.roll(acs, sh, axis=0), 0.0)
        sh *= 2
    a_last = acs[T - 1:T, :]
    e3 = e3_ref[...]

    def expand(v):
        return jnp.dot(_split3_lanes(v, heads), e3, preferred_element_type=F32)

    dt_e = expand(dtv)
    dend_e = expand(jnp.exp(a_last - acs))
    eacs_e = expand(jnp.exp(acs))
    xdt = xc * dt_e
    xdt_b = xdt.astype(BF16)
    xw_b = (xdt * dend_e).astype(BF16)
    cdec_e = eacs_e[T - 1:T, :]

    if T < V7X_LANES:
        acs_p = jnp.concatenate([acs, jnp.zeros((V7X_LANES - T, V7X_LANES), F32)], axis=0)
    else:
        acs_p = acs
    acs_t = acs_p.T
    ti = lax.broadcasted_iota(jnp.int32, (T, T), 0)
    si = lax.broadcasted_iota(jnp.int32, (T, T), 1)
    causal = ti >= si
    lane_g = lax.broadcasted_iota(jnp.int32, (T, gw), 1)

    z = z_ref[...]
    gate = z * jax.nn.sigmoid(z)
    for g in range(groups):
        bg = bcc[:, g * n:(g + 1) * n]
        cg = bcc[:, groups * n + g * n: groups * n + (g + 1) * n].astype(BF16)
        bg_b = bg.astype(BF16)
        gmat = lax.dot_general(cg, bg_b, (((1,), (1,)), ((), ())), preferred_element_type=F32)
        st_g = st_ref[:, g * gw:(g + 1) * gw]
        y_g = jnp.dot(cg, st_g.astype(BF16), preferred_element_type=F32) * eacs_e[:, g * gw:(g + 1) * gw]
        xg = xdt_b[:, g * gw:(g + 1) * gw]
        for j in range(hpg):
            h = g * hpg + j
            seg = acs[:, h:h + 1] - acs_t[h:h + 1, 0:T]
            decay = jnp.exp(jnp.where(causal, seg, -jnp.inf))
            m_h = (gmat * decay).astype(BF16)
            x_h = jnp.where((lane_g >= j * hd) & (lane_g < (j + 1) * hd), xg, jnp.zeros_like(xg))
            y_g = y_g + jnp.dot(m_h, x_h, preferred_element_type=F32)
        cs_t = jnp.dot(bg.T.astype(BF16), xw_b[:, g * gw:(g + 1) * gw], preferred_element_type=F32)
        st_ref[:, g * gw:(g + 1) * gw] = st_g * cdec_e[:, g * gw:(g + 1) * gw] + cs_t
        y_g = y_g + xc[:, g * gw:(g + 1) * gw] * dsk_ref[:, g * gw:(g + 1) * gw]
        y_g = y_g * gate[:, g * gw:(g + 1) * gw]
        y_g = _rms(y_g, ng_ref[:, g * gw:(g + 1) * gw])
        y_ref[:, g * gw:(g + 1) * gw] = y_g.astype(y_ref.dtype)

    @pl.when(c == nc - 1)
    def _():
        sfin_ref[0] = st_ref[...].T


def _ssd(zx, conv_w, conv_b, dt_bias_rep, a_log_rep, d_exp, norm_g, e3, *, row_offset, nb, L, T,
         conv0=None, s0=None):
    heads = d_exp.shape[-1] // SSD_HEADDIM
    d_inner = d_exp.shape[-1]
    bcw = 2 * SSD_GROUPS * D_STATE
    assert bcw == d_inner, "column blocks are indexed in units of d_inner"
    nc = L // T
    assert row_offset % T == 0
    rb0 = row_offset // T
    has_init = conv0 is not None
    rowmap = lambda col: (lambda b, c: (rb0 + b * nc + c, col))
    const = lambda b, c: (0, 0)
    per_b = lambda b, c: (b, 0, 0)
    if not has_init:
        conv0 = jnp.zeros((1, 8, d_inner + bcw), F32)
        s0 = jnp.zeros((1, 8, D_STATE), F32)
        init_specs = [pl.BlockSpec((1, 8, d_inner + bcw), lambda b, c: (0, 0, 0)),
                      pl.BlockSpec((1, 8, D_STATE), lambda b, c: (0, 0, 0))]
    else:
        init_specs = [pl.BlockSpec((1, 8, d_inner + bcw), per_b),
                      pl.BlockSpec((1, d_inner, D_STATE), per_b)]
    dt_col = (2 * d_inner + bcw) // V7X_LANES
    y, sfin, ctail = pl.pallas_call(
        functools.partial(_ssd_body, T=T, heads=heads, groups=SSD_GROUPS, has_init=has_init),
        out_shape=(jax.ShapeDtypeStruct((nb * L, d_inner), BF16),
                   jax.ShapeDtypeStruct((nb, d_inner, D_STATE), F32),
                   jax.ShapeDtypeStruct((nb, 8, d_inner + bcw), F32)),
        grid=(nb, nc),
        in_specs=[
            pl.BlockSpec((T, d_inner), rowmap(0)),
            pl.BlockSpec((T, d_inner), rowmap(1)),
            pl.BlockSpec((T, bcw), rowmap(2)),
            pl.BlockSpec((T, V7X_LANES), rowmap(dt_col)),
            pl.BlockSpec((4, d_inner + bcw), const),
            pl.BlockSpec((1, d_inner + bcw), const),
            pl.BlockSpec((1, V7X_LANES), const),
            pl.BlockSpec((1, V7X_LANES), const),
            pl.BlockSpec((1, d_inner), const),
            pl.BlockSpec((1, d_inner), const),
            pl.BlockSpec((V7X_LANES, d_inner), const),
        ] + init_specs,
        out_specs=(pl.BlockSpec((T, d_inner), lambda b, c: (b * nc + c, 0)),
                   pl.BlockSpec((1, d_inner, D_STATE), per_b),
                   pl.BlockSpec((1, 8, d_inner + bcw), per_b)),
        scratch_shapes=[pltpu.VMEM((T + 8, d_inner), F32), pltpu.VMEM((T + 8, bcw), F32),
                        pltpu.VMEM((D_STATE, d_inner), F32)],
        compiler_params=_cparams("parallel", "arbitrary"),
        name="ssd",
    )(zx, zx, zx, zx, conv_w, conv_b, dt_bias_rep, a_log_rep, d_exp, norm_g, e3, conv0, s0)
    return y, sfin, ctail


def _ssd_w_in_pad(w_in):
    heads = w_in.shape[1] - (w_in.shape[1] // V7X_LANES) * V7X_LANES
    main = w_in[:, :w_in.shape[1] - heads]
    dt = w_in[:, w_in.shape[1] - heads:]
    return jnp.concatenate([main] + [dt] * (V7X_LANES // heads), axis=1)


def _ssd_params(conv_w, conv_b, dt_bias, a_log, d_skip, norm_g):
    heads = dt_bias.shape[0]
    rep = V7X_LANES // heads
    d_inner = heads * SSD_HEADDIM
    src = jnp.arange(V7X_LANES)[:, None]
    dst_head = jnp.arange(d_inner)[None, :] // SSD_HEADDIM
    e3 = ((src % heads == dst_head) & (src < 3 * heads)).astype(BF16)
    return (conv_w.astype(F32), conv_b.reshape(1, -1).astype(F32),
            jnp.tile(dt_bias.reshape(1, heads), (1, rep)).astype(F32),
            jnp.tile(a_log.reshape(1, heads), (1, rep)).astype(F32),
            jnp.repeat(d_skip, SSD_HEADDIM).reshape(1, d_inner).astype(F32),
            norm_g.reshape(1, d_inner).astype(F32), e3)


SLOT = V7X_LANES


def _mla_proj_body(x_ref, g_ref, wd_ref, qn_ref, kvn_ref, wuq_ref, wuqr_ref, wuk_ref, wuv_ref, cos_ref, sin_ref,
                   q_ref, kk_ref, v_ref, lat_ref, kr_ref, *, heads, q_lora, kv_lora, rope, scale):
    xn = _rms(x_ref[...], g_ref[...]).astype(BF16)
    down = jnp.dot(xn, wd_ref[...], preferred_element_type=F32)
    cq = down[:, :q_lora]
    ckv = down[:, q_lora:q_lora + kv_lora]
    krs = down[:, q_lora + kv_lora:q_lora + kv_lora + SLOT]
    krr = down[:, q_lora + kv_lora + SLOT:]
    cos = cos_ref[...]
    sin = sin_ref[...]
    cqn = _rms(cq, qn_ref[...]).astype(BF16)
    qp = jnp.dot(cqn, wuq_ref[...], preferred_element_type=F32)
    qr = jnp.dot(cqn, wuqr_ref[...], preferred_element_type=F32)
    lat = _rms(ckv, kvn_ref[...])
    lat_ref[...] = lat
    kr_rot = krs * cos + krr * sin
    kr_ref[...] = kr_rot[:, QK_NOPE:QK_NOPE + rope]
    latb = lat.astype(BF16)
    kn = jnp.dot(latb, wuk_ref[...], preferred_element_type=F32)
    for h in range(heads):
        sl = slice(h * SLOT, (h + 1) * SLOT)
        q_ref[:, sl] = ((qp[:, sl] * cos + qr[:, sl] * sin) * scale).astype(BF16)
        kk_ref[:, sl] = (kn[:, sl] + kr_rot).astype(BF16)
    v_ref[...] = jnp.dot(latb, wuv_ref[...], preferred_element_type=F32).astype(BF16)


def _mla_proj(x, g, w, cos_tab, sin_tab, tab_index, *, tm=512):
    m, d = x.shape
    heads = w["wuq"].shape[1] // SLOT
    q_lora = w["qn"].shape[-1]
    kv_lora = w["kvn"].shape[-1]
    rope = w["rope"]
    tm = _row_tile(m, tm)
    const = lambda i: (0, 0)
    rowb = lambda i: (i, 0)
    full = lambda a: pl.BlockSpec(a.shape, const)
    return pl.pallas_call(
        functools.partial(_mla_proj_body, heads=heads, q_lora=q_lora, kv_lora=kv_lora, rope=rope,
                          scale=w["scale"]),
        out_shape=(jax.ShapeDtypeStruct((m, heads * SLOT), BF16),
                   jax.ShapeDtypeStruct((m, heads * SLOT), BF16),
                   jax.ShapeDtypeStruct((m, heads * V_HEAD), BF16),
                   jax.ShapeDtypeStruct((m, kv_lora), F32),
                   jax.ShapeDtypeStruct((m, rope), F32)),
        grid=(m // tm,),
        in_specs=[pl.BlockSpec((tm, d), rowb), pl.BlockSpec((1, d), const),
                  full(w["wd"]), full(w["qn"]), full(w["kvn"]), full(w["wuq"]), full(w["wuqr"]),
                  full(w["wuk"]), full(w["wuv"]),
                  pl.BlockSpec((tm, SLOT), lambda i: (tab_index(i), 0)),
                  pl.BlockSpec((tm, SLOT), lambda i: (tab_index(i), 0))],
        out_specs=(pl.BlockSpec((tm, heads * SLOT), rowb), pl.BlockSpec((tm, heads * SLOT), rowb),
                   pl.BlockSpec((tm, heads * V_HEAD), rowb), pl.BlockSpec((tm, kv_lora), rowb),
                   pl.BlockSpec((tm, rope), rowb)),
        compiler_params=_cparams("parallel"),
        name="mla_proj",
    )(x, g.reshape(1, d).astype(F32), w["wd"], w["qn"], w["kvn"], w["wuq"], w["wuqr"], w["wuk"], w["wuv"],
      cos_tab, sin_tab)


def _mla_weights(w_down, q_norm, kv_norm, w_uq, w_uk, w_uv):
    q_lora = q_norm.shape[0]
    kv_lora = kv_norm.shape[0]
    heads = w_uq.shape[1]
    qk = w_uq.shape[2]
    rope = qk - QK_NOPE
    half = rope // 2
    pad = SLOT - qk

    def slot_pair(wr):
        z_lo = jnp.zeros(wr.shape[:-1] + (QK_NOPE,), wr.dtype)
        z_hi = jnp.zeros(wr.shape[:-1] + (pad,), wr.dtype)
        plain = jnp.concatenate([z_lo, wr, z_hi], axis=-1)
        rot = jnp.concatenate([z_lo, -wr[..., half:], wr[..., :half], z_hi], axis=-1)
        return plain, rot

    w_kr = w_down[:, q_lora + kv_lora:]
    kr_plain, kr_rot = slot_pair(w_kr)
    wd = jnp.concatenate([w_down[:, :q_lora + kv_lora], kr_plain, kr_rot], axis=1)
    uq_nope = jnp.concatenate([w_uq[..., :QK_NOPE], jnp.zeros(w_uq.shape[:2] + (SLOT - QK_NOPE,), w_uq.dtype)], -1)
    uq_plain, uq_rot = slot_pair(w_uq[..., QK_NOPE:])
    wuq = (uq_nope + uq_plain).reshape(q_lora, heads * SLOT)
    wuqr = uq_rot.reshape(q_lora, heads * SLOT)
    wuk = jnp.concatenate([w_uk, jnp.zeros(w_uk.shape[:2] + (SLOT - QK_NOPE,), w_uk.dtype)], -1)
    wuk = wuk.reshape(kv_lora, heads * SLOT)
    wuv = w_uv.reshape(kv_lora, heads * V_HEAD)
    eye_r = jnp.zeros((SLOT, SLOT), w_uk.dtype).at[QK_NOPE + jnp.arange(rope), jnp.arange(rope)].set(1.0)
    uk_t = jnp.transpose(w_uk, (1, 2, 0))
    uk_t = jnp.concatenate([uk_t, jnp.zeros((heads, SLOT - QK_NOPE, kv_lora), w_uk.dtype)], axis=1)
    wabs = jnp.concatenate([uk_t, jnp.broadcast_to(eye_r, (heads, SLOT, SLOT))], axis=2)
    uv = jnp.transpose(w_uv, (1, 0, 2)).reshape(heads // 2, 2, kv_lora, V_HEAD)
    z = jnp.zeros((heads // 2, kv_lora, V_HEAD), w_uv.dtype)
    wuv_bd = jnp.concatenate([jnp.concatenate([uv[:, 0], z], axis=2), jnp.concatenate([z, uv[:, 1]], axis=2)], axis=1)
    return dict(wd=wd.astype(BF16), qn=q_norm.reshape(1, -1).astype(F32), kvn=kv_norm.reshape(1, -1).astype(F32),
                wuq=wuq.astype(BF16), wuqr=wuqr.astype(BF16), wuk=wuk.astype(BF16), wuv=wuv.astype(BF16),
                wabs=wabs.astype(BF16), wuv_bd=wuv_bd.astype(BF16), rope=rope, scale=1.0 / math.sqrt(qk))


def _rope_tables(positions, rope):
    half = rope // 2
    inv = ROPE_THETA ** (-jnp.arange(half, dtype=F32) * (2.0 / rope))
    ang = positions.astype(F32)[:, None] * inv[None, :]
    c, s = jnp.cos(ang), jnp.sin(ang)
    p = positions.shape[0]
    ones = jnp.ones((p, QK_NOPE), F32)
    zeros = jnp.zeros((p, QK_NOPE), F32)
    zpad = jnp.zeros((p, SLOT - QK_NOPE - rope), F32)
    return (jnp.concatenate([ones, c, c, zpad], axis=1), jnp.concatenate([zeros, s, s, zpad], axis=1))


def _headmm_body(x_ref, w_ref, o_ref):
    o_ref[...] = jnp.dot(x_ref[...], w_ref[...], preferred_element_type=F32).astype(o_ref.dtype)


def _headmm(x, w, *, out_dtype=BF16):
    m = x.shape[0]
    g, kb, nb = w.shape
    assert x.shape[1] == g * kb
    return pl.pallas_call(
        _headmm_body,
        out_shape=jax.ShapeDtypeStruct((m, g * nb), out_dtype),
        grid=(g,),
        in_specs=[pl.BlockSpec((m, kb), lambda h: (0, h)), pl.BlockSpec((None, kb, nb), lambda h: (h, 0, 0))],
        out_specs=pl.BlockSpec((m, nb), lambda h: (0, h)),
        compiler_params=_cparams("parallel"),
        name="headmm",
    )(x, w)


def _flash_body(qi_ref, ki_ref, q_ref, k_ref, v_ref, o_ref, m_ref, l_ref, acc_ref, *, heads, tq):
    p_idx = pl.program_id(1)
    qi = qi_ref[p_idx]
    ki = ki_ref[p_idx]

    @pl.when(ki == 0)
    def _():
        m_ref[...] = jnp.full(m_ref.shape, -jnp.inf, F32)
        l_ref[...] = jnp.zeros(l_ref.shape, F32)
        acc_ref[...] = jnp.zeros(acc_ref.shape, F32)

    def step(masked):
        if masked:
            ri = lax.broadcasted_iota(jnp.int32, (tq, tq), 0)
            ci = lax.broadcasted_iota(jnp.int32, (tq, tq), 1)
            keep = ri >= ci
        lane = lax.broadcasted_iota(jnp.int32, (tq, SLOT), 1)
        even = lane < V_HEAD
        for hp in range(heads // 2):
            vp = v_ref[:, hp * SLOT:(hp + 1) * SLOT]
            pv = None
            alphas = []
            for e in range(2):
                h = 2 * hp + e
                q = q_ref[:, h * SLOT:(h + 1) * SLOT]
                k = k_ref[:, h * SLOT:(h + 1) * SLOT]
                s = lax.dot_general(q, k, (((1,), (1,)), ((), ())), preferred_element_type=F32)
                if masked:
                    s = jnp.where(keep, s, -jnp.inf)
                m_prev = m_ref[h]
                m_new = jnp.maximum(m_prev, jnp.max(s, axis=-1, keepdims=True))
                alpha = jnp.exp(m_prev - m_new)
                p = jnp.exp(s - m_new[:, 0:1])
                l_ref[h] = alpha * l_ref[h] + jnp.sum(p, axis=-1, keepdims=True)
                m_ref[h] = m_new
                v_h = jnp.where(even if e == 0 else ~even, vp, jnp.zeros_like(vp))
                d = jnp.dot(p.astype(BF16), v_h, preferred_element_type=F32)
                pv = d if pv is None else pv + d
                alphas.append(alpha)
            a_pair = jnp.where(even, alphas[0], alphas[1])
            acc_ref[:, hp * SLOT:(hp + 1) * SLOT] = acc_ref[:, hp * SLOT:(hp + 1) * SLOT] * a_pair + pv

    @pl.when(ki < qi)
    def _():
        step(False)

    @pl.when(ki == qi)
    def _():
        step(True)
        lane = lax.broadcasted_iota(jnp.int32, (tq, SLOT), 1)
        even = lane < V_HEAD
        for hp in range(heads // 2):
            l_pair = jnp.where(even, l_ref[2 * hp], l_ref[2 * hp + 1])
            o_ref[:, hp * SLOT:(hp + 1) * SLOT] = (acc_ref[:, hp * SLOT:(hp + 1) * SLOT] / l_pair).astype(o_ref.dtype)


def _flash(q, kk, v, *, nb, L, tq=512):
    heads = q.shape[1] // SLOT
    tq = _row_tile(L, tq)
    nq = L // tq
    pairs = [(i, j) for i in range(nq) for j in range(i + 1)]
    qi_tab = jnp.asarray([p[0] for p in pairs], jnp.int32)
    ki_tab = jnp.asarray([p[1] for p in pairs], jnp.int32)
    grid_spec = pltpu.PrefetchScalarGridSpec(
        num_scalar_prefetch=2,
        grid=(nb, len(pairs)),
        in_specs=[
            pl.BlockSpec((tq, heads * SLOT), lambda b, p, qi, ki: (b * nq + qi[p], 0)),
            pl.BlockSpec((tq, heads * SLOT), lambda b, p, qi, ki: (b * nq + ki[p], 0)),
            pl.BlockSpec((tq, heads * V_HEAD), lambda b, p, qi, ki: (b * nq + ki[p], 0)),
        ],
        out_specs=pl.BlockSpec((tq, heads * V_HEAD), lambda b, p, qi, ki: (b * nq + qi[p], 0)),
        scratch_shapes=[pltpu.VMEM((heads, tq, SLOT), F32), pltpu.VMEM((heads, tq, SLOT), F32),
                        pltpu.VMEM((tq, heads * V_HEAD), F32)],
    )
    return pl.pallas_call(
        functools.partial(_flash_body, heads=heads, tq=tq),
        out_shape=jax.ShapeDtypeStruct((nb * L, heads * V_HEAD), BF16),
        grid_spec=grid_spec,
        compiler_params=_cparams("parallel", "arbitrary"),
        name="mla_flash",
    )(qi_tab, ki_tab, q, kk, v)


def _decode_body(pt_ref, q_ref, cnew_ref, rnew_ref, *rest, npg, page, kv_lora, rope, heads, ls):
    lat_refs = rest[:npg]
    rope_refs = rest[npg:2 * npg]
    o_ref = rest[2 * npg]
    m_ref, l_ref, acc_ref, cbuf_ref, rbuf_ref = rest[2 * npg + 1:]
    j = pl.program_id(1)
    nj = pl.num_programs(1)
    rows = q_ref.shape[0]

    @pl.when(j == 0)
    def _():
        m_ref[...] = jnp.full(m_ref.shape, -jnp.inf, F32)
        l_ref[...] = jnp.zeros(l_ref.shape, F32)
        acc_ref[...] = jnp.zeros(acc_ref.shape, F32)

    for i in range(npg):
        cbuf_ref[i * page:(i + 1) * page, :] = lat_refs[i][...].astype(BF16)
        rbuf_ref[i * page:(i + 1) * page, :] = rope_refs[i][...].astype(BF16)

    q_lat = q_ref[:, :kv_lora]
    q_r = q_ref[:, kv_lora:kv_lora + rope]

    def update(s, c_b):
        m_prev = m_ref[...]
        m_new = jnp.maximum(m_prev, jnp.max(s, axis=-1, keepdims=True))
        alpha = jnp.exp(m_prev - m_new)
        p = jnp.exp(s - m_new[:, 0:1])
        l_ref[...] = alpha * l_ref[...] + jnp.sum(p, axis=-1, keepdims=True)
        m_ref[...] = m_new
        pv = jnp.dot(p.astype(BF16), c_b, preferred_element_type=F32)
        acc_ref[...] = acc_ref[...] * jnp.concatenate([alpha] * (kv_lora // V7X_LANES), axis=1) + pv

    nt = (((1,), (1,)), ((), ()))
    c_b = cbuf_ref[...]
    s = (lax.dot_general(q_lat, c_b, nt, preferred_element_type=F32)
         + lax.dot_general(q_r, rbuf_ref[...], nt, preferred_element_type=F32))
    update(s, c_b)

    @pl.when(j == nj - 1)
    def _():
        cn = cnew_ref[...].astype(BF16)
        rn = rnew_ref[...].astype(BF16)
        sn = (lax.dot_general(q_lat, cn, nt, preferred_element_type=F32)
              + lax.dot_general(q_r, rn, nt, preferred_element_type=F32))
        qpos = lax.broadcasted_iota(jnp.int32, (rows, ls), 0) // heads
        kpos = lax.broadcasted_iota(jnp.int32, (rows, ls), 1)
        sn = jnp.where(qpos >= kpos, sn, -jnp.inf)
        update(sn, cn)
        l = l_ref[...]
        o_ref[...] = (acc_ref[...] / jnp.concatenate([l] * (kv_lora // V7X_LANES), axis=1)).astype(o_ref.dtype)


def _decode(q_ext, lat_new, kr_new, lat_pool, rope_pool, page_table, *, layer, bs, ls, heads, new_row_offset, npg=8):
    kv_lora = lat_new.shape[1]
    rope = kr_new.shape[1]
    page = lat_pool.shape[2]
    n_pages = page_table.shape[1]
    assert n_pages % npg == 0 and new_row_offset % ls == 0
    nj = n_pages // npg
    rows = ls * heads
    qw = q_ext.shape[1]

    def lat_spec(i):
        return pl.BlockSpec((None, None, page, kv_lora), lambda b, j, pt: (layer, pt[b, j * npg + i], 0, 0))

    def rope_spec(i):
        return pl.BlockSpec((None, None, page, rope), lambda b, j, pt: (layer, pt[b, j * npg + i], 0, 0))

    grid_spec = pltpu.PrefetchScalarGridSpec(
        num_scalar_prefetch=1,
        grid=(bs, nj),
        in_specs=[pl.BlockSpec((rows, qw), lambda b, j, pt: (b, 0)),
                  pl.BlockSpec((ls, kv_lora), lambda b, j, pt: (new_row_offset // ls + b, 0)),
                  pl.BlockSpec((ls, rope), lambda b, j, pt: (new_row_offset // ls + b, 0))]
                 + [lat_spec(i) for i in range(npg)] + [rope_spec(i) for i in range(npg)],
        out_specs=pl.BlockSpec((rows, kv_lora), lambda b, j, pt: (b, 0)),
        scratch_shapes=[pltpu.VMEM((rows, V7X_LANES), F32), pltpu.VMEM((rows, V7X_LANES), F32),
                        pltpu.VMEM((rows, kv_lora), F32),
                        pltpu.VMEM((npg * page, kv_lora), BF16), pltpu.VMEM((npg * page, rope), BF16)],
    )
    return pl.pallas_call(
        functools.partial(_decode_body, npg=npg, page=page, kv_lora=kv_lora, rope=rope, heads=heads, ls=ls),
        out_shape=jax.ShapeDtypeStruct((bs * rows, kv_lora), BF16),
        grid_spec=grid_spec,
        compiler_params=_cparams("parallel", "arbitrary"),
        name="mla_decode",
    )(page_table, q_ext, lat_new, kr_new, *([lat_pool] * npg), *([rope_pool] * npg))


MEM_ROWS = 1024
MEM_NB = 4
PROJ_TM = 512


def kernel(x_prompt, x_sample, mem_prompt, state_ssm, state_conv, cache_mla_latent, cache_mla_rope_k, cache_mem_k, cache_mem_v, page_table, norm_mix, norm_mem, norm_memkv, norm_ffn, norm_final, ssd_w_in, ssd_conv_w, ssd_conv_b, ssd_dt_bias, ssd_a_log, ssd_d, ssd_norm, ssd_w_out, mla_w_down, mla_q_norm, mla_kv_norm, mla_w_uq, mla_w_uk, mla_w_uv, mla_w_o, mem_w_q, mem_w_kv, mem_w_o, mlp_w_up, mlp_w_down):
    bp, lp, d = x_prompt.shape
    bs, ls, _ = x_sample.shape
    mp, ms = bp * lp, bs * ls
    depth = norm_mix.shape[0]
    n_mem = mem_prompt.shape[1]
    past_len = page_table.shape[1] * cache_mla_latent.shape[2]
    mla_heads = mla_w_uq.shape[2]
    rope = mla_w_uq.shape[3] - QK_NOPE
    ssd_heads = ssd_dt_bias.shape[1]
    d_inner = ssd_heads * SSD_HEADDIM

    x = jnp.concatenate([x_prompt.reshape(mp, d), x_sample.reshape(ms, d)], axis=0)
    mem_rows = mem_prompt.reshape(bp * n_mem, d)
    cmk = cache_mem_k.reshape(depth, bs, n_mem, d)
    cmv = cache_mem_v.reshape(depth, bs, n_mem, d)

    pos = jnp.concatenate([jnp.arange(lp), jnp.tile(past_len + jnp.arange(ls), bs)])
    cos_tab, sin_tab = _rope_tables(pos, rope)
    proj_tm = _row_tile(ms, PROJ_TM)
    assert lp % proj_tm == 0
    npt, tpl = mp // proj_tm, lp // proj_tm
    tab_index = lambda i: jnp.where(i < npt, i % tpl, tpl + (i - npt))

    mem_rows_p = _row_tile(lp, MEM_ROWS)
    mem_nb = _row_tile(bs, MEM_NB)

    p_ssm, p_conv, p_lat, p_rk, p_mk, p_mv = [], [], [], [], [], []
    s_ssm, s_conv, s_lat, s_rk = [], [], [], []
    for i in range(depth):
        j = i // 2
        if i % 2 == 0:
            w_pad = _ssd_w_in_pad(ssd_w_in[j]).astype(BF16)
            zx = _mm(x, w_pad, g=norm_mix[i], tn=7 * V7X_LANES, name="ssd_in")
            prm = _ssd_params(ssd_conv_w[j], ssd_conv_b[j], ssd_dt_bias[j], ssd_a_log[j], ssd_d[j], ssd_norm[j])
            y_p, st_p, ct_p = _ssd(zx, *prm, row_offset=0, nb=bp, L=lp, T=math.gcd(SSD_CHUNK, lp))
            conv0 = jnp.pad(state_conv[j], ((0, 0), (8 - state_conv.shape[2], 0), (0, 0)))
            y_s, st_s, ct_s = _ssd(zx, *prm, row_offset=mp, nb=bs, L=ls, T=math.gcd(SSD_CHUNK, ls),
                                   conv0=conv0, s0=state_ssm[j].reshape(bs, d_inner, D_STATE))
            x = _mm(jnp.concatenate([y_p, y_s], axis=0), ssd_w_out[j].astype(BF16), res=x, name="ssd_out")
            kc = state_conv.shape[2]
            p_ssm.append(st_p.reshape(bp, ssd_heads, SSD_HEADDIM, D_STATE))
            s_ssm.append(st_s.reshape(bs, ssd_heads, SSD_HEADDIM, D_STATE))
            p_conv.append(ct_p[:, 8 - kc:, :])
            s_conv.append(ct_s[:, 8 - kc:, :])
        else:
            w = _mla_weights(mla_w_down[j], mla_q_norm[j], mla_kv_norm[j], mla_w_uq[j], mla_w_uk[j], mla_w_uv[j])
            q, kk, v, lat, kr = _mla_proj(x, norm_mix[i], w, cos_tab, sin_tab, tab_index, tm=proj_tm)
            o_p = _flash(q, kk, v, nb=bp, L=lp)
            q_ext = _headmm(q[mp:], w["wabs"]).reshape(ms * mla_heads, -1)
            o_lat = _decode(q_ext, lat, kr, cache_mla_latent, cache_mla_rope_k, page_table, layer=j, bs=bs, ls=ls,
                            heads=mla_heads, new_row_offset=mp)
            o_s = _headmm(o_lat.reshape(ms, -1), w["wuv_bd"])
            x = _mm(jnp.concatenate([o_p, o_s], axis=0), mla_w_o[j].astype(BF16), res=x, name="mla_out")
            p_lat.append(lat[:mp].reshape(bp, lp, -1))
            s_lat.append(lat[mp:].reshape(bs, ls, -1))
            p_rk.append(kr[:mp].reshape(bp, lp, -1))
            s_rk.append(kr[mp:].reshape(bs, ls, -1))
        kv = _mm(mem_rows, mem_w_kv[i].astype(BF16), g=norm_memkv[i], name="mem_kv")
        kp = kv[:, :d].reshape(bp, n_mem, d)
        vp = kv[:, d:].reshape(bp, n_mem, d)
        p_mk.append(kp.reshape(bp, n_mem, MEM_HEADS, d // MEM_HEADS))
        p_mv.append(vp.reshape(bp, n_mem, MEM_HEADS, d // MEM_HEADS))
        qm = _mm(x, mem_w_q[i].astype(BF16), g=norm_mem[i], name="mem_q")
        steps_per_seq = lp // mem_rows_p
        o_p = _memattn(qm, kp, vp, row_offset=0, n_groups=bp * steps_per_seq, rows=mem_rows_p, nb=1,
                       kv_index=lambda s, n=steps_per_seq: (s // n, 0, 0))
        o_s = _memattn(qm, cmk, cmv, row_offset=mp, n_groups=bs, rows=ls, nb=mem_nb,
                       kv_index=lambda s, i=i: (i, s, 0, 0))
        x = _mm(jnp.concatenate([o_p, o_s], axis=0), mem_w_o[i].astype(BF16), res=x, name="mem_out")
        x = _mlp(x, norm_ffn[i], mlp_w_up[i].astype(BF16), mlp_w_down[i].astype(BF16))
    y = _norm(x, norm_final)
    return (y[:mp].reshape(bp, lp, d), y[mp:].reshape(bs, ls, d),
            jnp.stack(p_ssm), jnp.stack(p_conv), jnp.stack(p_lat), jnp.stack(p_rk), jnp.stack(p_mk), jnp.stack(p_mv),
            jnp.stack(s_ssm), jnp.stack(s_conv), jnp.stack(s_lat), jnp.stack(s_rk))
```

```python
import functools
import math

import jax
import jax.numpy as jnp
from jax import lax
from jax.experimental import pallas as pl
from jax.experimental.pallas import tpu as pltpu

F32 = jnp.float32
BF16 = jnp.bfloat16

EPS = 1e-6
ROPE_THETA = 10000.0

V7X_LANES = 128
V7X_SUBLANES = 8
V7X_VMEM_LIMIT_BYTES = 56 * 1024 * 1024

SSD_HEADDIM = 64
SSD_GROUPS = 8
D_STATE = 128
SSD_CHUNK = 128
QK_NOPE = 64
V_HEAD = 64
MEM_HEADS = 4


def _cparams(*sem):
    return pltpu.CompilerParams(dimension_semantics=sem, vmem_limit_bytes=V7X_VMEM_LIMIT_BYTES)


def _rms(x, g):
    return x * lax.rsqrt(jnp.mean(x * x, axis=-1, keepdims=True) + EPS) * g


def _row_tile(m, pref):
    t = min(pref, m)
    assert m % t == 0, (m, t)
    return t


def _mm_body(*refs, norm, res, act):
    it = iter(refs)
    x_ref = next(it)
    g_ref = next(it) if norm else None
    w_ref = next(it)
    r_ref = next(it) if res else None
    o_ref = next(it)
    xn_ref = next(it)

    @pl.when(pl.program_id(1) == 0)
    def _():
        x = x_ref[...].astype(F32)
        if norm:
            x = _rms(x, g_ref[...])
        xn_ref[...] = x.astype(BF16)

    acc = jnp.dot(xn_ref[...], w_ref[...], preferred_element_type=F32)
    if act == "relu2":
        acc = jnp.square(jnp.maximum(acc, 0.0))
    if res:
        acc = r_ref[...] + acc
    o_ref[...] = acc.astype(o_ref.dtype)


def _mm(x, w, *, g=None, res=None, act=None, out_dtype=F32, tm=1024, tn=1024, name="mm"):
    m, k = x.shape
    k2, n = w.shape
    assert k == k2
    tm = _row_tile(m, tm)
    tn = _row_tile(n, tn)
    in_specs = [pl.BlockSpec((tm, k), lambda i, j: (i, 0))]
    args = [x]
    if g is not None:
        in_specs.append(pl.BlockSpec((1, k), lambda i, j: (0, 0)))
        args.append(g.reshape(1, k).astype(F32))
    in_specs.append(pl.BlockSpec((k, tn), lambda i, j: (0, j)))
    args.append(w)
    if res is not None:
        in_specs.append(pl.BlockSpec((tm, tn), lambda i, j: (i, j)))
        args.append(res)
    return pl.pallas_call(
        functools.partial(_mm_body, norm=g is not None, res=res is not None, act=act),
        out_shape=jax.ShapeDtypeStruct((m, n), out_dtype),
        grid=(m // tm, n // tn),
        in_specs=in_specs,
        out_specs=pl.BlockSpec((tm, tn), lambda i, j: (i, j)),
        scratch_shapes=[pltpu.VMEM((tm, k), BF16)],
        compiler_params=_cparams("parallel", "arbitrary"),
        name=name,
    )(*args)


def _mlp_body(x_ref, g_ref, wu_ref, wd_ref, o_ref, xn_ref, acc_ref, *, final_norm):
    k = pl.program_id(1)

    @pl.when(k == 0)
    def _():
        x = x_ref[...]
        xn_ref[...] = _rms(x, g_ref[...]).astype(BF16)
        acc_ref[...] = x

    h = jnp.dot(xn_ref[...], wu_ref[...], preferred_element_type=F32)
    h = jnp.square(jnp.maximum(h, 0.0)).astype(BF16)
    acc_ref[...] += jnp.dot(h, wd_ref[...], preferred_element_type=F32)

    @pl.when(k == pl.num_programs(1) - 1)
    def _():
        o_ref[...] = acc_ref[...]


def _mlp(x, g, w_up, w_down, *, tm=1024, tf=512):
    m, d = x.shape
    ff = w_up.shape[1]
    tm = _row_tile(m, tm)
    tf = _row_tile(ff, tf)
    return pl.pallas_call(
        functools.partial(_mlp_body, final_norm=False),
        out_shape=jax.ShapeDtypeStruct((m, d), F32),
        grid=(m // tm, ff // tf),
        in_specs=[
            pl.BlockSpec((tm, d), lambda i, k: (i, 0)),
            pl.BlockSpec((1, d), lambda i, k: (0, 0)),
            pl.BlockSpec((d, tf), lambda i, k: (0, k)),
            pl.BlockSpec((tf, d), lambda i, k: (k, 0)),
        ],
        out_specs=pl.BlockSpec((tm, d), lambda i, k: (i, 0)),
        scratch_shapes=[pltpu.VMEM((tm, d), BF16), pltpu.VMEM((tm, d), F32)],
        compiler_params=_cparams("parallel", "arbitrary"),
        name="mlp",
    )(x, g.reshape(1, d).astype(F32), w_up, w_down)


def _norm_body(x_ref, g_ref, o_ref):
    o_ref[...] = _rms(x_ref[...], g_ref[...])


def _norm(x, g, *, tm=1024):
    m, d = x.shape
    tm = _row_tile(m, tm)
    return pl.pallas_call(
        _norm_body,
        out_shape=jax.ShapeDtypeStruct((m, d), F32),
        grid=(m // tm,),
        in_specs=[pl.BlockSpec((tm, d), lambda i: (i, 0)), pl.BlockSpec((1, d), lambda i: (0, 0))],
        out_specs=pl.BlockSpec((tm, d), lambda i: (i, 0)),
        compiler_params=_cparams("parallel"),
        name="final_norm",
    )(x, g.reshape(1, d).astype(F32))


def _memattn_body(q_ref, k_ref, v_ref, o_ref, *, heads, scale):
    dh = q_ref.shape[-1] // heads
    outs = []
    for h in range(heads):
        q = q_ref[:, h * dh:(h + 1) * dh].astype(BF16)
        k = k_ref[:, h * dh:(h + 1) * dh].astype(BF16)
        v = v_ref[:, h * dh:(h + 1) * dh].astype(BF16)
        s = lax.dot_general(q, k, (((1,), (1,)), ((), ())), preferred_element_type=F32) * scale
        s = s - jnp.max(s, axis=-1, keepdims=True)
        p = jnp.exp(s)
        p = p / jnp.sum(p, axis=-1, keepdims=True)
        outs.append(jnp.dot(p.astype(BF16), v, preferred_element_type=F32))
    o_ref[...] = jnp.concatenate(outs, axis=1).astype(o_ref.dtype)


def _memattn(q, k, v, *, n_seq, L, rows):
    d = q.shape[-1]
    n_mem = k.shape[-2]
    per_seq = L // rows
    return pl.pallas_call(
        functools.partial(_memattn_body, heads=MEM_HEADS, scale=1.0 / math.sqrt(d // MEM_HEADS)),
        out_shape=jax.ShapeDtypeStruct((n_seq * L, d), BF16),
        grid=(n_seq * per_seq,),
        in_specs=[pl.BlockSpec((rows, d), lambda s: (s, 0)),
                  pl.BlockSpec((None, n_mem, d), lambda s: (s // per_seq, 0, 0)),
                  pl.BlockSpec((None, n_mem, d), lambda s: (s // per_seq, 0, 0))],
        out_specs=pl.BlockSpec((rows, d), lambda s: (s, 0)),
        compiler_params=_cparams("parallel"),
        name="memattn",
    )(q, k, v)


def _memattn_dec_body(q_ref, k_ref, v_ref, o_ref, *, nb, rows, heads, scale):
    dh = q_ref.shape[-1] // heads
    n_mem = k_ref.shape[1]
    qrow_head = lax.broadcasted_iota(jnp.int32, (heads * rows, n_mem * heads), 0) // rows
    key_head = lax.broadcasted_iota(jnp.int32, (heads * rows, n_mem * heads), 1) % heads
    own = qrow_head == key_head
    outs = []
    for i in range(nb):
        qi = q_ref[i * rows:(i + 1) * rows, :]
        qs = jnp.concatenate([qi[:, h * dh:(h + 1) * dh] for h in range(heads)], axis=0).astype(BF16)
        k2 = k_ref[i].reshape(n_mem * heads, dh).astype(BF16)
        v2 = v_ref[i].reshape(n_mem * heads, dh).astype(BF16)
        s = lax.dot_general(qs, k2, (((1,), (1,)), ((), ())), preferred_element_type=F32) * scale
        s = jnp.where(own, s, -jnp.inf)
        s = s - jnp.max(s, axis=-1, keepdims=True)
        p = jnp.exp(s)
        p = p / jnp.sum(p, axis=-1, keepdims=True)
        o = jnp.dot(p.astype(BF16), v2, preferred_element_type=F32)
        outs.append(jnp.concatenate([o[h * rows:(h + 1) * rows, :] for h in range(heads)], axis=1))
    o_ref[...] = jnp.concatenate(outs, axis=0).astype(o_ref.dtype)


def _memattn_dec(q, k, v, *, layer, row_offset, n_seq, rows, nb):
    d = q.shape[-1]
    _, _, n_mem, heads, dh = k.shape
    blk = nb * rows
    assert row_offset % blk == 0 and n_seq % nb == 0
    kv_spec = pl.BlockSpec((None, nb, n_mem, heads, dh), lambda s: (layer, s, 0, 0, 0))
    return pl.pallas_call(
        functools.partial(_memattn_dec_body, nb=nb, rows=rows, heads=heads, scale=1.0 / math.sqrt(dh)),
        out_shape=jax.ShapeDtypeStruct((n_seq * rows, d), BF16),
        grid=(n_seq // nb,),
        in_specs=[pl.BlockSpec((blk, d), lambda s: (row_offset // blk + s, 0)), kv_spec, kv_spec],
        out_specs=pl.BlockSpec((blk, d), lambda s: (s, 0)),
        compiler_params=_cparams("parallel"),
        name="memattn_dec",
    )(q, k, v)


def _split3_lanes(v, heads):
    lane = lax.broadcasted_iota(jnp.int32, v.shape, 1)
    hi = v.astype(BF16).astype(F32)
    r1 = v - hi
    mid = r1.astype(BF16).astype(F32)
    lo = r1 - mid
    out = jnp.where(lane < heads, hi, jnp.where(lane < 2 * heads, mid, jnp.where(lane < 3 * heads, lo, 0.0)))
    return out.astype(BF16)


def _ssd_body(z_ref, x_ref, bc_ref, dt_ref, cw_ref, cb_ref, dtb_ref, alog_ref, dsk_ref, ng_ref, e3_ref,
              conv0_ref, s0_ref,
              y_ref, sfin_ref, ctail_ref,
              extx_ref, extbc_ref, st_ref, *, T, heads, groups, has_init):
    c = pl.program_id(1)
    nc = pl.num_programs(1)
    d_inner = x_ref.shape[-1]
    hpg = heads // groups
    gw = d_inner // groups
    hd = d_inner // heads
    n = D_STATE

    @pl.when(c == 0)
    def _():
        if has_init:
            extx_ref[0:8, :] = conv0_ref[0, :, 0:d_inner]
            extbc_ref[0:8, :] = conv0_ref[0, :, d_inner:]
            st_ref[...] = s0_ref[0].T
        else:
            extx_ref[0:8, :] = jnp.zeros((8, d_inner), F32)
            extbc_ref[0:8, :] = jnp.zeros((8, extbc_ref.shape[-1]), F32)
            st_ref[...] = jnp.zeros(st_ref.shape, F32)

    extx_ref[8:8 + T, :] = x_ref[...]
    extbc_ref[8:8 + T, :] = bc_ref[...]

    def conv(ext_ref, lo, width):
        acc = cb_ref[:, lo:lo + width] + ext_ref[5:5 + T, :] * cw_ref[0:1, lo:lo + width]
        for k in range(1, 4):
            acc = acc + ext_ref[5 + k:5 + k + T, :] * cw_ref[k:k + 1, lo:lo + width]
        return acc * jax.nn.sigmoid(acc)

    xc = conv(extx_ref, 0, d_inner)
    bcc = conv(extbc_ref, d_inner, extbc_ref.shape[-1])
    tail_x = extx_ref[T:T + 8, :]
    tail_bc = extbc_ref[T:T + 8, :]
    extx_ref[0:8, :] = tail_x
    extbc_ref[0:8, :] = tail_bc

    @pl.when(c == nc - 1)
    def _():
        ctail_ref[0, :, 0:d_inner] = tail_x
        ctail_ref[0, :, d_inner:] = tail_bc

    dtr = dt_ref[...] + dtb_ref[...]
    dtv = jnp.maximum(dtr, 0.0) + jnp.log1p(jnp.exp(-jnp.abs(dtr)))
    a = dtv * (-jnp.exp(alog_ref[...]))
    row = lax.broadcasted_iota(jnp.int32, a.shape, 0)
    acs = a
    sh = 1
    while sh < T:
        acs = acs + jnp.where(row >= sh, pltpu.roll(acs, sh, axis=0), 0.0)
        sh *= 2
    a_last = acs[T - 1:T, :]
    e3 = e3_ref[...]

    def expand(v):
        return jnp.dot(_split3_lanes(v, heads), e3, preferred_element_type=F32)

    dt_e = expand(dtv)
    dend_e = expand(jnp.exp(a_last - acs))
    eacs_e = expand(jnp.exp(acs))
    xdt = xc * dt_e
    xdt_b = xdt.astype(BF16)
    xw_b = (xdt * dend_e).astype(BF16)
    cdec_e = eacs_e[T - 1:T, :]

    if T < V7X_LANES:
        acs_p = jnp.concatenate([acs, jnp.zeros((V7X_LANES - T, V7X_LANES), F32)], axis=0)
    else:
        acs_p = acs
    acs_t = acs_p.T
    ti = lax.broadcasted_iota(jnp.int32, (T, T), 0)
    si = lax.broadcasted_iota(jnp.int32, (T, T), 1)
    causal = ti >= si
    lane_g = lax.broadcasted_iota(jnp.int32, (T, gw), 1)

    z = z_ref[...]
    gate = z * jax.nn.sigmoid(z)
    for g in range(groups):
        bg = bcc[:, g * n:(g + 1) * n]
        cg = bcc[:, groups * n + g * n: groups * n + (g + 1) * n].astype(BF16)
        bg_b = bg.astype(BF16)
        gmat = lax.dot_general(cg, bg_b, (((1,), (1,)), ((), ())), preferred_element_type=F32)
        st_g = st_ref[:, g * gw:(g + 1) * gw]
        y_g = jnp.dot(cg, st_g.astype(BF16), preferred_element_type=F32) * eacs_e[:, g * gw:(g + 1) * gw]
        xg = xdt_b[:, g * gw:(g + 1) * gw]
        for j in range(hpg):
            h = g * hpg + j
            seg = acs[:, h:h + 1] - acs_t[h:h + 1, 0:T]
            decay = jnp.exp(jnp.where(causal, seg, -jnp.inf))
            m_h = (gmat * decay).astype(BF16)
            x_h = jnp.where((lane_g >= j * hd) & (lane_g < (j + 1) * hd), xg, jnp.zeros_like(xg))
            y_g = y_g + jnp.dot(m_h, x_h, preferred_element_type=F32)
        cs_t = jnp.dot(bg.T.astype(BF16), xw_b[:, g * gw:(g + 1) * gw], preferred_element_type=F32)
        st_ref[:, g * gw:(g + 1) * gw] = st_g * cdec_e[:, g * gw:(g + 1) * gw] + cs_t
        y_g = y_g + xc[:, g * gw:(g + 1) * gw] * dsk_ref[:, g * gw:(g + 1) * gw]
        y_g = y_g * gate[:, g * gw:(g + 1) * gw]
        y_g = _rms(y_g, ng_ref[:, g * gw:(g + 1) * gw])
        y_ref[:, g * gw:(g + 1) * gw] = y_g.astype(y_ref.dtype)

    @pl.when(c == nc - 1)
    def _():
        sfin_ref[0] = st_ref[...].T


def _ssd(zx, conv_w, conv_b, dt_bias_rep, a_log_rep, d_exp, norm_g, e3, *, row_offset, nb, L, T,
         conv0=None, s0=None, layer=0):
    heads = d_exp.shape[-1] // SSD_HEADDIM
    d_inner = d_exp.shape[-1]
    bcw = 2 * SSD_GROUPS * D_STATE
    assert bcw == d_inner, "column blocks are indexed in units of d_inner"
    nc = L // T
    assert row_offset % T == 0
    rb0 = row_offset // T
    has_init = conv0 is not None
    rowmap = lambda col: (lambda b, c: (rb0 + b * nc + c, col))
    const = lambda b, c: (0, 0)
    per_b = lambda b, c: (b, 0, 0)
    if not has_init:
        conv0 = jnp.zeros((1, 8, d_inner + bcw), F32)
        s0 = jnp.zeros((1, 8, D_STATE), F32)
        init_specs = [pl.BlockSpec((1, 8, d_inner + bcw), lambda b, c: (0, 0, 0)),
                      pl.BlockSpec((1, 8, D_STATE), lambda b, c: (0, 0, 0))]
    else:
        init_specs = [pl.BlockSpec((1, 8, d_inner + bcw), per_b),
                      pl.BlockSpec((None, 1, d_inner, D_STATE), lambda b, c: (layer, b, 0, 0))]
    dt_col = (2 * d_inner + bcw) // V7X_LANES
    y, sfin, ctail = pl.pallas_call(
        functools.partial(_ssd_body, T=T, heads=heads, groups=SSD_GROUPS, has_init=has_init),
        out_shape=(jax.ShapeDtypeStruct((nb * L, d_inner), BF16),
                   jax.ShapeDtypeStruct((nb, d_inner, D_STATE), F32),
                   jax.ShapeDtypeStruct((nb, 8, d_inner + bcw), F32)),
        grid=(nb, nc),
        in_specs=[
            pl.BlockSpec((T, d_inner), rowmap(0)),
            pl.BlockSpec((T, d_inner), rowmap(1)),
            pl.BlockSpec((T, bcw), rowmap(2)),
            pl.BlockSpec((T, V7X_LANES), rowmap(dt_col)),
            pl.BlockSpec((4, d_inner + bcw), const),
            pl.BlockSpec((1, d_inner + bcw), const),
            pl.BlockSpec((1, V7X_LANES), const),
            pl.BlockSpec((1, V7X_LANES), const),
            pl.BlockSpec((1, d_inner), const),
            pl.BlockSpec((1, d_inner), const),
            pl.BlockSpec((V7X_LANES, d_inner), const),
        ] + init_specs,
        out_specs=(pl.BlockSpec((T, d_inner), lambda b, c: (b * nc + c, 0)),
                   pl.BlockSpec((1, d_inner, D_STATE), per_b),
                   pl.BlockSpec((1, 8, d_inner + bcw), per_b)),
        scratch_shapes=[pltpu.VMEM((T + 8, d_inner), F32), pltpu.VMEM((T + 8, bcw), F32),
                        pltpu.VMEM((D_STATE, d_inner), F32)],
        compiler_params=_cparams("parallel", "arbitrary"),
        name="ssd",
    )(zx, zx, zx, zx, conv_w, conv_b, dt_bias_rep, a_log_rep, d_exp, norm_g, e3, conv0, s0)
    return y, sfin, ctail


def _ssd_w_in_pad(w_in):
    heads = w_in.shape[1] - (w_in.shape[1] // V7X_LANES) * V7X_LANES
    main = w_in[:, :w_in.shape[1] - heads]
    dt = w_in[:, w_in.shape[1] - heads:]
    return jnp.concatenate([main] + [dt] * (V7X_LANES // heads), axis=1)


def _ssd_params(conv_w, conv_b, dt_bias, a_log, d_skip, norm_g):
    heads = dt_bias.shape[0]
    rep = V7X_LANES // heads
    d_inner = heads * SSD_HEADDIM
    src = jnp.arange(V7X_LANES)[:, None]
    dst_head = jnp.arange(d_inner)[None, :] // SSD_HEADDIM
    e3 = ((src % heads == dst_head) & (src < 3 * heads)).astype(BF16)
    return (conv_w.astype(F32), conv_b.reshape(1, -1).astype(F32),
            jnp.tile(dt_bias.reshape(1, heads), (1, rep)).astype(F32),
            jnp.tile(a_log.reshape(1, heads), (1, rep)).astype(F32),
            jnp.repeat(d_skip, SSD_HEADDIM).reshape(1, d_inner).astype(F32),
            norm_g.reshape(1, d_inner).astype(F32), e3)


SLOT = V7X_LANES


def _mla_proj_body(x_ref, g_ref, wd_ref, qn_ref, kvn_ref, wuq_ref, wuqr_ref, wuk_ref, wuv_ref, cos_ref, sin_ref,
                   q_ref, kk_ref, v_ref, lat_ref, kr_ref, *, heads, q_lora, kv_lora, rope, scale):
    xn = _rms(x_ref[...], g_ref[...]).astype(BF16)
    down = jnp.dot(xn, wd_ref[...], preferred_element_type=F32)
    cq = down[:, :q_lora]
    ckv = down[:, q_lora:q_lora + kv_lora]
    krs = down[:, q_lora + kv_lora:q_lora + kv_lora + SLOT]
    krr = down[:, q_lora + kv_lora + SLOT:]
    cos = cos_ref[...]
    sin = sin_ref[...]
    cqn = _rms(cq, qn_ref[...]).astype(BF16)
    qp = jnp.dot(cqn, wuq_ref[...], preferred_element_type=F32)
    qr = jnp.dot(cqn, wuqr_ref[...], preferred_element_type=F32)
    lat = _rms(ckv, kvn_ref[...])
    lat_ref[...] = lat
    kr_rot = krs * cos + krr * sin
    kr_ref[...] = kr_rot[:, QK_NOPE:QK_NOPE + rope]
    latb = lat.astype(BF16)
    kn = jnp.dot(latb, wuk_ref[...], preferred_element_type=F32)
    for h in range(heads):
        sl = slice(h * SLOT, (h + 1) * SLOT)
        q_ref[:, sl] = ((qp[:, sl] * cos + qr[:, sl] * sin) * scale).astype(BF16)
        kk_ref[:, sl] = (kn[:, sl] + kr_rot).astype(BF16)
    v_ref[...] = jnp.dot(latb, wuv_ref[...], preferred_element_type=F32).astype(BF16)


def _mla_proj(x, g, w, cos_tab, sin_tab, tab_index, *, tm=512):
    m, d = x.shape
    heads = w["wuq"].shape[1] // SLOT
    q_lora = w["qn"].shape[-1]
    kv_lora = w["kvn"].shape[-1]
    rope = w["rope"]
    tm = _row_tile(m, tm)
    const = lambda i: (0, 0)
    rowb = lambda i: (i, 0)
    full = lambda a: pl.BlockSpec(a.shape, const)
    return pl.pallas_call(
        functools.partial(_mla_proj_body, heads=heads, q_lora=q_lora, kv_lora=kv_lora, rope=rope,
                          scale=w["scale"]),
        out_shape=(jax.ShapeDtypeStruct((m, heads * SLOT), BF16),
                   jax.ShapeDtypeStruct((m, heads * SLOT), BF16),
                   jax.ShapeDtypeStruct((m, heads * V_HEAD), BF16),
                   jax.ShapeDtypeStruct((m, kv_lora), F32),
                   jax.ShapeDtypeStruct((m, rope), F32)),
        grid=(m // tm,),
        in_specs=[pl.BlockSpec((tm, d), rowb), pl.BlockSpec((1, d), const),
                  full(w["wd"]), full(w["qn"]), full(w["kvn"]), full(w["wuq"]), full(w["wuqr"]),
                  full(w["wuk"]), full(w["wuv"]),
                  pl.BlockSpec((tm, SLOT), lambda i: (tab_index(i), 0)),
                  pl.BlockSpec((tm, SLOT), lambda i: (tab_index(i), 0))],
        out_specs=(pl.BlockSpec((tm, heads * SLOT), rowb), pl.BlockSpec((tm, heads * SLOT), rowb),
                   pl.BlockSpec((tm, heads * V_HEAD), rowb), pl.BlockSpec((tm, kv_lora), rowb),
                   pl.BlockSpec((tm, rope), rowb)),
        compiler_params=_cparams("parallel"),
        name="mla_proj",
    )(x, g.reshape(1, d).astype(F32), w["wd"], w["qn"], w["kvn"], w["wuq"], w["wuqr"], w["wuk"], w["wuv"],
      cos_tab, sin_tab)


def _mla_weights(w_down, q_norm, kv_norm, w_uq, w_uk, w_uv):
    q_lora = q_norm.shape[0]
    kv_lora = kv_norm.shape[0]
    heads = w_uq.shape[1]
    qk = w_uq.shape[2]
    rope = qk - QK_NOPE
    half = rope // 2
    pad = SLOT - qk

    def slot_pair(wr):
        z_lo = jnp.zeros(wr.shape[:-1] + (QK_NOPE,), wr.dtype)
        z_hi = jnp.zeros(wr.shape[:-1] + (pad,), wr.dtype)
        plain = jnp.concatenate([z_lo, wr, z_hi], axis=-1)
        rot = jnp.concatenate([z_lo, -wr[..., half:], wr[..., :half], z_hi], axis=-1)
        return plain, rot

    w_kr = w_down[:, q_lora + kv_lora:]
    kr_plain, kr_rot = slot_pair(w_kr)
    wd = jnp.concatenate([w_down[:, :q_lora + kv_lora], kr_plain, kr_rot], axis=1)
    uq_nope = jnp.concatenate([w_uq[..., :QK_NOPE], jnp.zeros(w_uq.shape[:2] + (SLOT - QK_NOPE,), w_uq.dtype)], -1)
    uq_plain, uq_rot = slot_pair(w_uq[..., QK_NOPE:])
    wuq = (uq_nope + uq_plain).reshape(q_lora, heads * SLOT)
    wuqr = uq_rot.reshape(q_lora, heads * SLOT)
    wuk = jnp.concatenate([w_uk, jnp.zeros(w_uk.shape[:2] + (SLOT - QK_NOPE,), w_uk.dtype)], -1)
    wuk = wuk.reshape(kv_lora, heads * SLOT)
    wuv = w_uv.reshape(kv_lora, heads * V_HEAD)
    eye_r = jnp.zeros((SLOT, SLOT), w_uk.dtype).at[QK_NOPE + jnp.arange(rope), jnp.arange(rope)].set(1.0)
    uk_t = jnp.transpose(w_uk, (1, 2, 0))
    uk_t = jnp.concatenate([uk_t, jnp.zeros((heads, SLOT - QK_NOPE, kv_lora), w_uk.dtype)], axis=1)
    wabs = jnp.concatenate([uk_t, jnp.broadcast_to(eye_r, (heads, SLOT, SLOT))], axis=2)
    uv = jnp.transpose(w_uv, (1, 0, 2)).reshape(heads // 2, 2, kv_lora, V_HEAD)
    z = jnp.zeros((heads // 2, kv_lora, V_HEAD), w_uv.dtype)
    wuv_bd = jnp.concatenate([jnp.concatenate([uv[:, 0], z], axis=2), jnp.concatenate([z, uv[:, 1]], axis=2)], axis=1)
    return dict(wd=wd.astype(BF16), qn=q_norm.reshape(1, -1).astype(F32), kvn=kv_norm.reshape(1, -1).astype(F32),
                wuq=wuq.astype(BF16), wuqr=wuqr.astype(BF16), wuk=wuk.astype(BF16), wuv=wuv.astype(BF16),
                wabs=wabs.astype(BF16), wuv_bd=wuv_bd.astype(BF16), rope=rope, scale=1.0 / math.sqrt(qk))


def _rope_tables(positions, rope):
    half = rope // 2
    inv = ROPE_THETA ** (-jnp.arange(half, dtype=F32) * (2.0 / rope))
    ang = positions.astype(F32)[:, None] * inv[None, :]
    c, s = jnp.cos(ang), jnp.sin(ang)
    p = positions.shape[0]
    ones = jnp.ones((p, QK_NOPE), F32)
    zeros = jnp.zeros((p, QK_NOPE), F32)
    zpad = jnp.zeros((p, SLOT - QK_NOPE - rope), F32)
    return (jnp.concatenate([ones, c, c, zpad], axis=1), jnp.concatenate([zeros, s, s, zpad], axis=1))


def _headmm_body(x_ref, w_ref, o_ref):
    o_ref[...] = jnp.dot(x_ref[...], w_ref[...], preferred_element_type=F32).astype(o_ref.dtype)


def _headmm(x, w, *, out_dtype=BF16):
    m = x.shape[0]
    g, kb, nb = w.shape
    assert x.shape[1] == g * kb
    return pl.pallas_call(
        _headmm_body,
        out_shape=jax.ShapeDtypeStruct((m, g * nb), out_dtype),
        grid=(g,),
        in_specs=[pl.BlockSpec((m, kb), lambda h: (0, h)), pl.BlockSpec((None, kb, nb), lambda h: (h, 0, 0))],
        out_specs=pl.BlockSpec((m, nb), lambda h: (0, h)),
        compiler_params=_cparams("parallel"),
        name="headmm",
    )(x, w)


def _flash_body(qi_ref, ki_ref, q_ref, k_ref, v_ref, o_ref, m_ref, l_ref, acc_ref, *, heads, tq):
    p_idx = pl.program_id(1)
    qi = qi_ref[p_idx]
    ki = ki_ref[p_idx]

    @pl.when(ki == 0)
    def _():
        m_ref[...] = jnp.full(m_ref.shape, -jnp.inf, F32)
        l_ref[...] = jnp.zeros(l_ref.shape, F32)
        acc_ref[...] = jnp.zeros(acc_ref.shape, F32)

    def step(masked):
        if masked:
            ri = lax.broadcasted_iota(jnp.int32, (tq, tq), 0)
            ci = lax.broadcasted_iota(jnp.int32, (tq, tq), 1)
            keep = ri >= ci
        lane = lax.broadcasted_iota(jnp.int32, (tq, SLOT), 1)
        even = lane < V_HEAD
        for hp in range(heads // 2):
            vp = v_ref[:, hp * SLOT:(hp + 1) * SLOT]
            pv = None
            alphas = []
            for e in range(2):
                h = 2 * hp + e
                q = q_ref[:, h * SLOT:(h + 1) * SLOT]
                k = k_ref[:, h * SLOT:(h + 1) * SLOT]
                s = lax.dot_general(q, k, (((1,), (1,)), ((), ())), preferred_element_type=F32)
                if masked:
                    s = jnp.where(keep, s, -jnp.inf)
                m_prev = m_ref[h]
                m_new = jnp.maximum(m_prev, jnp.max(s, axis=-1, keepdims=True))
                alpha = jnp.exp(m_prev - m_new)
                p = jnp.exp(s - m_new[:, 0:1])
                l_ref[h] = alpha * l_ref[h] + jnp.sum(p, axis=-1, keepdims=True)
                m_ref[h] = m_new
                v_h = jnp.where(even if e == 0 else ~even, vp, jnp.zeros_like(vp))
                d = jnp.dot(p.astype(BF16), v_h, preferred_element_type=F32)
                pv = d if pv is None else pv + d
                alphas.append(alpha)
            a_pair = jnp.where(even, alphas[0], alphas[1])
            acc_ref[:, hp * SLOT:(hp + 1) * SLOT] = acc_ref[:, hp * SLOT:(hp + 1) * SLOT] * a_pair + pv

    @pl.when(ki < qi)
    def _():
        step(False)

    @pl.when(ki == qi)
    def _():
        step(True)
        lane = lax.broadcasted_iota(jnp.int32, (tq, SLOT), 1)
        even = lane < V_HEAD
        for hp in range(heads // 2):
            l_pair = jnp.where(even, l_ref[2 * hp], l_ref[2 * hp + 1])
            o_ref[:, hp * SLOT:(hp + 1) * SLOT] = (acc_ref[:, hp * SLOT:(hp + 1) * SLOT] / l_pair).astype(o_ref.dtype)


def _flash(q, kk, v, *, nb, L, tq=512):
    heads = q.shape[1] // SLOT
    tq = _row_tile(L, tq)
    nq = L // tq
    pairs = [(i, j) for i in range(nq) for j in range(i + 1)]
    qi_tab = jnp.asarray([p[0] for p in pairs], jnp.int32)
    ki_tab = jnp.asarray([p[1] for p in pairs], jnp.int32)
    grid_spec = pltpu.PrefetchScalarGridSpec(
        num_scalar_prefetch=2,
        grid=(nb, len(pairs)),
        in_specs=[
            pl.BlockSpec((tq, heads * SLOT), lambda b, p, qi, ki: (b * nq + qi[p], 0)),
            pl.BlockSpec((tq, heads * SLOT), lambda b, p, qi, ki: (b * nq + ki[p], 0)),
            pl.BlockSpec((tq, heads * V_HEAD), lambda b, p, qi, ki: (b * nq + ki[p], 0)),
        ],
        out_specs=pl.BlockSpec((tq, heads * V_HEAD), lambda b, p, qi, ki: (b * nq + qi[p], 0)),
        scratch_shapes=[pltpu.VMEM((heads, tq, SLOT), F32), pltpu.VMEM((heads, tq, SLOT), F32),
                        pltpu.VMEM((tq, heads * V_HEAD), F32)],
    )
    return pl.pallas_call(
        functools.partial(_flash_body, heads=heads, tq=tq),
        out_shape=jax.ShapeDtypeStruct((nb * L, heads * V_HEAD), BF16),
        grid_spec=grid_spec,
        compiler_params=_cparams("parallel", "arbitrary"),
        name="mla_flash",
    )(qi_tab, ki_tab, q, kk, v)


def _decode_body(pt_ref, q_ref, cnew_ref, rnew_ref, lat_hbm, ropet_hbm, o_ref,
                 cbuf, rbuf, cb, sbuf, sem, *, layer, n_pages, page, chunk_pages, kv_lora, rope, heads, ls):
    b = pl.program_id(0)
    nb = pl.num_programs(0)
    slot = lax.rem(b, 2)
    rows = q_ref.shape[0]
    chunk = chunk_pages * page

    def page_copies(seq, sl):
        cps = []
        for i in range(n_pages):
            pg = pt_ref[seq, i]
            cps.append(pltpu.make_async_copy(lat_hbm.at[layer, pg], cbuf.at[sl, pl.ds(i * page, page), :], sem.at[0, sl]))
            cps.append(pltpu.make_async_copy(ropet_hbm.at[layer, pg], rbuf.at[sl, i], sem.at[1, sl]))
        return cps

    @pl.when(b == 0)
    def _():
        for cp in page_copies(0, 0):
            cp.start()

    @pl.when(b + 1 < nb)
    def _():
        for cp in page_copies(b + 1, 1 - slot):
            cp.start()

    for cp in page_copies(b, slot):
        cp.wait()

    q_lat = q_ref[:, :kv_lora]
    q_r = q_ref[:, kv_lora:kv_lora + rope]
    nt = (((1,), (1,)), ((), ()))
    n_chunks = n_pages // chunk_pages
    for ck in range(n_chunks):
        c_b = cbuf[slot, ck * chunk:(ck + 1) * chunk, :].astype(BF16)
        cb[ck * chunk:(ck + 1) * chunk, :] = c_b
        r_b = jnp.concatenate([rbuf[slot, ck * chunk_pages + i].astype(BF16) for i in range(chunk_pages)],
                              axis=1)
        sbuf[:, ck * chunk:(ck + 1) * chunk] = (
            lax.dot_general(q_lat, c_b, nt, preferred_element_type=F32)
            + jnp.dot(q_r, r_b, preferred_element_type=F32))

    cn = cnew_ref[...].astype(BF16)
    rn = rnew_ref[...].astype(BF16)
    sn = (lax.dot_general(q_lat, cn, nt, preferred_element_type=F32)
          + lax.dot_general(q_r, rn, nt, preferred_element_type=F32))
    qpos = lax.broadcasted_iota(jnp.int32, (rows, ls), 0) // heads
    kpos = lax.broadcasted_iota(jnp.int32, (rows, ls), 1)
    sn = jnp.where(qpos >= kpos, sn, -jnp.inf)

    m = jnp.maximum(jnp.max(sbuf[...], axis=-1, keepdims=True), jnp.max(sn, axis=-1, keepdims=True))
    pn = jnp.exp(sn - m)
    l = jnp.sum(pn, axis=-1, keepdims=True)
    acc = jnp.dot(pn.astype(BF16), cn, preferred_element_type=F32)
    for ck in range(n_chunks):
        p = jnp.exp(sbuf[:, ck * chunk:(ck + 1) * chunk] - m)
        l = l + jnp.sum(p, axis=-1, keepdims=True)
        acc = acc + jnp.dot(p.astype(BF16), cb[ck * chunk:(ck + 1) * chunk, :], preferred_element_type=F32)
    o_ref[...] = (acc / l).astype(o_ref.dtype)


def _decode(q_ext, lat_new, kr_new, lat_pool, ropet_pool, page_table, *, layer, bs, ls, heads, new_row_offset,
            chunk_pages=16):
    kv_lora = lat_new.shape[1]
    rope = kr_new.shape[1]
    page = lat_pool.shape[2]
    n_pages = page_table.shape[1]
    chunk_pages = math.gcd(chunk_pages, n_pages)
    assert new_row_offset % ls == 0
    rows = ls * heads
    qw = q_ext.shape[1]
    grid_spec = pltpu.PrefetchScalarGridSpec(
        num_scalar_prefetch=1,
        grid=(bs,),
        in_specs=[pl.BlockSpec((rows, qw), lambda b, pt: (b, 0)),
                  pl.BlockSpec((ls, kv_lora), lambda b, pt: (new_row_offset // ls + b, 0)),
                  pl.BlockSpec((ls, rope), lambda b, pt: (new_row_offset // ls + b, 0)),
                  pl.BlockSpec(memory_space=pl.ANY),
                  pl.BlockSpec(memory_space=pl.ANY)],
        out_specs=pl.BlockSpec((rows, kv_lora), lambda b, pt: (b, 0)),
        scratch_shapes=[pltpu.VMEM((2, n_pages * page, kv_lora), F32),
                        pltpu.VMEM((2, n_pages, rope, page), F32),
                        pltpu.VMEM((n_pages * page, kv_lora), BF16),
                        pltpu.VMEM((rows, n_pages * page), F32),
                        pltpu.SemaphoreType.DMA((2, 2))],
    )
    return pl.pallas_call(
        functools.partial(_decode_body, layer=layer, n_pages=n_pages, page=page, chunk_pages=chunk_pages,
                          kv_lora=kv_lora, rope=rope, heads=heads, ls=ls),
        out_shape=jax.ShapeDtypeStruct((bs * rows, kv_lora), BF16),
        grid_spec=grid_spec,
        compiler_params=_cparams("arbitrary"),
        name="mla_decode",
    )(page_table, q_ext, lat_new, kr_new, lat_pool, ropet_pool)


MEM_ROWS = 1024
MEM_NB = 4
PROJ_TM = 512


def kernel(x_prompt, x_sample, mem_prompt, state_ssm, state_conv, cache_mla_latent, cache_mla_rope_k, cache_mem_k, cache_mem_v, page_table, norm_mix, norm_mem, norm_memkv, norm_ffn, norm_final, ssd_w_in, ssd_conv_w, ssd_conv_b, ssd_dt_bias, ssd_a_log, ssd_d, ssd_norm, ssd_w_out, mla_w_down, mla_q_norm, mla_kv_norm, mla_w_uq, mla_w_uk, mla_w_uv, mla_w_o, mem_w_q, mem_w_kv, mem_w_o, mlp_w_up, mlp_w_down):
    bp, lp, d = x_prompt.shape
    bs, ls, _ = x_sample.shape
    mp, ms = bp * lp, bs * ls
    depth = norm_mix.shape[0]
    n_mem = mem_prompt.shape[1]
    past_len = page_table.shape[1] * cache_mla_latent.shape[2]
    mla_heads = mla_w_uq.shape[2]
    rope = mla_w_uq.shape[3] - QK_NOPE
    ssd_heads = ssd_dt_bias.shape[1]
    d_inner = ssd_heads * SSD_HEADDIM

    x = jnp.concatenate([x_prompt.reshape(mp, d), x_sample.reshape(ms, d)], axis=0)
    mem_rows = mem_prompt.reshape(bp * n_mem, d)
    ropet_pool = jnp.swapaxes(cache_mla_rope_k, 2, 3)
    ssm0 = state_ssm.reshape(state_ssm.shape[0], bs, d_inner, D_STATE)

    pos = jnp.concatenate([jnp.arange(lp), jnp.tile(past_len + jnp.arange(ls), bs)])
    cos_tab, sin_tab = _rope_tables(pos, rope)
    proj_tm = _row_tile(ms, PROJ_TM)
    assert lp % proj_tm == 0
    npt, tpl = mp // proj_tm, lp // proj_tm
    tab_index = lambda i: jnp.where(i < npt, i % tpl, tpl + (i - npt))

    mem_rows_p = _row_tile(lp, MEM_ROWS)
    mem_nb = _row_tile(bs, MEM_NB)

    p_ssm, p_conv, p_lat, p_rk, p_mk, p_mv = [], [], [], [], [], []
    s_ssm, s_conv, s_lat, s_rk = [], [], [], []
    for i in range(depth):
        j = i // 2
        if i % 2 == 0:
            w_pad = _ssd_w_in_pad(ssd_w_in[j]).astype(BF16)
            zx = _mm(x, w_pad, g=norm_mix[i], tn=7 * V7X_LANES, name="ssd_in")
            prm = _ssd_params(ssd_conv_w[j], ssd_conv_b[j], ssd_dt_bias[j], ssd_a_log[j], ssd_d[j], ssd_norm[j])
            y_p, st_p, ct_p = _ssd(zx, *prm, row_offset=0, nb=bp, L=lp, T=math.gcd(SSD_CHUNK, lp))
            conv0 = jnp.pad(state_conv[j], ((0, 0), (8 - state_conv.shape[2], 0), (0, 0)))
            y_s, st_s, ct_s = _ssd(zx, *prm, row_offset=mp, nb=bs, L=ls, T=math.gcd(SSD_CHUNK, ls),
                                   conv0=conv0, s0=ssm0, layer=j)
            x = _mm(jnp.concatenate([y_p, y_s], axis=0), ssd_w_out[j].astype(BF16), res=x, name="ssd_out")
            kc = state_conv.shape[2]
            p_ssm.append(st_p.reshape(bp, ssd_heads, SSD_HEADDIM, D_STATE))
            s_ssm.append(st_s.reshape(bs, ssd_heads, SSD_HEADDIM, D_STATE))
            p_conv.append(ct_p[:, 8 - kc:, :])
            s_conv.append(ct_s[:, 8 - kc:, :])
        else:
            w = _mla_weights(mla_w_down[j], mla_q_norm[j], mla_kv_norm[j], mla_w_uq[j], mla_w_uk[j], mla_w_uv[j])
            q, kk, v, lat, kr = _mla_proj(x, norm_mix[i], w, cos_tab, sin_tab, tab_index, tm=proj_tm)
            o_p = _flash(q, kk, v, nb=bp, L=lp)
            q_ext = _headmm(q[mp:], w["wabs"]).reshape(ms * mla_heads, -1)
            o_lat = _decode(q_ext, lat, kr, cache_mla_latent, ropet_pool, page_table, layer=j, bs=bs, ls=ls,
                            heads=mla_heads, new_row_offset=mp)
            o_s = _headmm(o_lat.reshape(ms, -1), w["wuv_bd"])
            x = _mm(jnp.concatenate([o_p, o_s], axis=0), mla_w_o[j].astype(BF16), res=x, name="mla_out")
            p_lat.append(lat[:mp].reshape(bp, lp, -1))
            s_lat.append(lat[mp:].reshape(bs, ls, -1))
            p_rk.append(kr[:mp].reshape(bp, lp, -1))
            s_rk.append(kr[mp:].reshape(bs, ls, -1))
        kv = _mm(mem_rows, mem_w_kv[i].astype(BF16), g=norm_memkv[i], name="mem_kv")
        kp = kv[:, :d].reshape(bp, n_mem, d)
        vp = kv[:, d:].reshape(bp, n_mem, d)
        p_mk.append(kp.reshape(bp, n_mem, MEM_HEADS, d // MEM_HEADS))
        p_mv.append(vp.reshape(bp, n_mem, MEM_HEADS, d // MEM_HEADS))
        qm = _mm(x, mem_w_q[i].astype(BF16), g=norm_mem[i], name="mem_q")
        o_p = _memattn(qm, kp, vp, n_seq=bp, L=lp, rows=mem_rows_p)
        o_s = _memattn_dec(qm, cache_mem_k, cache_mem_v, layer=i, row_offset=mp, n_seq=bs, rows=ls, nb=mem_nb)
        x = _mm(jnp.concatenate([o_p, o_s], axis=0), mem_w_o[i].astype(BF16), res=x, name="mem_out")
        x = _mlp(x, norm_ffn[i], mlp_w_up[i].astype(BF16), mlp_w_down[i].astype(BF16))
    y = _norm(x, norm_final)
    return (y[:mp].reshape(bp, lp, d), y[mp:].reshape(bs, ls, d),
            jnp.stack(p_ssm), jnp.stack(p_conv), jnp.stack(p_lat), jnp.stack(p_rk), jnp.stack(p_mk), jnp.stack(p_mv),
            jnp.stack(s_ssm), jnp.stack(s_conv), jnp.stack(s_lat), jnp.stack(s_rk))
```

```python
import functools
import math

import jax
import jax.numpy as jnp
from jax import lax
from jax.experimental import pallas as pl
from jax.experimental.pallas import tpu as pltpu

F32 = jnp.float32
BF16 = jnp.bfloat16

EPS = 1e-6
ROPE_THETA = 10000.0

V7X_LANES = 128
V7X_SUBLANES = 8
V7X_VMEM_LIMIT_BYTES = 56 * 1024 * 1024

SSD_HEADDIM = 64
SSD_GROUPS = 8
D_STATE = 128
SSD_CHUNK = 128
QK_NOPE = 64
V_HEAD = 64
MEM_HEADS = 4


def _cparams(*sem):
    return pltpu.CompilerParams(dimension_semantics=sem, vmem_limit_bytes=V7X_VMEM_LIMIT_BYTES)


def _rms(x, g):
    return x * lax.rsqrt(jnp.mean(x * x, axis=-1, keepdims=True) + EPS) * g


def _row_tile(m, pref):
    t = min(pref, m)
    assert m % t == 0, (m, t)
    return t


def _mm_body(*refs, norm, res, act):
    it = iter(refs)
    x_ref = next(it)
    g_ref = next(it) if norm else None
    w_ref = next(it)
    r_ref = next(it) if res else None
    o_ref = next(it)
    xn_ref = next(it)

    @pl.when(pl.program_id(1) == 0)
    def _():
        x = x_ref[...].astype(F32)
        if norm:
            x = _rms(x, g_ref[...])
        xn_ref[...] = x.astype(BF16)

    acc = jnp.dot(xn_ref[...], w_ref[...], preferred_element_type=F32)
    if act == "relu2":
        acc = jnp.square(jnp.maximum(acc, 0.0))
    if res:
        acc = r_ref[...] + acc
    o_ref[...] = acc.astype(o_ref.dtype)


def _mm(x, w, *, g=None, res=None, act=None, out_dtype=F32, tm=1024, tn=1024, name="mm"):
    m, k = x.shape
    k2, n = w.shape
    assert k == k2
    tm = _row_tile(m, tm)
    tn = _row_tile(n, tn)
    in_specs = [pl.BlockSpec((tm, k), lambda i, j: (i, 0))]
    args = [x]
    if g is not None:
        in_specs.append(pl.BlockSpec((1, k), lambda i, j: (0, 0)))
        args.append(g.reshape(1, k).astype(F32))
    in_specs.append(pl.BlockSpec((k, tn), lambda i, j: (0, j)))
    args.append(w)
    if res is not None:
        in_specs.append(pl.BlockSpec((tm, tn), lambda i, j: (i, j)))
        args.append(res)
    return pl.pallas_call(
        functools.partial(_mm_body, norm=g is not None, res=res is not None, act=act),
        out_shape=jax.ShapeDtypeStruct((m, n), out_dtype),
        grid=(m // tm, n // tn),
        in_specs=in_specs,
        out_specs=pl.BlockSpec((tm, tn), lambda i, j: (i, j)),
        scratch_shapes=[pltpu.VMEM((tm, k), BF16)],
        compiler_params=_cparams("parallel", "arbitrary"),
        name=name,
    )(*args)


def _mlp_body(x_ref, g_ref, wu_ref, wd_ref, o_ref, xn_ref, acc_ref, *, final_norm):
    k = pl.program_id(1)

    @pl.when(k == 0)
    def _():
        x = x_ref[...]
        xn_ref[...] = _rms(x, g_ref[...]).astype(BF16)
        acc_ref[...] = x

    h = jnp.dot(xn_ref[...], wu_ref[...], preferred_element_type=F32)
    h = jnp.square(jnp.maximum(h, 0.0)).astype(BF16)
    acc_ref[...] += jnp.dot(h, wd_ref[...], preferred_element_type=F32)

    @pl.when(k == pl.num_programs(1) - 1)
    def _():
        o_ref[...] = acc_ref[...]


def _mlp(x, g, w_up, w_down, *, tm=1024, tf=512):
    m, d = x.shape
    ff = w_up.shape[1]
    tm = _row_tile(m, tm)
    tf = _row_tile(ff, tf)
    return pl.pallas_call(
        functools.partial(_mlp_body, final_norm=False),
        out_shape=jax.ShapeDtypeStruct((m, d), F32),
        grid=(m // tm, ff // tf),
        in_specs=[
            pl.BlockSpec((tm, d), lambda i, k: (i, 0)),
            pl.BlockSpec((1, d), lambda i, k: (0, 0)),
            pl.BlockSpec((d, tf), lambda i, k: (0, k)),
            pl.BlockSpec((tf, d), lambda i, k: (k, 0)),
        ],
        out_specs=pl.BlockSpec((tm, d), lambda i, k: (i, 0)),
        scratch_shapes=[pltpu.VMEM((tm, d), BF16), pltpu.VMEM((tm, d), F32)],
        compiler_params=_cparams("parallel", "arbitrary"),
        name="mlp",
    )(x, g.reshape(1, d).astype(F32), w_up, w_down)


def _norm_body(x_ref, g_ref, o_ref):
    o_ref[...] = _rms(x_ref[...], g_ref[...])


def _norm(x, g, *, tm=1024):
    m, d = x.shape
    tm = _row_tile(m, tm)
    return pl.pallas_call(
        _norm_body,
        out_shape=jax.ShapeDtypeStruct((m, d), F32),
        grid=(m // tm,),
        in_specs=[pl.BlockSpec((tm, d), lambda i: (i, 0)), pl.BlockSpec((1, d), lambda i: (0, 0))],
        out_specs=pl.BlockSpec((tm, d), lambda i: (i, 0)),
        compiler_params=_cparams("parallel"),
        name="final_norm",
    )(x, g.reshape(1, d).astype(F32))


def _memattn_body(q_ref, k_ref, v_ref, o_ref, *, heads, scale):
    dh = q_ref.shape[-1] // heads
    outs = []
    for h in range(heads):
        q = q_ref[:, h * dh:(h + 1) * dh].astype(BF16)
        k = k_ref[:, h * dh:(h + 1) * dh].astype(BF16)
        v = v_ref[:, h * dh:(h + 1) * dh].astype(BF16)
        s = lax.dot_general(q, k, (((1,), (1,)), ((), ())), preferred_element_type=F32) * scale
        s = s - jnp.max(s, axis=-1, keepdims=True)
        p = jnp.exp(s)
        p = p / jnp.sum(p, axis=-1, keepdims=True)
        outs.append(jnp.dot(p.astype(BF16), v, preferred_element_type=F32))
    o_ref[...] = jnp.concatenate(outs, axis=1).astype(o_ref.dtype)


def _memattn(q, k, v, *, n_seq, L, rows):
    d = q.shape[-1]
    n_mem = k.shape[-2]
    per_seq = L // rows
    return pl.pallas_call(
        functools.partial(_memattn_body, heads=MEM_HEADS, scale=1.0 / math.sqrt(d // MEM_HEADS)),
        out_shape=jax.ShapeDtypeStruct((n_seq * L, d), BF16),
        grid=(n_seq * per_seq,),
        in_specs=[pl.BlockSpec((rows, d), lambda s: (s, 0)),
                  pl.BlockSpec((None, n_mem, d), lambda s: (s // per_seq, 0, 0)),
                  pl.BlockSpec((None, n_mem, d), lambda s: (s // per_seq, 0, 0))],
        out_specs=pl.BlockSpec((rows, d), lambda s: (s, 0)),
        compiler_params=_cparams("parallel"),
        name="memattn",
    )(q, k, v)


def _memattn_dec_body(q_ref, k_ref, v_ref, o_ref, *, nb, rows, heads, scale):
    dh = q_ref.shape[-1] // heads
    n_mem = k_ref.shape[1]
    qrow_head = lax.broadcasted_iota(jnp.int32, (heads * rows, n_mem * heads), 0) // rows
    key_head = lax.broadcasted_iota(jnp.int32, (heads * rows, n_mem * heads), 1) % heads
    own = qrow_head == key_head
    outs = []
    for i in range(nb):
        qi = q_ref[i * rows:(i + 1) * rows, :]
        qs = jnp.concatenate([qi[:, h * dh:(h + 1) * dh] for h in range(heads)], axis=0).astype(BF16)
        k2 = k_ref[i].reshape(n_mem * heads, dh).astype(BF16)
        v2 = v_ref[i].reshape(n_mem * heads, dh).astype(BF16)
        s = lax.dot_general(qs, k2, (((1,), (1,)), ((), ())), preferred_element_type=F32) * scale
        s = jnp.where(own, s, -jnp.inf)
        s = s - jnp.max(s, axis=-1, keepdims=True)
        p = jnp.exp(s)
        p = p / jnp.sum(p, axis=-1, keepdims=True)
        o = jnp.dot(p.astype(BF16), v2, preferred_element_type=F32)
        outs.append(jnp.concatenate([o[h * rows:(h + 1) * rows, :] for h in range(heads)], axis=1))
    o_ref[...] = jnp.concatenate(outs, axis=0).astype(o_ref.dtype)


def _memattn_dec(q, k, v, *, layer, row_offset, n_seq, rows, nb):
    d = q.shape[-1]
    _, _, n_mem, heads, dh = k.shape
    blk = nb * rows
    assert row_offset % blk == 0 and n_seq % nb == 0
    kv_spec = pl.BlockSpec((None, nb, n_mem, heads, dh), lambda s: (layer, s, 0, 0, 0))
    return pl.pallas_call(
        functools.partial(_memattn_dec_body, nb=nb, rows=rows, heads=heads, scale=1.0 / math.sqrt(dh)),
        out_shape=jax.ShapeDtypeStruct((n_seq * rows, d), BF16),
        grid=(n_seq // nb,),
        in_specs=[pl.BlockSpec((blk, d), lambda s: (row_offset // blk + s, 0)), kv_spec, kv_spec],
        out_specs=pl.BlockSpec((blk, d), lambda s: (s, 0)),
        compiler_params=_cparams("parallel"),
        name="memattn_dec",
    )(q, k, v)


def _split3_lanes(v, heads):
    lane = lax.broadcasted_iota(jnp.int32, v.shape, 1)
    hi = v.astype(BF16).astype(F32)
    r1 = v - hi
    mid = r1.astype(BF16).astype(F32)
    lo = r1 - mid
    out = jnp.where(lane < heads, hi, jnp.where(lane < 2 * heads, mid, jnp.where(lane < 3 * heads, lo, 0.0)))
    return out.astype(BF16)


def _ssd_body(z_ref, x_ref, bc_ref, dt_ref, cw_ref, cb_ref, dtb_ref, alog_ref, dsk_ref, ng_ref, e3_ref,
              conv0_ref, s0_ref,
              y_ref, sfin_ref, ctail_ref,
              extx_ref, extbc_ref, st_ref, *, T, heads, groups, has_init):
    c = pl.program_id(1)
    nc = pl.num_programs(1)
    d_inner = x_ref.shape[-1]
    hpg = heads // groups
    gw = d_inner // groups
    hd = d_inner // heads
    n = D_STATE

    @pl.when(c == 0)
    def _():
        if has_init:
            extx_ref[0:8, :] = conv0_ref[0, :, 0:d_inner]
            extbc_ref[0:8, :] = conv0_ref[0, :, d_inner:]
            st_ref[...] = s0_ref[0].T
        else:
            extx_ref[0:8, :] = jnp.zeros((8, d_inner), F32)
            extbc_ref[0:8, :] = jnp.zeros((8, extbc_ref.shape[-1]), F32)
            st_ref[...] = jnp.zeros(st_ref.shape, F32)

    extx_ref[8:8 + T, :] = x_ref[...]
    extbc_ref[8:8 + T, :] = bc_ref[...]

    def conv(ext_ref, lo, width):
        cur = ext_ref[8:8 + T, :]
        tail = ext_ref[0:8, :]
        r8 = lax.broadcasted_iota(jnp.int32, (8, width), 0)
        kw = cw_ref.shape[0]
        acc = cb_ref[:, lo:lo + width] + cur * cw_ref[kw - 1:kw, lo:lo + width]
        for s in range(1, kw):
            rolled = pltpu.roll(cur, s, axis=0)
            head = jnp.where(r8 < s, pltpu.roll(tail, s, axis=0), rolled[0:8])
            shifted = head if T == 8 else jnp.concatenate([head, rolled[8:]], axis=0)
            acc = acc + shifted * cw_ref[kw - 1 - s:kw - s, lo:lo + width]
        return acc * jax.nn.sigmoid(acc)

    xc = conv(extx_ref, 0, d_inner)
    bcc = conv(extbc_ref, d_inner, extbc_ref.shape[-1])
    tail_x = extx_ref[T:T + 8, :]
    tail_bc = extbc_ref[T:T + 8, :]
    extx_ref[0:8, :] = tail_x
    extbc_ref[0:8, :] = tail_bc

    @pl.when(c == nc - 1)
    def _():
        ctail_ref[0, :, 0:d_inner] = tail_x
        ctail_ref[0, :, d_inner:] = tail_bc

    dtr = dt_ref[...] + dtb_ref[...]
    dtv = jnp.maximum(dtr, 0.0) + jnp.log1p(jnp.exp(-jnp.abs(dtr)))
    a = dtv * (-jnp.exp(alog_ref[...]))
    row = lax.broadcasted_iota(jnp.int32, a.shape, 0)
    acs = a
    sh = 1
    while sh < T:
        acs = acs + jnp.where(row >= sh, pltpu.roll(acs, sh, axis=0), 0.0)
        sh *= 2
    a_last = acs[T - 1:T, :]
    e3 = e3_ref[...]

    def expand(v):
        return jnp.dot(_split3_lanes(v, heads), e3, preferred_element_type=F32)

    dt_e = expand(dtv)
    dend_e = expand(jnp.exp(a_last - acs))
    eacs_e = expand(jnp.exp(acs))
    xdt = xc * dt_e
    xdt_b = xdt.astype(BF16)
    xw_b = (xdt * dend_e).astype(BF16)
    cdec_e = eacs_e[T - 1:T, :]

    if T < V7X_LANES:
        acs_p = jnp.concatenate([acs, jnp.zeros((V7X_LANES - T, V7X_LANES), F32)], axis=0)
    else:
        acs_p = acs
    acs_t = acs_p.T
    ti = lax.broadcasted_iota(jnp.int32, (T, T), 0)
    si = lax.broadcasted_iota(jnp.int32, (T, T), 1)
    causal = ti >= si
    lane_g = lax.broadcasted_iota(jnp.int32, (T, gw), 1)

    z = z_ref[...]
    gate = z * jax.nn.sigmoid(z)
    for g in range(groups):
        bg = bcc[:, g * n:(g + 1) * n]
        cg = bcc[:, groups * n + g * n: groups * n + (g + 1) * n].astype(BF16)
        bg_b = bg.astype(BF16)
        gmat = lax.dot_general(cg, bg_b, (((1,), (1,)), ((), ())), preferred_element_type=F32)
        st_g = st_ref[:, g * gw:(g + 1) * gw]
        y_g = jnp.dot(cg, st_g.astype(BF16), preferred_element_type=F32) * eacs_e[:, g * gw:(g + 1) * gw]
        xg = xdt_b[:, g * gw:(g + 1) * gw]
        for j in range(hpg):
            h = g * hpg + j
            seg = acs[:, h:h + 1] - acs_t[h:h + 1, 0:T]
            decay = jnp.exp(jnp.where(causal, seg, -jnp.inf))
            m_h = (gmat * decay).astype(BF16)
            x_h = jnp.where((lane_g >= j * hd) & (lane_g < (j + 1) * hd), xg, jnp.zeros_like(xg))
            y_g = y_g + jnp.dot(m_h, x_h, preferred_element_type=F32)
        cs_t = jnp.dot(bg.T.astype(BF16), xw_b[:, g * gw:(g + 1) * gw], preferred_element_type=F32)
        st_ref[:, g * gw:(g + 1) * gw] = st_g * cdec_e[:, g * gw:(g + 1) * gw] + cs_t
        y_g = y_g + xc[:, g * gw:(g + 1) * gw] * dsk_ref[:, g * gw:(g + 1) * gw]
        y_g = y_g * gate[:, g * gw:(g + 1) * gw]
        y_g = _rms(y_g, ng_ref[:, g * gw:(g + 1) * gw])
        y_ref[:, g * gw:(g + 1) * gw] = y_g.astype(y_ref.dtype)

    @pl.when(c == nc - 1)
    def _():
        sfin_ref[0] = st_ref[...].T


def _ssd(zx, conv_w, conv_b, dt_bias_rep, a_log_rep, d_exp, norm_g, e3, *, row_offset, nb, L, T,
         conv0=None, s0=None, layer=0):
    heads = d_exp.shape[-1] // SSD_HEADDIM
    d_inner = d_exp.shape[-1]
    bcw = 2 * SSD_GROUPS * D_STATE
    assert bcw == d_inner, "column blocks are indexed in units of d_inner"
    nc = L // T
    assert row_offset % T == 0
    rb0 = row_offset // T
    has_init = conv0 is not None
    rowmap = lambda col: (lambda b, c: (rb0 + b * nc + c, col))
    const = lambda b, c: (0, 0)
    per_b = lambda b, c: (b, 0, 0)
    if not has_init:
        conv0 = jnp.zeros((1, 8, d_inner + bcw), F32)
        s0 = jnp.zeros((1, 8, D_STATE), F32)
        init_specs = [pl.BlockSpec((1, 8, d_inner + bcw), lambda b, c: (0, 0, 0)),
                      pl.BlockSpec((1, 8, D_STATE), lambda b, c: (0, 0, 0))]
    else:
        init_specs = [pl.BlockSpec((1, 8, d_inner + bcw), per_b),
                      pl.BlockSpec((None, 1, d_inner, D_STATE), lambda b, c: (layer, b, 0, 0))]
    dt_col = (2 * d_inner + bcw) // V7X_LANES
    y, sfin, ctail = pl.pallas_call(
        functools.partial(_ssd_body, T=T, heads=heads, groups=SSD_GROUPS, has_init=has_init),
        out_shape=(jax.ShapeDtypeStruct((nb * L, d_inner), BF16),
                   jax.ShapeDtypeStruct((nb, d_inner, D_STATE), F32),
                   jax.ShapeDtypeStruct((nb, 8, d_inner + bcw), F32)),
        grid=(nb, nc),
        in_specs=[
            pl.BlockSpec((T, d_inner), rowmap(0)),
            pl.BlockSpec((T, d_inner), rowmap(1)),
            pl.BlockSpec((T, bcw), rowmap(2)),
            pl.BlockSpec((T, V7X_LANES), rowmap(dt_col)),
            pl.BlockSpec((4, d_inner + bcw), const),
            pl.BlockSpec((1, d_inner + bcw), const),
            pl.BlockSpec((1, V7X_LANES), const),
            pl.BlockSpec((1, V7X_LANES), const),
            pl.BlockSpec((1, d_inner), const),
            pl.BlockSpec((1, d_inner), const),
            pl.BlockSpec((V7X_LANES, d_inner), const),
        ] + init_specs,
        out_specs=(pl.BlockSpec((T, d_inner), lambda b, c: (b * nc + c, 0)),
                   pl.BlockSpec((1, d_inner, D_STATE), per_b),
                   pl.BlockSpec((1, 8, d_inner + bcw), per_b)),
        scratch_shapes=[pltpu.VMEM((T + 8, d_inner), F32), pltpu.VMEM((T + 8, bcw), F32),
                        pltpu.VMEM((D_STATE, d_inner), F32)],
        compiler_params=_cparams("parallel", "arbitrary"),
        name="ssd",
    )(zx, zx, zx, zx, conv_w, conv_b, dt_bias_rep, a_log_rep, d_exp, norm_g, e3, conv0, s0)
    return y, sfin, ctail


def _ssd_w_in_pad(w_in):
    heads = w_in.shape[1] - (w_in.shape[1] // V7X_LANES) * V7X_LANES
    main = w_in[:, :w_in.shape[1] - heads]
    dt = w_in[:, w_in.shape[1] - heads:]
    return jnp.concatenate([main] + [dt] * (V7X_LANES // heads), axis=1)


def _ssd_params(conv_w, conv_b, dt_bias, a_log, d_skip, norm_g):
    heads = dt_bias.shape[0]
    rep = V7X_LANES // heads
    d_inner = heads * SSD_HEADDIM
    src = jnp.arange(V7X_LANES)[:, None]
    dst_head = jnp.arange(d_inner)[None, :] // SSD_HEADDIM
    e3 = ((src % heads == dst_head) & (src < 3 * heads)).astype(BF16)
    return (conv_w.astype(F32), conv_b.reshape(1, -1).astype(F32),
            jnp.tile(dt_bias.reshape(1, heads), (1, rep)).astype(F32),
            jnp.tile(a_log.reshape(1, heads), (1, rep)).astype(F32),
            jnp.repeat(d_skip, SSD_HEADDIM).reshape(1, d_inner).astype(F32),
            norm_g.reshape(1, d_inner).astype(F32), e3)


SLOT = V7X_LANES
VT_SLOT = V_HEAD + 16
LOG2E = math.log2(math.e)


def _mla_proj_body(x_ref, g_ref, wd_ref, qn_ref, kvn_ref, wuq_ref, wuqr_ref, wuk_ref, wuvt_ref, cos_ref, sin_ref,
                   q_ref, qt_ref, kk_ref, vt_ref, lat_ref, kr_ref, *, heads, q_lora, kv_lora, rope, scale):
    xn = _rms(x_ref[...], g_ref[...]).astype(BF16)
    down = jnp.dot(xn, wd_ref[...], preferred_element_type=F32)
    cq = down[:, :q_lora]
    ckv = down[:, q_lora:q_lora + kv_lora]
    krs = down[:, q_lora + kv_lora:q_lora + kv_lora + SLOT]
    krr = down[:, q_lora + kv_lora + SLOT:]
    cos = cos_ref[...]
    sin = sin_ref[...]
    cqn = _rms(cq, qn_ref[...]).astype(BF16)
    qp = jnp.dot(cqn, wuq_ref[...], preferred_element_type=F32)
    qr = jnp.dot(cqn, wuqr_ref[...], preferred_element_type=F32)
    lat = _rms(ckv, kvn_ref[...])
    lat_ref[...] = lat
    kr_rot = krs * cos + krr * sin
    kr_ref[...] = kr_rot[:, QK_NOPE:QK_NOPE + rope]
    latb = lat.astype(BF16)
    kn = jnp.dot(latb, wuk_ref[...], preferred_element_type=F32)
    for h in range(heads):
        sl = slice(h * SLOT, (h + 1) * SLOT)
        q_h = (qp[:, sl] * cos + qr[:, sl] * sin) * scale
        q_ref[:, sl] = q_h.astype(BF16)
        qt_ref[sl, :] = q_h.T.astype(BF16)
        kk_ref[:, sl] = (kn[:, sl] + kr_rot).astype(BF16)
    vt = lax.dot_general(wuvt_ref[...], latb, (((1,), (1,)), ((), ())),
                         preferred_element_type=F32)
    fill_rows = lax.broadcasted_iota(jnp.int32, (VT_SLOT - V_HEAD, vt.shape[1]), 0)
    ones_then_zeros = jnp.where(fill_rows == 0, 1.0, 0.0).astype(BF16)
    for h in range(heads):
        vt_ref[h * VT_SLOT:h * VT_SLOT + V_HEAD, :] = vt[h * V_HEAD:(h + 1) * V_HEAD, :].astype(BF16)
        vt_ref[h * VT_SLOT + V_HEAD:(h + 1) * VT_SLOT, :] = ones_then_zeros


def _mla_proj(x, g, w, cos_tab, sin_tab, tab_index, *, tm=512):
    m, d = x.shape
    heads = w["wuq"].shape[1] // SLOT
    q_lora = w["qn"].shape[-1]
    kv_lora = w["kvn"].shape[-1]
    rope = w["rope"]
    tm = _row_tile(m, tm)
    const = lambda i: (0, 0)
    rowb = lambda i: (i, 0)
    full = lambda a: pl.BlockSpec(a.shape, const)
    return pl.pallas_call(
        functools.partial(_mla_proj_body, heads=heads, q_lora=q_lora, kv_lora=kv_lora, rope=rope,
                          scale=w["scale"]),
        out_shape=(jax.ShapeDtypeStruct((m, heads * SLOT), BF16),
                   jax.ShapeDtypeStruct((heads * SLOT, m), BF16),
                   jax.ShapeDtypeStruct((m, heads * SLOT), BF16),
                   jax.ShapeDtypeStruct((heads * VT_SLOT, m), BF16),
                   jax.ShapeDtypeStruct((m, kv_lora), F32),
                   jax.ShapeDtypeStruct((m, rope), F32)),
        grid=(m // tm,),
        in_specs=[pl.BlockSpec((tm, d), rowb), pl.BlockSpec((1, d), const),
                  full(w["wd"]), full(w["qn"]), full(w["kvn"]), full(w["wuq"]), full(w["wuqr"]),
                  full(w["wuk"]), full(w["wuvt"]),
                  pl.BlockSpec((tm, SLOT), lambda i: (tab_index(i), 0)),
                  pl.BlockSpec((tm, SLOT), lambda i: (tab_index(i), 0))],
        out_specs=(pl.BlockSpec((tm, heads * SLOT), rowb), pl.BlockSpec((heads * SLOT, tm), lambda i: (0, i)),
                   pl.BlockSpec((tm, heads * SLOT), rowb), pl.BlockSpec((heads * VT_SLOT, tm), lambda i: (0, i)),
                   pl.BlockSpec((tm, kv_lora), rowb), pl.BlockSpec((tm, rope), rowb)),
        compiler_params=_cparams("parallel"),
        name="mla_proj",
    )(x, g.reshape(1, d).astype(F32), w["wd"], w["qn"], w["kvn"], w["wuq"], w["wuqr"], w["wuk"], w["wuvt"],
      cos_tab, sin_tab)


def _mla_weights(w_down, q_norm, kv_norm, w_uq, w_uk, w_uv):
    q_lora = q_norm.shape[0]
    kv_lora = kv_norm.shape[0]
    heads = w_uq.shape[1]
    qk = w_uq.shape[2]
    rope = qk - QK_NOPE
    half = rope // 2
    pad = SLOT - qk

    def slot_pair(wr):
        z_lo = jnp.zeros(wr.shape[:-1] + (QK_NOPE,), wr.dtype)
        z_hi = jnp.zeros(wr.shape[:-1] + (pad,), wr.dtype)
        plain = jnp.concatenate([z_lo, wr, z_hi], axis=-1)
        rot = jnp.concatenate([z_lo, -wr[..., half:], wr[..., :half], z_hi], axis=-1)
        return plain, rot

    w_kr = w_down[:, q_lora + kv_lora:]
    kr_plain, kr_rot = slot_pair(w_kr)
    wd = jnp.concatenate([w_down[:, :q_lora + kv_lora], kr_plain, kr_rot], axis=1)
    uq_nope = jnp.concatenate([w_uq[..., :QK_NOPE], jnp.zeros(w_uq.shape[:2] + (SLOT - QK_NOPE,), w_uq.dtype)], -1)
    uq_plain, uq_rot = slot_pair(w_uq[..., QK_NOPE:])
    wuq = (uq_nope + uq_plain).reshape(q_lora, heads * SLOT)
    wuqr = uq_rot.reshape(q_lora, heads * SLOT)
    wuk = jnp.concatenate([w_uk, jnp.zeros(w_uk.shape[:2] + (SLOT - QK_NOPE,), w_uk.dtype)], -1)
    wuk = wuk.reshape(kv_lora, heads * SLOT)
    wuvt = w_uv.reshape(kv_lora, heads * V_HEAD).T
    eye_r = jnp.zeros((SLOT, SLOT), w_uk.dtype).at[QK_NOPE + jnp.arange(rope), jnp.arange(rope)].set(1.0)
    uk_t = jnp.transpose(w_uk, (1, 2, 0))
    uk_t = jnp.concatenate([uk_t, jnp.zeros((heads, SLOT - QK_NOPE, kv_lora), w_uk.dtype)], axis=1)
    wabs = jnp.concatenate([uk_t, jnp.broadcast_to(eye_r, (heads, SLOT, SLOT))], axis=2)
    uv = jnp.transpose(w_uv, (1, 0, 2)).reshape(heads // 2, 2, kv_lora, V_HEAD)
    z = jnp.zeros((heads // 2, kv_lora, V_HEAD), w_uv.dtype)
    wuv_bd = jnp.concatenate([jnp.concatenate([uv[:, 0], z], axis=2), jnp.concatenate([z, uv[:, 1]], axis=2)], axis=1)
    return dict(wd=wd.astype(BF16), qn=q_norm.reshape(1, -1).astype(F32), kvn=kv_norm.reshape(1, -1).astype(F32),
                wuq=wuq.astype(BF16), wuqr=wuqr.astype(BF16), wuk=wuk.astype(BF16), wuvt=wuvt.astype(BF16),
                wabs=wabs.astype(BF16), wuv_bd=wuv_bd.astype(BF16), rope=rope,
                scale=LOG2E / math.sqrt(qk))


def _rope_tables(positions, rope):
    half = rope // 2
    inv = ROPE_THETA ** (-jnp.arange(half, dtype=F32) * (2.0 / rope))
    ang = positions.astype(F32)[:, None] * inv[None, :]
    c, s = jnp.cos(ang), jnp.sin(ang)
    p = positions.shape[0]
    ones = jnp.ones((p, QK_NOPE), F32)
    zeros = jnp.zeros((p, QK_NOPE), F32)
    zpad = jnp.zeros((p, SLOT - QK_NOPE - rope), F32)
    return (jnp.concatenate([ones, c, c, zpad], axis=1), jnp.concatenate([zeros, s, s, zpad], axis=1))


def _headmm_body(x_ref, w_ref, o_ref):
    o_ref[...] = jnp.dot(x_ref[...], w_ref[...], preferred_element_type=F32).astype(o_ref.dtype)


def _headmm(x, w, *, out_dtype=BF16):
    m = x.shape[0]
    g, kb, nb = w.shape
    assert x.shape[1] == g * kb
    return pl.pallas_call(
        _headmm_body,
        out_shape=jax.ShapeDtypeStruct((m, g * nb), out_dtype),
        grid=(g,),
        in_specs=[pl.BlockSpec((m, kb), lambda h: (0, h)), pl.BlockSpec((None, kb, nb), lambda h: (h, 0, 0))],
        out_specs=pl.BlockSpec((m, nb), lambda h: (0, h)),
        compiler_params=_cparams("parallel"),
        name="headmm",
    )(x, w)


def _flash_body(qi_ref, ki_ref, qt_ref, k_ref, vt_ref, o_ref, m_ref, acc_ref, *, heads, tq):
    p_idx = pl.program_id(1)
    qi = qi_ref[p_idx]
    ki = ki_ref[p_idx]

    @pl.when(ki == 0)
    def _():
        m_ref[...] = jnp.full(m_ref.shape, -jnp.inf, F32)
        acc_ref[...] = jnp.zeros(acc_ref.shape, F32)

    def step(masked):
        if masked:
            key_i = lax.broadcasted_iota(jnp.int32, (tq, tq), 0)
            qry_i = lax.broadcasted_iota(jnp.int32, (tq, tq), 1)
            keep = key_i <= qry_i
        def scores(h):
            k = k_ref[:, h * SLOT:(h + 1) * SLOT]
            qt = qt_ref[h * SLOT:(h + 1) * SLOT, :]
            return jnp.dot(k, qt, preferred_element_type=F32)

        ahead = [scores(0), scores(1)]
        for h in range(heads):
            s = ahead.pop(0)
            if h + 2 < heads:
                ahead.append(scores(h + 2))
            if masked:
                s = jnp.where(keep, s, -jnp.inf)
            m_prev = m_ref[h:h + 1, :]
            m_new = jnp.maximum(m_prev, jnp.max(s, axis=0, keepdims=True))
            alpha = jnp.exp2(m_prev - m_new)
            p = jnp.exp2(s - m_new)
            m_ref[h:h + 1, :] = m_new
            vt = vt_ref[h * VT_SLOT:(h + 1) * VT_SLOT, :]
            pv = jnp.dot(vt, p.astype(BF16), preferred_element_type=F32)
            acc_ref[h * VT_SLOT:(h + 1) * VT_SLOT, :] = acc_ref[h * VT_SLOT:(h + 1) * VT_SLOT, :] * alpha + pv

    @pl.when(ki < qi)
    def _():
        step(False)

    @pl.when(ki == qi)
    def _():
        step(True)
        outs = []
        for h in range(heads):
            l = acc_ref[h * VT_SLOT + V_HEAD:h * VT_SLOT + V_HEAD + 1, :]
            outs.append(acc_ref[h * VT_SLOT:h * VT_SLOT + V_HEAD, :] / l)
        o_ref[...] = jnp.concatenate(outs, axis=0).T.astype(o_ref.dtype)


def _flash(qt, kk, vt, *, nb, L, tq=512):
    heads = kk.shape[1] // SLOT
    tq = _row_tile(L, tq)
    nq = L // tq
    pairs = [(i, j) for i in range(nq) for j in range(i + 1)]
    qi_tab = jnp.asarray([p[0] for p in pairs], jnp.int32)
    ki_tab = jnp.asarray([p[1] for p in pairs], jnp.int32)
    grid_spec = pltpu.PrefetchScalarGridSpec(
        num_scalar_prefetch=2,
        grid=(nb, len(pairs)),
        in_specs=[
            pl.BlockSpec((heads * SLOT, tq), lambda b, p, qi, ki: (0, b * nq + qi[p])),
            pl.BlockSpec((tq, heads * SLOT), lambda b, p, qi, ki: (b * nq + ki[p], 0)),
            pl.BlockSpec((heads * VT_SLOT, tq), lambda b, p, qi, ki: (0, b * nq + ki[p])),
        ],
        out_specs=pl.BlockSpec((tq, heads * V_HEAD), lambda b, p, qi, ki: (b * nq + qi[p], 0)),
        scratch_shapes=[pltpu.VMEM((heads, tq), F32), pltpu.VMEM((heads * VT_SLOT, tq), F32)],
    )
    return pl.pallas_call(
        functools.partial(_flash_body, heads=heads, tq=tq),
        out_shape=jax.ShapeDtypeStruct((nb * L, heads * V_HEAD), BF16),
        grid_spec=grid_spec,
        compiler_params=_cparams("parallel", "arbitrary"),
        name="mla_flash",
    )(qi_tab, ki_tab, qt, kk, vt)


def _decode_body(pt_ref, q_ref, cnew_ref, rnew_ref, lat_hbm, ropet_hbm, o_ref,
                 cbuf, rbuf, cb, sbuf, sem, *, layer, n_pages, page, chunk_pages, kv_lora, rope, heads, ls):
    b = pl.program_id(0)
    nb = pl.num_programs(0)
    slot = lax.rem(b, 2)
    rows = q_ref.shape[0]
    chunk = chunk_pages * page

    def page_copies(seq, sl):
        cps = []
        for i in range(n_pages):
            pg = pt_ref[seq, i]
            cps.append(pltpu.make_async_copy(lat_hbm.at[layer, pg], cbuf.at[sl, pl.ds(i * page, page), :], sem.at[0, sl]))
            cps.append(pltpu.make_async_copy(ropet_hbm.at[layer, pg], rbuf.at[sl, i], sem.at[1, sl]))
        return cps

    @pl.when(b == 0)
    def _():
        for cp in page_copies(0, 0):
            cp.start()

    @pl.when(b + 1 < nb)
    def _():
        for cp in page_copies(b + 1, 1 - slot):
            cp.start()

    for cp in page_copies(b, slot):
        cp.wait()

    q_lat = q_ref[:, :kv_lora]
    q_r = q_ref[:, kv_lora:kv_lora + rope]
    nt = (((1,), (1,)), ((), ()))
    n_chunks = n_pages // chunk_pages
    for ck in range(n_chunks):
        c_b = cbuf[slot, ck * chunk:(ck + 1) * chunk, :].astype(BF16)
        cb[ck * chunk:(ck + 1) * chunk, :] = c_b
        r_b = jnp.concatenate([rbuf[slot, ck * chunk_pages + i].astype(BF16) for i in range(chunk_pages)],
                              axis=1)
        sbuf[:, ck * chunk:(ck + 1) * chunk] = (
            lax.dot_general(q_lat, c_b, nt, preferred_element_type=F32)
            + jnp.dot(q_r, r_b, preferred_element_type=F32))

    cn = cnew_ref[...].astype(BF16)
    rn = rnew_ref[...].astype(BF16)
    sn = (lax.dot_general(q_lat, cn, nt, preferred_element_type=F32)
          + lax.dot_general(q_r, rn, nt, preferred_element_type=F32))
    qpos = lax.broadcasted_iota(jnp.int32, (rows, ls), 0) // heads
    kpos = lax.broadcasted_iota(jnp.int32, (rows, ls), 1)
    sn = jnp.where(qpos >= kpos, sn, -jnp.inf)

    m = jnp.maximum(jnp.max(sbuf[...], axis=-1, keepdims=True), jnp.max(sn, axis=-1, keepdims=True))
    pn = jnp.exp2(sn - m)
    l = jnp.sum(pn, axis=-1, keepdims=True)
    acc = jnp.dot(pn.astype(BF16), cn, preferred_element_type=F32)
    for ck in range(n_chunks):
        p = jnp.exp2(sbuf[:, ck * chunk:(ck + 1) * chunk] - m)
        l = l + jnp.sum(p, axis=-1, keepdims=True)
        acc = acc + jnp.dot(p.astype(BF16), cb[ck * chunk:(ck + 1) * chunk, :], preferred_element_type=F32)
    o_ref[...] = (acc / l).astype(o_ref.dtype)


def _decode(q_ext, lat_new, kr_new, lat_pool, ropet_pool, page_table, *, layer, bs, ls, heads, new_row_offset,
            chunk_pages=16):
    kv_lora = lat_new.shape[1]
    rope = kr_new.shape[1]
    page = lat_pool.shape[2]
    n_pages = page_table.shape[1]
    chunk_pages = math.gcd(chunk_pages, n_pages)
    assert new_row_offset % ls == 0
    rows = ls * heads
    qw = q_ext.shape[1]
    grid_spec = pltpu.PrefetchScalarGridSpec(
        num_scalar_prefetch=1,
        grid=(bs,),
        in_specs=[pl.BlockSpec((rows, qw), lambda b, pt: (b, 0)),
                  pl.BlockSpec((ls, kv_lora), lambda b, pt: (new_row_offset // ls + b, 0)),
                  pl.BlockSpec((ls, rope), lambda b, pt: (new_row_offset // ls + b, 0)),
                  pl.BlockSpec(memory_space=pl.ANY),
                  pl.BlockSpec(memory_space=pl.ANY)],
        out_specs=pl.BlockSpec((rows, kv_lora), lambda b, pt: (b, 0)),
        scratch_shapes=[pltpu.VMEM((2, n_pages * page, kv_lora), F32),
                        pltpu.VMEM((2, n_pages, rope, page), F32),
                        pltpu.VMEM((n_pages * page, kv_lora), BF16),
                        pltpu.VMEM((rows, n_pages * page), F32),
                        pltpu.SemaphoreType.DMA((2, 2))],
    )
    return pl.pallas_call(
        functools.partial(_decode_body, layer=layer, n_pages=n_pages, page=page, chunk_pages=chunk_pages,
                          kv_lora=kv_lora, rope=rope, heads=heads, ls=ls),
        out_shape=jax.ShapeDtypeStruct((bs * rows, kv_lora), BF16),
        grid_spec=grid_spec,
        compiler_params=_cparams("arbitrary"),
        name="mla_decode",
    )(page_table, q_ext, lat_new, kr_new, lat_pool, ropet_pool)


MEM_ROWS = 1024
MEM_NB = 4
PROJ_TM = 512


def kernel(x_prompt, x_sample, mem_prompt, state_ssm, state_conv, cache_mla_latent, cache_mla_rope_k, cache_mem_k, cache_mem_v, page_table, norm_mix, norm_mem, norm_memkv, norm_ffn, norm_final, ssd_w_in, ssd_conv_w, ssd_conv_b, ssd_dt_bias, ssd_a_log, ssd_d, ssd_norm, ssd_w_out, mla_w_down, mla_q_norm, mla_kv_norm, mla_w_uq, mla_w_uk, mla_w_uv, mla_w_o, mem_w_q, mem_w_kv, mem_w_o, mlp_w_up, mlp_w_down):
    bp, lp, d = x_prompt.shape
    bs, ls, _ = x_sample.shape
    mp, ms = bp * lp, bs * ls
    depth = norm_mix.shape[0]
    n_mem = mem_prompt.shape[1]
    past_len = page_table.shape[1] * cache_mla_latent.shape[2]
    mla_heads = mla_w_uq.shape[2]
    rope = mla_w_uq.shape[3] - QK_NOPE
    ssd_heads = ssd_dt_bias.shape[1]
    d_inner = ssd_heads * SSD_HEADDIM

    x = jnp.concatenate([x_prompt.reshape(mp, d), x_sample.reshape(ms, d)], axis=0)
    mem_rows = mem_prompt.reshape(bp * n_mem, d)
    ropet_pool = jnp.swapaxes(cache_mla_rope_k, 2, 3)
    ssm0 = state_ssm.reshape(state_ssm.shape[0], bs, d_inner, D_STATE)

    pos = jnp.concatenate([jnp.arange(lp), jnp.tile(past_len + jnp.arange(ls), bs)])
    cos_tab, sin_tab = _rope_tables(pos, rope)
    proj_tm = _row_tile(ms, PROJ_TM)
    assert lp % proj_tm == 0
    npt, tpl = mp // proj_tm, lp // proj_tm
    tab_index = lambda i: jnp.where(i < npt, i % tpl, tpl + (i - npt))

    mem_rows_p = _row_tile(lp, MEM_ROWS)
    mem_nb = _row_tile(bs, MEM_NB)

    p_ssm, p_conv, p_lat, p_rk, p_mk, p_mv = [], [], [], [], [], []
    s_ssm, s_conv, s_lat, s_rk = [], [], [], []
    for i in range(depth):
        j = i // 2
        if i % 2 == 0:
            w_pad = _ssd_w_in_pad(ssd_w_in[j]).astype(BF16)
            zx = _mm(x, w_pad, g=norm_mix[i], tn=7 * V7X_LANES, name="ssd_in")
            prm = _ssd_params(ssd_conv_w[j], ssd_conv_b[j], ssd_dt_bias[j], ssd_a_log[j], ssd_d[j], ssd_norm[j])
            y_p, st_p, ct_p = _ssd(zx, *prm, row_offset=0, nb=bp, L=lp, T=math.gcd(SSD_CHUNK, lp))
            conv0 = jnp.pad(state_conv[j], ((0, 0), (8 - state_conv.shape[2], 0), (0, 0)))
            y_s, st_s, ct_s = _ssd(zx, *prm, row_offset=mp, nb=bs, L=ls, T=math.gcd(SSD_CHUNK, ls),
                                   conv0=conv0, s0=ssm0, layer=j)
            x = _mm(jnp.concatenate([y_p, y_s], axis=0), ssd_w_out[j].astype(BF16), res=x, name="ssd_out")
            kc = state_conv.shape[2]
            p_ssm.append(st_p.reshape(bp, ssd_heads, SSD_HEADDIM, D_STATE))
            s_ssm.append(st_s.reshape(bs, ssd_heads, SSD_HEADDIM, D_STATE))
            p_conv.append(ct_p[:, 8 - kc:, :])
            s_conv.append(ct_s[:, 8 - kc:, :])
        else:
            w = _mla_weights(mla_w_down[j], mla_q_norm[j], mla_kv_norm[j], mla_w_uq[j], mla_w_uk[j], mla_w_uv[j])
            q, qt, kk, vt, lat, kr = _mla_proj(x, norm_mix[i], w, cos_tab, sin_tab, tab_index, tm=proj_tm)
            o_p = _flash(qt, kk, vt, nb=bp, L=lp)
            q_ext = _headmm(q[mp:], w["wabs"]).reshape(ms * mla_heads, -1)
            o_lat = _decode(q_ext, lat, kr, cache_mla_latent, ropet_pool, page_table, layer=j, bs=bs, ls=ls,
                            heads=mla_heads, new_row_offset=mp)
            o_s = _headmm(o_lat.reshape(ms, -1), w["wuv_bd"])
            x = _mm(jnp.concatenate([o_p, o_s], axis=0), mla_w_o[j].astype(BF16), res=x, name="mla_out")
            p_lat.append(lat[:mp].reshape(bp, lp, -1))
            s_lat.append(lat[mp:].reshape(bs, ls, -1))
            p_rk.append(kr[:mp].reshape(bp, lp, -1))
            s_rk.append(kr[mp:].reshape(bs, ls, -1))
        kv = _mm(mem_rows, mem_w_kv[i].astype(BF16), g=norm_memkv[i], name="mem_kv")
        kp = kv[:, :d].reshape(bp, n_mem, d)
        vp = kv[:, d:].reshape(bp, n_mem, d)
        p_mk.append(kp.reshape(bp, n_mem, MEM_HEADS, d // MEM_HEADS))
        p_mv.append(vp.reshape(bp, n_mem, MEM_HEADS, d // MEM_HEADS))
        qm = _mm(x, mem_w_q[i].astype(BF16), g=norm_mem[i], name="mem_q")
        o_p = _memattn(qm, kp, vp, n_seq=bp, L=lp, rows=mem_rows_p)
        o_s = _memattn_dec(qm, cache_mem_k, cache_mem_v, layer=i, row_offset=mp, n_seq=bs, rows=ls, nb=mem_nb)
        x = _mm(jnp.concatenate([o_p, o_s], axis=0), mem_w_o[i].astype(BF16), res=x, name="mem_out")
        x = _mlp(x, norm_ffn[i], mlp_w_up[i].astype(BF16), mlp_w_down[i].astype(BF16))
    y = _norm(x, norm_final)
    return (y[:mp].reshape(bp, lp, d), y[mp:].reshape(bs, ls, d),
            jnp.stack(p_ssm), jnp.stack(p_conv), jnp.stack(p_lat), jnp.stack(p_rk), jnp.stack(p_mk), jnp.stack(p_mv),
            jnp.stack(s_ssm), jnp.stack(s_conv), jnp.stack(s_lat), jnp.stack(s_rk))
```

```python
import functools
import math

import jax
import jax.numpy as jnp
from jax import lax
from jax.experimental import pallas as pl
from jax.experimental.pallas import tpu as pltpu

F32 = jnp.float32
BF16 = jnp.bfloat16

EPS = 1e-6
ROPE_THETA = 10000.0

V7X_LANES = 128
V7X_SUBLANES = 8
V7X_MXU_COLS = 256
V7X_VMEM_LIMIT_BYTES = 56 * 1024 * 1024

SSD_HEADDIM = 64
SSD_GROUPS = 8
D_STATE = 128
SSD_CHUNK = 128
QK_NOPE = 64
V_HEAD = 64
MEM_HEADS = 4


def _cparams(*sem):
    return pltpu.CompilerParams(dimension_semantics=sem, vmem_limit_bytes=V7X_VMEM_LIMIT_BYTES)


def _rms(x, g):
    return x * lax.rsqrt(jnp.mean(x * x, axis=-1, keepdims=True) + EPS) * g


def _row_tile(m, pref):
    t = min(pref, m)
    assert m % t == 0, (m, t)
    return t


def _mm_body(*refs, norm, res, act, head_tiles):
    it = iter(refs)
    x_ref = next(it)
    t_ref = next(it) if head_tiles is not None else None
    g_ref = next(it) if norm else None
    w_ref = next(it)
    r_ref = next(it) if res else None
    o_ref = next(it)
    xn_ref = next(it)

    def stage(src_ref):
        x = src_ref[...].astype(F32)
        if norm:
            x = _rms(x, g_ref[...])
        xn_ref[...] = x.astype(BF16)

    first = pl.program_id(1) == 0
    if head_tiles is None:
        pl.when(first)(lambda: stage(x_ref))
    else:
        in_head = pl.program_id(0) < head_tiles
        pl.when(first & in_head)(lambda: stage(x_ref))
        pl.when(first & jnp.logical_not(in_head))(lambda: stage(t_ref))

    acc = jnp.dot(xn_ref[...], w_ref[...], preferred_element_type=F32)
    if act == "relu2":
        acc = jnp.square(jnp.maximum(acc, 0.0))
    if res:
        acc = r_ref[...] + acc
    o_ref[...] = acc.astype(o_ref.dtype)


def _mm(x, w, *, x_tail=None, g=None, res=None, act=None, out_dtype=F32, tm=1024, tn=1024, name="mm"):
    m, k = x.shape
    k2, n = w.shape
    assert k == k2
    head_tiles = None
    if x_tail is None:
        tm = _row_tile(m, tm)
        in_specs = [pl.BlockSpec((tm, k), lambda i, j: (i, 0))]
        args = [x]
    else:
        m_tail = x_tail.shape[0]
        tm = math.gcd(math.gcd(m, m_tail), tm)
        head_tiles = m // tm
        m = m + m_tail
        in_specs = [pl.BlockSpec((tm, k), lambda i, j: (jnp.minimum(i, head_tiles - 1), 0)),
                    pl.BlockSpec((tm, k), lambda i, j: (jnp.maximum(i - head_tiles, 0), 0))]
        args = [x, x_tail]
    tn = _row_tile(n, tn)
    if g is not None:
        in_specs.append(pl.BlockSpec((1, k), lambda i, j: (0, 0)))
        args.append(g.reshape(1, k).astype(F32))
    in_specs.append(pl.BlockSpec((k, tn), lambda i, j: (0, j)))
    args.append(w)
    if res is not None:
        in_specs.append(pl.BlockSpec((tm, tn), lambda i, j: (i, j)))
        args.append(res)
    return pl.pallas_call(
        functools.partial(_mm_body, norm=g is not None, res=res is not None, act=act, head_tiles=head_tiles),
        out_shape=jax.ShapeDtypeStruct((m, n), out_dtype),
        grid=(m // tm, n // tn),
        in_specs=in_specs,
        out_specs=pl.BlockSpec((tm, tn), lambda i, j: (i, j)),
        scratch_shapes=[pltpu.VMEM((tm, k), BF16)],
        compiler_params=_cparams("parallel", "arbitrary"),
        name=name,
    )(*args)


def _mlp_body(x_ref, g_ref, wu_ref, wd_ref, o_ref, xn_ref, acc_ref, *, final_norm):
    k = pl.program_id(1)

    @pl.when(k == 0)
    def _():
        x = x_ref[...]
        xn_ref[...] = _rms(x, g_ref[...]).astype(BF16)
        acc_ref[...] = x

    h = jnp.dot(xn_ref[...], wu_ref[...], preferred_element_type=F32)
    h = jnp.square(jnp.maximum(h, 0.0)).astype(BF16)
    acc_ref[...] += jnp.dot(h, wd_ref[...], preferred_element_type=F32)

    @pl.when(k == pl.num_programs(1) - 1)
    def _():
        o_ref[...] = acc_ref[...]


def _mlp(x, g, w_up, w_down, *, tm=1024, tf=512):
    m, d = x.shape
    ff = w_up.shape[1]
    tm = _row_tile(m, tm)
    tf = _row_tile(ff, tf)
    return pl.pallas_call(
        functools.partial(_mlp_body, final_norm=False),
        out_shape=jax.ShapeDtypeStruct((m, d), F32),
        grid=(m // tm, ff // tf),
        in_specs=[
            pl.BlockSpec((tm, d), lambda i, k: (i, 0)),
            pl.BlockSpec((1, d), lambda i, k: (0, 0)),
            pl.BlockSpec((d, tf), lambda i, k: (0, k)),
            pl.BlockSpec((tf, d), lambda i, k: (k, 0)),
        ],
        out_specs=pl.BlockSpec((tm, d), lambda i, k: (i, 0)),
        scratch_shapes=[pltpu.VMEM((tm, d), BF16), pltpu.VMEM((tm, d), F32)],
        compiler_params=_cparams("parallel", "arbitrary"),
        name="mlp",
    )(x, g.reshape(1, d).astype(F32), w_up, w_down)


def _norm_body(x_ref, g_ref, o_ref):
    o_ref[...] = _rms(x_ref[...], g_ref[...])


def _norm(x, g, *, row_offset, rows, tm=1024):
    d = x.shape[1]
    tm = math.gcd(math.gcd(rows, tm), row_offset) if row_offset else _row_tile(rows, tm)
    first = row_offset // tm
    return pl.pallas_call(
        _norm_body,
        out_shape=jax.ShapeDtypeStruct((rows, d), F32),
        grid=(rows // tm,),
        in_specs=[pl.BlockSpec((tm, d), lambda i: (first + i, 0)), pl.BlockSpec((1, d), lambda i: (0, 0))],
        out_specs=pl.BlockSpec((tm, d), lambda i: (i, 0)),
        compiler_params=_cparams("parallel"),
        name="final_norm",
    )(x, g.reshape(1, d).astype(F32))


def _memattn_body(q_ref, k_ref, v_ref, o_ref, *, heads, scale):
    dh = q_ref.shape[-1] // heads
    outs = []
    for h in range(heads):
        q = q_ref[:, h * dh:(h + 1) * dh].astype(BF16)
        k = k_ref[:, h * dh:(h + 1) * dh].astype(BF16)
        v = v_ref[:, h * dh:(h + 1) * dh].astype(BF16)
        s = lax.dot_general(q, k, (((1,), (1,)), ((), ())), preferred_element_type=F32) * scale
        s = s - jnp.max(s, axis=-1, keepdims=True)
        p = jnp.exp(s)
        p = p / jnp.sum(p, axis=-1, keepdims=True)
        outs.append(jnp.dot(p.astype(BF16), v, preferred_element_type=F32))
    o_ref[...] = jnp.concatenate(outs, axis=1).astype(o_ref.dtype)


def _memattn(q, k, v, *, n_seq, L, rows):
    d = q.shape[-1]
    n_mem = k.shape[-2]
    per_seq = L // rows
    return pl.pallas_call(
        functools.partial(_memattn_body, heads=MEM_HEADS, scale=1.0 / math.sqrt(d // MEM_HEADS)),
        out_shape=jax.ShapeDtypeStruct((n_seq * L, d), BF16),
        grid=(n_seq * per_seq,),
        in_specs=[pl.BlockSpec((rows, d), lambda s: (s, 0)),
                  pl.BlockSpec((None, n_mem, d), lambda s: (s // per_seq, 0, 0)),
                  pl.BlockSpec((None, n_mem, d), lambda s: (s // per_seq, 0, 0))],
        out_specs=pl.BlockSpec((rows, d), lambda s: (s, 0)),
        compiler_params=_cparams("parallel"),
        name="memattn",
    )(q, k, v)


def _memattn_dec_body(q_ref, k_ref, v_ref, o_ref, *, nb, rows, heads, scale):
    dh = q_ref.shape[-1] // heads
    n_mem = k_ref.shape[1]
    qrow_head = lax.broadcasted_iota(jnp.int32, (heads * rows, n_mem * heads), 0) // rows
    key_head = lax.broadcasted_iota(jnp.int32, (heads * rows, n_mem * heads), 1) % heads
    own = qrow_head == key_head
    outs = []
    for i in range(nb):
        qi = q_ref[i * rows:(i + 1) * rows, :]
        qs = jnp.concatenate([qi[:, h * dh:(h + 1) * dh] for h in range(heads)], axis=0).astype(BF16)
        k2 = k_ref[i].reshape(n_mem * heads, dh).astype(BF16)
        v2 = v_ref[i].reshape(n_mem * heads, dh).astype(BF16)
        s = lax.dot_general(qs, k2, (((1,), (1,)), ((), ())), preferred_element_type=F32) * scale
        s = jnp.where(own, s, -jnp.inf)
        s = s - jnp.max(s, axis=-1, keepdims=True)
        p = jnp.exp(s)
        p = p / jnp.sum(p, axis=-1, keepdims=True)
        o = jnp.dot(p.astype(BF16), v2, preferred_element_type=F32)
        outs.append(jnp.concatenate([o[h * rows:(h + 1) * rows, :] for h in range(heads)], axis=1))
    o_ref[...] = jnp.concatenate(outs, axis=0).astype(o_ref.dtype)


def _memattn_dec(q, k, v, *, layer, row_offset, n_seq, rows, nb):
    d = q.shape[-1]
    _, _, n_mem, heads, dh = k.shape
    blk = nb * rows
    assert row_offset % blk == 0 and n_seq % nb == 0
    kv_spec = pl.BlockSpec((None, nb, n_mem, heads, dh), lambda s: (layer, s, 0, 0, 0))
    return pl.pallas_call(
        functools.partial(_memattn_dec_body, nb=nb, rows=rows, heads=heads, scale=1.0 / math.sqrt(dh)),
        out_shape=jax.ShapeDtypeStruct((n_seq * rows, d), BF16),
        grid=(n_seq // nb,),
        in_specs=[pl.BlockSpec((blk, d), lambda s: (row_offset // blk + s, 0)), kv_spec, kv_spec],
        out_specs=pl.BlockSpec((blk, d), lambda s: (s, 0)),
        compiler_params=_cparams("parallel"),
        name="memattn_dec",
    )(q, k, v)


def _split3_lanes(v, heads):
    lane = lax.broadcasted_iota(jnp.int32, v.shape, 1)
    hi = v.astype(BF16).astype(F32)
    r1 = v - hi
    mid = r1.astype(BF16).astype(F32)
    lo = r1 - mid
    out = jnp.where(lane < heads, hi, jnp.where(lane < 2 * heads, mid, jnp.where(lane < 3 * heads, lo, 0.0)))
    return out.astype(BF16)


def _ssd_body(z_ref, x_ref, bc_ref, dt_ref, cw_ref, cb_ref, dtb_ref, alog_ref, dsk_ref, ng_ref, e3_ref,
              conv0_ref, s0_ref,
              y_ref, sfin_ref, ctail_ref,
              extx_ref, extbc_ref, st_ref, *, T, heads, groups, has_init):
    c = pl.program_id(1)
    nc = pl.num_programs(1)
    d_inner = x_ref.shape[-1]
    hpg = heads // groups
    gw = d_inner // groups
    hd = d_inner // heads
    n = D_STATE

    @pl.when(c == 0)
    def _():
        if has_init:
            extx_ref[0:8, :] = conv0_ref[0, :, 0:d_inner]
            extbc_ref[0:8, :] = conv0_ref[0, :, d_inner:]
            st_ref[...] = s0_ref[0].T
        else:
            extx_ref[0:8, :] = jnp.zeros((8, d_inner), F32)
            extbc_ref[0:8, :] = jnp.zeros((8, extbc_ref.shape[-1]), F32)
            st_ref[...] = jnp.zeros(st_ref.shape, F32)

    extx_ref[8:8 + T, :] = x_ref[...]
    extbc_ref[8:8 + T, :] = bc_ref[...]

    def conv(ext_ref, lo, width):
        cur = ext_ref[8:8 + T, :]
        tail = ext_ref[0:8, :]
        r8 = lax.broadcasted_iota(jnp.int32, (8, width), 0)
        kw = cw_ref.shape[0]
        acc = cb_ref[:, lo:lo + width] + cur * cw_ref[kw - 1:kw, lo:lo + width]
        for s in range(1, kw):
            rolled = pltpu.roll(cur, s, axis=0)
            head = jnp.where(r8 < s, pltpu.roll(tail, s, axis=0), rolled[0:8])
            shifted = head if T == 8 else jnp.concatenate([head, rolled[8:]], axis=0)
            acc = acc + shifted * cw_ref[kw - 1 - s:kw - s, lo:lo + width]
        return acc * jax.nn.sigmoid(acc)

    xc = conv(extx_ref, 0, d_inner)
    bcc = conv(extbc_ref, d_inner, extbc_ref.shape[-1])
    tail_x = extx_ref[T:T + 8, :]
    tail_bc = extbc_ref[T:T + 8, :]
    extx_ref[0:8, :] = tail_x
    extbc_ref[0:8, :] = tail_bc

    @pl.when(c == nc - 1)
    def _():
        ctail_ref[0, :, 0:d_inner] = tail_x
        ctail_ref[0, :, d_inner:] = tail_bc

    dtr = dt_ref[...] + dtb_ref[...]
    dtv = jnp.maximum(dtr, 0.0) + jnp.log1p(jnp.exp(-jnp.abs(dtr)))
    a = dtv * (-jnp.exp(alog_ref[...]))
    row = lax.broadcasted_iota(jnp.int32, a.shape, 0)
    acs = a
    sh = 1
    while sh < T:
        acs = acs + jnp.where(row >= sh, pltpu.roll(acs, sh, axis=0), 0.0)
        sh *= 2
    a_last = acs[T - 1:T, :]
    e3 = e3_ref[...]

    def expand(v):
        return jnp.dot(_split3_lanes(v, heads), e3, preferred_element_type=F32)

    dt_e = expand(dtv)
    dend_e = expand(jnp.exp(a_last - acs))
    eacs_e = expand(jnp.exp(acs))
    xdt = xc * dt_e
    xdt_b = xdt.astype(BF16)
    xw_b = (xdt * dend_e).astype(BF16)
    cdec_e = eacs_e[T - 1:T, :]

    if T < V7X_LANES:
        acs_p = jnp.concatenate([acs, jnp.zeros((V7X_LANES - T, V7X_LANES), F32)], axis=0)
    else:
        acs_p = acs
    acs_t = acs_p.T
    ti = lax.broadcasted_iota(jnp.int32, (T, T), 0)
    si = lax.broadcasted_iota(jnp.int32, (T, T), 1)
    causal = ti >= si
    lane_g = lax.broadcasted_iota(jnp.int32, (T, gw), 1)

    z = z_ref[...]
    gate = z * jax.nn.sigmoid(z)
    for g in range(groups):
        bg = bcc[:, g * n:(g + 1) * n]
        cg = bcc[:, groups * n + g * n: groups * n + (g + 1) * n].astype(BF16)
        bg_b = bg.astype(BF16)
        gmat = lax.dot_general(cg, bg_b, (((1,), (1,)), ((), ())), preferred_element_type=F32)
        st_g = st_ref[:, g * gw:(g + 1) * gw]
        y_g = jnp.dot(cg, st_g.astype(BF16), preferred_element_type=F32) * eacs_e[:, g * gw:(g + 1) * gw]
        xg = xdt_b[:, g * gw:(g + 1) * gw]
        for j in range(hpg):
            h = g * hpg + j
            seg = acs[:, h:h + 1] - acs_t[h:h + 1, 0:T]
            decay = jnp.exp(jnp.where(causal, seg, -jnp.inf))
            m_h = (gmat * decay).astype(BF16)
            x_h = jnp.where((lane_g >= j * hd) & (lane_g < (j + 1) * hd), xg, jnp.zeros_like(xg))
            y_g = y_g + jnp.dot(m_h, x_h, preferred_element_type=F32)
        cs_t = jnp.dot(bg.T.astype(BF16), xw_b[:, g * gw:(g + 1) * gw], preferred_element_type=F32)
        st_ref[:, g * gw:(g + 1) * gw] = st_g * cdec_e[:, g * gw:(g + 1) * gw] + cs_t
        y_g = y_g + xc[:, g * gw:(g + 1) * gw] * dsk_ref[:, g * gw:(g + 1) * gw]
        y_g = y_g * gate[:, g * gw:(g + 1) * gw]
        y_g = _rms(y_g, ng_ref[:, g * gw:(g + 1) * gw])
        y_ref[:, g * gw:(g + 1) * gw] = y_g.astype(y_ref.dtype)

    @pl.when(c == nc - 1)
    def _():
        sfin_ref[0] = st_ref[...].T


def _ssd(zx, conv_w, conv_b, dt_bias_rep, a_log_rep, d_exp, norm_g, e3, *, row_offset, nb, L, T,
         conv0=None, s0=None, layer=0):
    heads = d_exp.shape[-1] // SSD_HEADDIM
    d_inner = d_exp.shape[-1]
    bcw = 2 * SSD_GROUPS * D_STATE
    assert bcw == d_inner, "column blocks are indexed in units of d_inner"
    nc = L // T
    assert row_offset % T == 0
    rb0 = row_offset // T
    has_init = conv0 is not None
    rowmap = lambda col: (lambda b, c: (rb0 + b * nc + c, col))
    const = lambda b, c: (0, 0)
    per_b = lambda b, c: (b, 0, 0)
    if not has_init:
        conv0 = jnp.zeros((1, 8, d_inner + bcw), F32)
        s0 = jnp.zeros((1, 8, D_STATE), F32)
        init_specs = [pl.BlockSpec((1, 8, d_inner + bcw), lambda b, c: (0, 0, 0)),
                      pl.BlockSpec((1, 8, D_STATE), lambda b, c: (0, 0, 0))]
    else:
        init_specs = [pl.BlockSpec((1, 8, d_inner + bcw), per_b),
                      pl.BlockSpec((None, 1, d_inner, D_STATE), lambda b, c: (layer, b, 0, 0))]
    dt_col = (2 * d_inner + bcw) // V7X_LANES
    y, sfin, ctail = pl.pallas_call(
        functools.partial(_ssd_body, T=T, heads=heads, groups=SSD_GROUPS, has_init=has_init),
        out_shape=(jax.ShapeDtypeStruct((nb * L, d_inner), BF16),
                   jax.ShapeDtypeStruct((nb, d_inner, D_STATE), F32),
                   jax.ShapeDtypeStruct((nb, 8, d_inner + bcw), F32)),
        grid=(nb, nc),
        in_specs=[
            pl.BlockSpec((T, d_inner), rowmap(0)),
            pl.BlockSpec((T, d_inner), rowmap(1)),
            pl.BlockSpec((T, bcw), rowmap(2)),
            pl.BlockSpec((T, V7X_LANES), rowmap(dt_col)),
            pl.BlockSpec((4, d_inner + bcw), const),
            pl.BlockSpec((1, d_inner + bcw), const),
            pl.BlockSpec((1, V7X_LANES), const),
            pl.BlockSpec((1, V7X_LANES), const),
            pl.BlockSpec((1, d_inner), const),
            pl.BlockSpec((1, d_inner), const),
            pl.BlockSpec((V7X_LANES, d_inner), const),
        ] + init_specs,
        out_specs=(pl.BlockSpec((T, d_inner), lambda b, c: (b * nc + c, 0)),
                   pl.BlockSpec((1, d_inner, D_STATE), per_b),
                   pl.BlockSpec((1, 8, d_inner + bcw), per_b)),
        scratch_shapes=[pltpu.VMEM((T + 8, d_inner), F32), pltpu.VMEM((T + 8, bcw), F32),
                        pltpu.VMEM((D_STATE, d_inner), F32)],
        compiler_params=_cparams("parallel", "arbitrary"),
        name="ssd",
    )(zx, zx, zx, zx, conv_w, conv_b, dt_bias_rep, a_log_rep, d_exp, norm_g, e3, conv0, s0)
    return y, sfin, ctail


def _ssd_w_in_pad(w_in):
    heads = w_in.shape[1] - (w_in.shape[1] // V7X_LANES) * V7X_LANES
    main = w_in[:, :w_in.shape[1] - heads]
    dt = w_in[:, w_in.shape[1] - heads:]
    return jnp.concatenate([main] + [dt] * (V7X_MXU_COLS // heads), axis=1)


def _in_proj_tile(n):
    return max(t for t in range(V7X_MXU_COLS, 1536 + 1, V7X_MXU_COLS) if n % t == 0)


def _ssd_params(conv_w, conv_b, dt_bias, a_log, d_skip, norm_g):
    heads = dt_bias.shape[0]
    rep = V7X_LANES // heads
    d_inner = heads * SSD_HEADDIM
    src = jnp.arange(V7X_LANES)[:, None]
    dst_head = jnp.arange(d_inner)[None, :] // SSD_HEADDIM
    e3 = ((src % heads == dst_head) & (src < 3 * heads)).astype(BF16)
    return (conv_w.astype(F32), conv_b.reshape(1, -1).astype(F32),
            jnp.tile(dt_bias.reshape(1, heads), (1, rep)).astype(F32),
            jnp.tile(a_log.reshape(1, heads), (1, rep)).astype(F32),
            jnp.repeat(d_skip, SSD_HEADDIM).reshape(1, d_inner).astype(F32),
            norm_g.reshape(1, d_inner).astype(F32), e3)


SLOT = V7X_LANES
VT_SLOT = V_HEAD + 16
LOG2E = math.log2(math.e)


def _mla_proj_body(x_ref, g_ref, wd_ref, qn_ref, kvn_ref, wuq_ref, wuqr_ref, wuk_ref, wuvt_ref, cos_ref, sin_ref,
                   q_ref, qt_ref, kk_ref, vt_ref, lat_ref, kr_ref, *, heads, q_lora, kv_lora, rope, scale):
    xn = _rms(x_ref[...], g_ref[...]).astype(BF16)
    down = jnp.dot(xn, wd_ref[...], preferred_element_type=F32)
    cq = down[:, :q_lora]
    ckv = down[:, q_lora:q_lora + kv_lora]
    krs = down[:, q_lora + kv_lora:q_lora + kv_lora + SLOT]
    krr = down[:, q_lora + kv_lora + SLOT:]
    cos = cos_ref[...]
    sin = sin_ref[...]
    cqn = _rms(cq, qn_ref[...]).astype(BF16)
    qp = jnp.dot(cqn, wuq_ref[...], preferred_element_type=F32)
    qr = jnp.dot(cqn, wuqr_ref[...], preferred_element_type=F32)
    lat = _rms(ckv, kvn_ref[...])
    lat_ref[...] = lat
    kr_rot = krs * cos + krr * sin
    kr_ref[...] = kr_rot[:, QK_NOPE:QK_NOPE + rope]
    latb = lat.astype(BF16)
    kn = jnp.dot(latb, wuk_ref[...], preferred_element_type=F32)
    for h in range(heads):
        sl = slice(h * SLOT, (h + 1) * SLOT)
        q_h = (qp[:, sl] * cos + qr[:, sl] * sin) * scale
        q_ref[:, sl] = q_h.astype(BF16)
        qt_ref[sl, :] = q_h.T.astype(BF16)
        kk_ref[:, sl] = (kn[:, sl] + kr_rot).astype(BF16)
    vt = lax.dot_general(wuvt_ref[...], latb, (((1,), (1,)), ((), ())),
                         preferred_element_type=F32)
    fill_rows = lax.broadcasted_iota(jnp.int32, (VT_SLOT - V_HEAD, vt.shape[1]), 0)
    ones_then_zeros = jnp.where(fill_rows == 0, 1.0, 0.0).astype(BF16)
    for h in range(heads):
        vt_ref[h * VT_SLOT:h * VT_SLOT + V_HEAD, :] = vt[h * V_HEAD:(h + 1) * V_HEAD, :].astype(BF16)
        vt_ref[h * VT_SLOT + V_HEAD:(h + 1) * VT_SLOT, :] = ones_then_zeros


def _mla_proj(x, g, w, cos_tab, sin_tab, tab_index, *, tm=512):
    m, d = x.shape
    heads = w["wuq"].shape[1] // SLOT
    q_lora = w["qn"].shape[-1]
    kv_lora = w["kvn"].shape[-1]
    rope = w["rope"]
    tm = _row_tile(m, tm)
    const = lambda i: (0, 0)
    rowb = lambda i: (i, 0)
    full = lambda a: pl.BlockSpec(a.shape, const)
    return pl.pallas_call(
        functools.partial(_mla_proj_body, heads=heads, q_lora=q_lora, kv_lora=kv_lora, rope=rope,
                          scale=w["scale"]),
        out_shape=(jax.ShapeDtypeStruct((m, heads * SLOT), BF16),
                   jax.ShapeDtypeStruct((heads * SLOT, m), BF16),
                   jax.ShapeDtypeStruct((m, heads * SLOT), BF16),
                   jax.ShapeDtypeStruct((heads * VT_SLOT, m), BF16),
                   jax.ShapeDtypeStruct((m, kv_lora), F32),
                   jax.ShapeDtypeStruct((m, rope), F32)),
        grid=(m // tm,),
        in_specs=[pl.BlockSpec((tm, d), rowb), pl.BlockSpec((1, d), const),
                  full(w["wd"]), full(w["qn"]), full(w["kvn"]), full(w["wuq"]), full(w["wuqr"]),
                  full(w["wuk"]), full(w["wuvt"]),
                  pl.BlockSpec((tm, SLOT), lambda i: (tab_index(i), 0)),
                  pl.BlockSpec((tm, SLOT), lambda i: (tab_index(i), 0))],
        out_specs=(pl.BlockSpec((tm, heads * SLOT), rowb), pl.BlockSpec((heads * SLOT, tm), lambda i: (0, i)),
                   pl.BlockSpec((tm, heads * SLOT), rowb), pl.BlockSpec((heads * VT_SLOT, tm), lambda i: (0, i)),
                   pl.BlockSpec((tm, kv_lora), rowb), pl.BlockSpec((tm, rope), rowb)),
        compiler_params=_cparams("parallel"),
        name="mla_proj",
    )(x, g.reshape(1, d).astype(F32), w["wd"], w["qn"], w["kvn"], w["wuq"], w["wuqr"], w["wuk"], w["wuvt"],
      cos_tab, sin_tab)


def _mla_weights(w_down, q_norm, kv_norm, w_uq, w_uk, w_uv):
    q_lora = q_norm.shape[0]
    kv_lora = kv_norm.shape[0]
    heads = w_uq.shape[1]
    qk = w_uq.shape[2]
    rope = qk - QK_NOPE
    half = rope // 2
    pad = SLOT - qk

    def slot_pair(wr):
        z_lo = jnp.zeros(wr.shape[:-1] + (QK_NOPE,), wr.dtype)
        z_hi = jnp.zeros(wr.shape[:-1] + (pad,), wr.dtype)
        plain = jnp.concatenate([z_lo, wr, z_hi], axis=-1)
        rot = jnp.concatenate([z_lo, -wr[..., half:], wr[..., :half], z_hi], axis=-1)
        return plain, rot

    w_kr = w_down[:, q_lora + kv_lora:]
    kr_plain, kr_rot = slot_pair(w_kr)
    wd = jnp.concatenate([w_down[:, :q_lora + kv_lora], kr_plain, kr_rot], axis=1)
    uq_nope = jnp.concatenate([w_uq[..., :QK_NOPE], jnp.zeros(w_uq.shape[:2] + (SLOT - QK_NOPE,), w_uq.dtype)], -1)
    uq_plain, uq_rot = slot_pair(w_uq[..., QK_NOPE:])
    wuq = (uq_nope + uq_plain).reshape(q_lora, heads * SLOT)
    wuqr = uq_rot.reshape(q_lora, heads * SLOT)
    wuk = jnp.concatenate([w_uk, jnp.zeros(w_uk.shape[:2] + (SLOT - QK_NOPE,), w_uk.dtype)], -1)
    wuk = wuk.reshape(kv_lora, heads * SLOT)
    wuvt = w_uv.reshape(kv_lora, heads * V_HEAD).T
    eye_r = jnp.zeros((SLOT, SLOT), w_uk.dtype).at[QK_NOPE + jnp.arange(rope), jnp.arange(rope)].set(1.0)
    uk_t = jnp.transpose(w_uk, (1, 2, 0))
    uk_t = jnp.concatenate([uk_t, jnp.zeros((heads, SLOT - QK_NOPE, kv_lora), w_uk.dtype)], axis=1)
    wabs = jnp.concatenate([uk_t, jnp.broadcast_to(eye_r, (heads, SLOT, SLOT))], axis=2)
    uv = jnp.transpose(w_uv, (1, 0, 2)).reshape(heads // 2, 2, kv_lora, V_HEAD)
    z = jnp.zeros((heads // 2, kv_lora, V_HEAD), w_uv.dtype)
    wuv_bd = jnp.concatenate([jnp.concatenate([uv[:, 0], z], axis=2), jnp.concatenate([z, uv[:, 1]], axis=2)], axis=1)
    return dict(wd=wd.astype(BF16), qn=q_norm.reshape(1, -1).astype(F32), kvn=kv_norm.reshape(1, -1).astype(F32),
                wuq=wuq.astype(BF16), wuqr=wuqr.astype(BF16), wuk=wuk.astype(BF16), wuvt=wuvt.astype(BF16),
                wabs=wabs.astype(BF16), wuv_bd=wuv_bd.astype(BF16), rope=rope,
                scale=LOG2E / math.sqrt(qk))


def _rope_tables(positions, rope):
    half = rope // 2
    inv = ROPE_THETA ** (-jnp.arange(half, dtype=F32) * (2.0 / rope))
    ang = positions.astype(F32)[:, None] * inv[None, :]
    c, s = jnp.cos(ang), jnp.sin(ang)
    p = positions.shape[0]
    ones = jnp.ones((p, QK_NOPE), F32)
    zeros = jnp.zeros((p, QK_NOPE), F32)
    zpad = jnp.zeros((p, SLOT - QK_NOPE - rope), F32)
    return (jnp.concatenate([ones, c, c, zpad], axis=1), jnp.concatenate([zeros, s, s, zpad], axis=1))


def _headmm_body(x_ref, w_ref, o_ref):
    o_ref[...] = jnp.dot(x_ref[...], w_ref[...], preferred_element_type=F32).astype(o_ref.dtype)


def _headmm(x, w, *, out_dtype=BF16):
    m = x.shape[0]
    g, kb, nb = w.shape
    assert x.shape[1] == g * kb
    return pl.pallas_call(
        _headmm_body,
        out_shape=jax.ShapeDtypeStruct((m, g * nb), out_dtype),
        grid=(g,),
        in_specs=[pl.BlockSpec((m, kb), lambda h: (0, h)), pl.BlockSpec((None, kb, nb), lambda h: (h, 0, 0))],
        out_specs=pl.BlockSpec((m, nb), lambda h: (0, h)),
        compiler_params=_cparams("parallel"),
        name="headmm",
    )(x, w)


def _flash_body(qi_ref, ki_ref, qt_ref, k_ref, vt_ref, o_ref, m_ref, acc_ref, *, heads, tq):
    p_idx = pl.program_id(1)
    qi = qi_ref[p_idx]
    ki = ki_ref[p_idx]

    @pl.when(ki == 0)
    def _():
        m_ref[...] = jnp.full(m_ref.shape, -jnp.inf, F32)
        acc_ref[...] = jnp.zeros(acc_ref.shape, F32)

    def step(masked):
        if masked:
            key_i = lax.broadcasted_iota(jnp.int32, (tq, tq), 0)
            qry_i = lax.broadcasted_iota(jnp.int32, (tq, tq), 1)
            keep = key_i <= qry_i
        def scores(h):
            k = k_ref[:, h * SLOT:(h + 1) * SLOT]
            qt = qt_ref[h * SLOT:(h + 1) * SLOT, :]
            return jnp.dot(k, qt, preferred_element_type=F32)

        ahead = [scores(0), scores(1)]
        for h in range(heads):
            s = ahead.pop(0)
            if h + 2 < heads:
                ahead.append(scores(h + 2))
            if masked:
                s = jnp.where(keep, s, -jnp.inf)
            m_prev = m_ref[h:h + 1, :]
            m_new = jnp.maximum(m_prev, jnp.max(s, axis=0, keepdims=True))
            alpha = jnp.exp2(m_prev - m_new)
            p = jnp.exp2(s - m_new)
            m_ref[h:h + 1, :] = m_new
            vt = vt_ref[h * VT_SLOT:(h + 1) * VT_SLOT, :]
            pv = jnp.dot(vt, p.astype(BF16), preferred_element_type=F32)
            acc_ref[h * VT_SLOT:(h + 1) * VT_SLOT, :] = acc_ref[h * VT_SLOT:(h + 1) * VT_SLOT, :] * alpha + pv

    @pl.when(ki < qi)
    def _():
        step(False)

    @pl.when(ki == qi)
    def _():
        step(True)
        outs = []
        for h in range(heads):
            l = acc_ref[h * VT_SLOT + V_HEAD:h * VT_SLOT + V_HEAD + 1, :]
            outs.append(acc_ref[h * VT_SLOT:h * VT_SLOT + V_HEAD, :] / l)
        o_ref[...] = jnp.concatenate(outs, axis=0).T.astype(o_ref.dtype)


def _flash(qt, kk, vt, *, nb, L, tq=512):
    heads = kk.shape[1] // SLOT
    tq = _row_tile(L, tq)
    nq = L // tq
    pairs = [(i, j) for i in range(nq) for j in range(i + 1)]
    qi_tab = jnp.asarray([p[0] for p in pairs], jnp.int32)
    ki_tab = jnp.asarray([p[1] for p in pairs], jnp.int32)
    grid_spec = pltpu.PrefetchScalarGridSpec(
        num_scalar_prefetch=2,
        grid=(nb, len(pairs)),
        in_specs=[
            pl.BlockSpec((heads * SLOT, tq), lambda b, p, qi, ki: (0, b * nq + qi[p])),
            pl.BlockSpec((tq, heads * SLOT), lambda b, p, qi, ki: (b * nq + ki[p], 0)),
            pl.BlockSpec((heads * VT_SLOT, tq), lambda b, p, qi, ki: (0, b * nq + ki[p])),
        ],
        out_specs=pl.BlockSpec((tq, heads * V_HEAD), lambda b, p, qi, ki: (b * nq + qi[p], 0)),
        scratch_shapes=[pltpu.VMEM((heads, tq), F32), pltpu.VMEM((heads * VT_SLOT, tq), F32)],
    )
    return pl.pallas_call(
        functools.partial(_flash_body, heads=heads, tq=tq),
        out_shape=jax.ShapeDtypeStruct((nb * L, heads * V_HEAD), BF16),
        grid_spec=grid_spec,
        compiler_params=_cparams("parallel", "arbitrary"),
        name="mla_flash",
    )(qi_tab, ki_tab, qt, kk, vt)


def _decode_body(pt_ref, q_ref, cnew_ref, rnew_ref, lat_hbm, ropet_hbm, o_ref,
                 cbuf, rbuf, cb, sem, *, layer, n_pages, page, chunk_pages, kv_lora, rope, heads, ls):
    b = pl.program_id(0)
    nb = pl.num_programs(0)
    slot = lax.rem(b, 2)
    rows = q_ref.shape[0]
    chunk = chunk_pages * page

    def page_copies(seq, sl):
        cps = []
        for i in range(n_pages):
            pg = pt_ref[seq, i]
            cps.append(pltpu.make_async_copy(lat_hbm.at[layer, pg], cbuf.at[sl, pl.ds(i * page, page), :], sem.at[0, sl]))
            cps.append(pltpu.make_async_copy(ropet_hbm.at[layer, pg], rbuf.at[sl, i], sem.at[1, sl]))
        return cps

    @pl.when(b == 0)
    def _():
        for cp in page_copies(0, 0):
            cp.start()

    @pl.when(b + 1 < nb)
    def _():
        for cp in page_copies(b + 1, 1 - slot):
            cp.start()

    for cp in page_copies(b, slot):
        cp.wait()

    q_lat = q_ref[:, :kv_lora]
    q_r = q_ref[:, kv_lora:kv_lora + rope]
    nt = (((1,), (1,)), ((), ()))
    n_chunks = n_pages // chunk_pages

    def scores(ck):
        c_b = cbuf[slot, ck * chunk:(ck + 1) * chunk, :].astype(BF16)
        cb[ck * chunk:(ck + 1) * chunk, :] = c_b
        r_b = jnp.concatenate([rbuf[slot, ck * chunk_pages + i].astype(BF16) for i in range(chunk_pages)],
                              axis=1)
        return (lax.dot_general(q_lat, c_b, nt, preferred_element_type=F32)
                + jnp.dot(q_r, r_b, preferred_element_type=F32))

    cn = cnew_ref[...].astype(BF16)
    rn = rnew_ref[...].astype(BF16)
    sn = (lax.dot_general(q_lat, cn, nt, preferred_element_type=F32)
          + lax.dot_general(q_r, rn, nt, preferred_element_type=F32))
    qpos = lax.broadcasted_iota(jnp.int32, (rows, ls), 0) // heads
    kpos = lax.broadcasted_iota(jnp.int32, (rows, ls), 1)
    sn = jnp.where(qpos >= kpos, sn, -jnp.inf)
    m = jnp.max(sn, axis=-1, keepdims=True)
    pn = jnp.exp2(sn - m)
    l = jnp.sum(pn, axis=-1, keepdims=True)
    acc = jnp.dot(pn.astype(BF16), cn, preferred_element_type=F32)

    s_next = scores(0)
    for ck in range(n_chunks):
        s = s_next
        if ck + 1 < n_chunks:
            s_next = scores(ck + 1)
        m_new = jnp.maximum(m, jnp.max(s, axis=-1, keepdims=True))
        alpha = jnp.exp2(m - m_new)
        p = jnp.exp2(s - m_new)
        l = alpha * l + jnp.sum(p, axis=-1, keepdims=True)
        acc = alpha * acc + jnp.dot(p.astype(BF16), cb[ck * chunk:(ck + 1) * chunk, :], preferred_element_type=F32)
        m = m_new
    o_ref[...] = (acc / l).astype(o_ref.dtype)


def _decode(q_ext, lat_new, kr_new, lat_pool, ropet_pool, page_table, *, layer, bs, ls, heads, new_row_offset,
            chunk_pages=16):
    kv_lora = lat_new.shape[1]
    rope = kr_new.shape[1]
    page = lat_pool.shape[2]
    n_pages = page_table.shape[1]
    chunk_pages = math.gcd(chunk_pages, n_pages)
    assert new_row_offset % ls == 0
    rows = ls * heads
    qw = q_ext.shape[1]
    grid_spec = pltpu.PrefetchScalarGridSpec(
        num_scalar_prefetch=1,
        grid=(bs,),
        in_specs=[pl.BlockSpec((rows, qw), lambda b, pt: (b, 0)),
                  pl.BlockSpec((ls, kv_lora), lambda b, pt: (new_row_offset // ls + b, 0)),
                  pl.BlockSpec((ls, rope), lambda b, pt: (new_row_offset // ls + b, 0)),
                  pl.BlockSpec(memory_space=pl.ANY),
                  pl.BlockSpec(memory_space=pl.ANY)],
        out_specs=pl.BlockSpec((rows, kv_lora), lambda b, pt: (b, 0)),
        scratch_shapes=[pltpu.VMEM((2, n_pages * page, kv_lora), F32),
                        pltpu.VMEM((2, n_pages, rope, page), F32),
                        pltpu.VMEM((n_pages * page, kv_lora), BF16),
                        pltpu.SemaphoreType.DMA((2, 2))],
    )
    return pl.pallas_call(
        functools.partial(_decode_body, layer=layer, n_pages=n_pages, page=page, chunk_pages=chunk_pages,
                          kv_lora=kv_lora, rope=rope, heads=heads, ls=ls),
        out_shape=jax.ShapeDtypeStruct((bs * rows, kv_lora), BF16),
        grid_spec=grid_spec,
        compiler_params=_cparams("arbitrary"),
        name="mla_decode",
    )(page_table, q_ext, lat_new, kr_new, lat_pool, ropet_pool)


MEM_ROWS = 1024
MEM_NB = 4
PROJ_TM = 512


def kernel(x_prompt, x_sample, mem_prompt, state_ssm, state_conv, cache_mla_latent, cache_mla_rope_k, cache_mem_k, cache_mem_v, page_table, norm_mix, norm_mem, norm_memkv, norm_ffn, norm_final, ssd_w_in, ssd_conv_w, ssd_conv_b, ssd_dt_bias, ssd_a_log, ssd_d, ssd_norm, ssd_w_out, mla_w_down, mla_q_norm, mla_kv_norm, mla_w_uq, mla_w_uk, mla_w_uv, mla_w_o, mem_w_q, mem_w_kv, mem_w_o, mlp_w_up, mlp_w_down):
    bp, lp, d = x_prompt.shape
    bs, ls, _ = x_sample.shape
    mp, ms = bp * lp, bs * ls
    depth = norm_mix.shape[0]
    n_mem = mem_prompt.shape[1]
    past_len = page_table.shape[1] * cache_mla_latent.shape[2]
    mla_heads = mla_w_uq.shape[2]
    rope = mla_w_uq.shape[3] - QK_NOPE
    ssd_heads = ssd_dt_bias.shape[1]
    d_inner = ssd_heads * SSD_HEADDIM

    x = jnp.concatenate([x_prompt.reshape(mp, d), x_sample.reshape(ms, d)], axis=0)
    mem_rows = mem_prompt.reshape(bp * n_mem, d)
    ropet_pool = jnp.swapaxes(cache_mla_rope_k, 2, 3)
    ssm0 = state_ssm.reshape(state_ssm.shape[0], bs, d_inner, D_STATE)

    pos = jnp.concatenate([jnp.arange(lp), jnp.tile(past_len + jnp.arange(ls), bs)])
    cos_tab, sin_tab = _rope_tables(pos, rope)
    proj_tm = _row_tile(ms, PROJ_TM)
    assert lp % proj_tm == 0
    npt, tpl = mp // proj_tm, lp // proj_tm
    tab_index = lambda i: jnp.where(i < npt, i % tpl, tpl + (i - npt))

    mem_rows_p = _row_tile(lp, MEM_ROWS)
    mem_nb = _row_tile(bs, MEM_NB)

    p_ssm, p_conv, p_lat, p_rk, p_mk, p_mv = [], [], [], [], [], []
    s_ssm, s_conv, s_lat, s_rk = [], [], [], []
    for i in range(depth):
        j = i // 2
        if i % 2 == 0:
            w_pad = _ssd_w_in_pad(ssd_w_in[j]).astype(BF16)
            zx = _mm(x, w_pad, g=norm_mix[i], tn=_in_proj_tile(w_pad.shape[1]), name="ssd_in")
            prm = _ssd_params(ssd_conv_w[j], ssd_conv_b[j], ssd_dt_bias[j], ssd_a_log[j], ssd_d[j], ssd_norm[j])
            y_p, st_p, ct_p = _ssd(zx, *prm, row_offset=0, nb=bp, L=lp, T=math.gcd(SSD_CHUNK, lp))
            conv0 = jnp.pad(state_conv[j], ((0, 0), (8 - state_conv.shape[2], 0), (0, 0)))
            y_s, st_s, ct_s = _ssd(zx, *prm, row_offset=mp, nb=bs, L=ls, T=math.gcd(SSD_CHUNK, ls),
                                   conv0=conv0, s0=ssm0, layer=j)
            x = _mm(y_p, ssd_w_out[j].astype(BF16), x_tail=y_s, res=x, name="ssd_out")
            kc = state_conv.shape[2]
            p_ssm.append(st_p.reshape(bp, ssd_heads, SSD_HEADDIM, D_STATE))
            s_ssm.append(st_s.reshape(bs, ssd_heads, SSD_HEADDIM, D_STATE))
            p_conv.append(ct_p[:, 8 - kc:, :])
            s_conv.append(ct_s[:, 8 - kc:, :])
        else:
            w = _mla_weights(mla_w_down[j], mla_q_norm[j], mla_kv_norm[j], mla_w_uq[j], mla_w_uk[j], mla_w_uv[j])
            q, qt, kk, vt, lat, kr = _mla_proj(x, norm_mix[i], w, cos_tab, sin_tab, tab_index, tm=proj_tm)
            o_p = _flash(qt, kk, vt, nb=bp, L=lp)
            q_ext = _headmm(q[mp:], w["wabs"]).reshape(ms * mla_heads, -1)
            o_lat = _decode(q_ext, lat, kr, cache_mla_latent, ropet_pool, page_table, layer=j, bs=bs, ls=ls,
                            heads=mla_heads, new_row_offset=mp)
            o_s = _headmm(o_lat.reshape(ms, -1), w["wuv_bd"])
            x = _mm(o_p, mla_w_o[j].astype(BF16), x_tail=o_s, res=x, name="mla_out")
            p_lat.append(lat[:mp].reshape(bp, lp, -1))
            s_lat.append(lat[mp:].reshape(bs, ls, -1))
            p_rk.append(kr[:mp].reshape(bp, lp, -1))
            s_rk.append(kr[mp:].reshape(bs, ls, -1))
        kv = _mm(mem_rows, mem_w_kv[i].astype(BF16), g=norm_memkv[i], name="mem_kv")
        kp = kv[:, :d].reshape(bp, n_mem, d)
        vp = kv[:, d:].reshape(bp, n_mem, d)
        p_mk.append(kp.reshape(bp, n_mem, MEM_HEADS, d // MEM_HEADS))
        p_mv.append(vp.reshape(bp, n_mem, MEM_HEADS, d // MEM_HEADS))
        qm = _mm(x, mem_w_q[i].astype(BF16), g=norm_mem[i], name="mem_q")
        o_p = _memattn(qm, kp, vp, n_seq=bp, L=lp, rows=mem_rows_p)
        o_s = _memattn_dec(qm, cache_mem_k, cache_mem_v, layer=i, row_offset=mp, n_seq=bs, rows=ls, nb=mem_nb)
        x = _mm(o_p, mem_w_o[i].astype(BF16), x_tail=o_s, res=x, name="mem_out")
        x = _mlp(x, norm_ffn[i], mlp_w_up[i].astype(BF16), mlp_w_down[i].astype(BF16))
    y_p = _norm(x, norm_final, row_offset=0, rows=mp)
    y_s = _norm(x, norm_final, row_offset=mp, rows=ms)
    return (y_p.reshape(bp, lp, d), y_s.reshape(bs, ls, d),
            jnp.stack(p_ssm), jnp.stack(p_conv), jnp.stack(p_lat), jnp.stack(p_rk), jnp.stack(p_mk), jnp.stack(p_mv),
            jnp.stack(s_ssm), jnp.stack(s_conv), jnp.stack(s_lat), jnp.stack(s_rk))
```

```python
import functools
import math

import jax
import jax.numpy as jnp
from jax import lax
from jax.experimental import pallas as pl
from jax.experimental.pallas import tpu as pltpu

F32 = jnp.float32
BF16 = jnp.bfloat16

EPS = 1e-6
ROPE_THETA = 10000.0

V7X_LANES = 128
V7X_SUBLANES = 8
V7X_MXU_COLS = 256
V7X_VMEM_LIMIT_BYTES = 56 * 1024 * 1024

SSD_HEADDIM = 64
SSD_GROUPS = 8
D_STATE = 128
SSD_CHUNK = 128
QK_NOPE = 64
V_HEAD = 64
MEM_HEADS = 4


def _cparams(*sem):
    return pltpu.CompilerParams(dimension_semantics=sem, vmem_limit_bytes=V7X_VMEM_LIMIT_BYTES)


def _rms(x, g):
    return x * lax.rsqrt(jnp.mean(x * x, axis=-1, keepdims=True) + EPS) * g


def _row_tile(m, pref):
    t = min(pref, m)
    assert m % t == 0, (m, t)
    return t


def _mm_body(*refs, norm, res, act, head_tiles):
    it = iter(refs)
    x_ref = next(it)
    t_ref = next(it) if head_tiles is not None else None
    g_ref = next(it) if norm else None
    w_ref = next(it)
    r_ref = next(it) if res else None
    o_ref = next(it)
    xn_ref = next(it)

    def stage(src_ref):
        x = src_ref[...].astype(F32)
        if norm:
            x = _rms(x, g_ref[...])
        xn_ref[...] = x.astype(BF16)

    first = pl.program_id(1) == 0
    if head_tiles is None:
        pl.when(first)(lambda: stage(x_ref))
    else:
        in_head = pl.program_id(0) < head_tiles
        pl.when(first & in_head)(lambda: stage(x_ref))
        pl.when(first & jnp.logical_not(in_head))(lambda: stage(t_ref))

    acc = jnp.dot(xn_ref[...], w_ref[...], preferred_element_type=F32)
    if act == "relu2":
        acc = jnp.square(jnp.maximum(acc, 0.0))
    if res:
        acc = r_ref[...] + acc
    o_ref[...] = acc.astype(o_ref.dtype)


def _mm(x, w, *, x_tail=None, g=None, res=None, act=None, out_dtype=F32, tm=1024, tn=1024, name="mm"):
    m, k = x.shape
    k2, n = w.shape
    assert k == k2
    head_tiles = None
    if x_tail is None:
        tm = _row_tile(m, tm)
        in_specs = [pl.BlockSpec((tm, k), lambda i, j: (i, 0))]
        args = [x]
    else:
        m_tail = x_tail.shape[0]
        tm = math.gcd(math.gcd(m, m_tail), tm)
        head_tiles = m // tm
        m = m + m_tail
        in_specs = [pl.BlockSpec((tm, k), lambda i, j: (jnp.minimum(i, head_tiles - 1), 0)),
                    pl.BlockSpec((tm, k), lambda i, j: (jnp.maximum(i - head_tiles, 0), 0))]
        args = [x, x_tail]
    tn = _row_tile(n, tn)
    if g is not None:
        in_specs.append(pl.BlockSpec((1, k), lambda i, j: (0, 0)))
        args.append(g.reshape(1, k).astype(F32))
    in_specs.append(pl.BlockSpec((k, tn), lambda i, j: (0, j)))
    args.append(w)
    if res is not None:
        in_specs.append(pl.BlockSpec((tm, tn), lambda i, j: (i, j)))
        args.append(res)
    return pl.pallas_call(
        functools.partial(_mm_body, norm=g is not None, res=res is not None, act=act, head_tiles=head_tiles),
        out_shape=jax.ShapeDtypeStruct((m, n), out_dtype),
        grid=(m // tm, n // tn),
        in_specs=in_specs,
        out_specs=pl.BlockSpec((tm, tn), lambda i, j: (i, j)),
        scratch_shapes=[pltpu.VMEM((tm, k), BF16)],
        compiler_params=_cparams("parallel", "arbitrary"),
        name=name,
    )(*args)


def _mlp_body(*refs, head_tiles):
    x_ref = refs[0]
    t_ref = refs[1] if head_tiles is not None else None
    g_ref, wu_ref, wd_ref, o_ref, xn_ref, acc_ref = refs[1 + int(head_tiles is not None):]
    k = pl.program_id(1)

    def stage(src_ref):
        x = src_ref[...]
        xn_ref[...] = _rms(x, g_ref[...]).astype(BF16)
        acc_ref[...] = x

    if head_tiles is None:
        pl.when(k == 0)(lambda: stage(x_ref))
    else:
        in_head = pl.program_id(0) < head_tiles
        pl.when((k == 0) & in_head)(lambda: stage(x_ref))
        pl.when((k == 0) & jnp.logical_not(in_head))(lambda: stage(t_ref))

    h = jnp.dot(xn_ref[...], wu_ref[...], preferred_element_type=F32)
    h = jnp.square(jnp.maximum(h, 0.0)).astype(BF16)
    acc_ref[...] += jnp.dot(h, wd_ref[...], preferred_element_type=F32)

    @pl.when(k == pl.num_programs(1) - 1)
    def _():
        o_ref[...] = acc_ref[...]


def _mlp(x, g, w_up, w_down, *, x_tail=None, tm=1024, tf=512):
    m, d = x.shape
    ff = w_up.shape[1]
    head_tiles = None
    if x_tail is None:
        tm = _row_tile(m, tm)
        row_specs, row_args = [pl.BlockSpec((tm, d), lambda i, k: (i, 0))], [x]
    else:
        m_tail = x_tail.shape[0]
        tm = math.gcd(math.gcd(m, m_tail), tm)
        head_tiles = m // tm
        m = m + m_tail
        row_specs = [pl.BlockSpec((tm, d), lambda i, k: (jnp.minimum(i, head_tiles - 1), 0)),
                     pl.BlockSpec((tm, d), lambda i, k: (jnp.maximum(i - head_tiles, 0), 0))]
        row_args = [x, x_tail]
    tf = _row_tile(ff, tf)
    return pl.pallas_call(
        functools.partial(_mlp_body, head_tiles=head_tiles),
        out_shape=jax.ShapeDtypeStruct((m, d), F32),
        grid=(m // tm, ff // tf),
        in_specs=row_specs + [
            pl.BlockSpec((1, d), lambda i, k: (0, 0)),
            pl.BlockSpec((d, tf), lambda i, k: (0, k)),
            pl.BlockSpec((tf, d), lambda i, k: (k, 0)),
        ],
        out_specs=pl.BlockSpec((tm, d), lambda i, k: (i, 0)),
        scratch_shapes=[pltpu.VMEM((tm, d), BF16), pltpu.VMEM((tm, d), F32)],
        compiler_params=_cparams("parallel", "arbitrary"),
        name="mlp",
    )(*row_args, g.reshape(1, d).astype(F32), w_up, w_down)


def _norm_body(x_ref, g_ref, o_ref):
    o_ref[...] = _rms(x_ref[...], g_ref[...])


def _norm(x, g, *, row_offset, rows, tm=1024):
    d = x.shape[1]
    tm = math.gcd(math.gcd(rows, tm), row_offset) if row_offset else _row_tile(rows, tm)
    first = row_offset // tm
    return pl.pallas_call(
        _norm_body,
        out_shape=jax.ShapeDtypeStruct((rows, d), F32),
        grid=(rows // tm,),
        in_specs=[pl.BlockSpec((tm, d), lambda i: (first + i, 0)), pl.BlockSpec((1, d), lambda i: (0, 0))],
        out_specs=pl.BlockSpec((tm, d), lambda i: (i, 0)),
        compiler_params=_cparams("parallel"),
        name="final_norm",
    )(x, g.reshape(1, d).astype(F32))


def _mem_fused_body(x_ref, g_ref, wq_ref, k_ref, v_ref, wo_ref, o_ref, *, heads, scale):
    x = x_ref[...]
    q = jnp.dot(_rms(x, g_ref[...]).astype(BF16), wq_ref[...], preferred_element_type=F32)
    dh = q.shape[-1] // heads
    outs = []
    for h in range(heads):
        q_h = q[:, h * dh:(h + 1) * dh].astype(BF16)
        k_h = k_ref[:, h * dh:(h + 1) * dh]
        v_h = v_ref[:, h * dh:(h + 1) * dh]
        s = lax.dot_general(q_h, k_h, (((1,), (1,)), ((), ())), preferred_element_type=F32) * scale
        s = s - jnp.max(s, axis=-1, keepdims=True)
        p = jnp.exp(s)
        p = p / jnp.sum(p, axis=-1, keepdims=True)
        outs.append(jnp.dot(p.astype(BF16), v_h, preferred_element_type=F32).astype(BF16))
    o = jnp.concatenate(outs, axis=1)
    o_ref[...] = x + jnp.dot(o, wo_ref[...], preferred_element_type=F32)


def _mem_fused(x, g, wq, k, v, wo, *, n_seq, L, rows):
    d = x.shape[-1]
    n_mem = k.shape[-2]
    per_seq = L // rows
    const = lambda s: (0, 0)
    weight = lambda: pl.BlockSpec((d, d), const, pipeline_mode=pl.Buffered(1))
    return pl.pallas_call(
        functools.partial(_mem_fused_body, heads=MEM_HEADS, scale=1.0 / math.sqrt(d // MEM_HEADS)),
        out_shape=jax.ShapeDtypeStruct((n_seq * L, d), F32),
        grid=(n_seq * per_seq,),
        in_specs=[pl.BlockSpec((rows, d), lambda s: (s, 0)),
                  pl.BlockSpec((1, d), const),
                  weight(),
                  pl.BlockSpec((None, n_mem, d), lambda s: (s // per_seq, 0, 0)),
                  pl.BlockSpec((None, n_mem, d), lambda s: (s // per_seq, 0, 0)),
                  weight()],
        out_specs=pl.BlockSpec((rows, d), lambda s: (s, 0)),
        compiler_params=_cparams("parallel"),
        name="mem_fused",
    )(x, g.reshape(1, d).astype(F32), wq, k, v, wo)


def _memattn_dec_body(q_ref, k_ref, v_ref, o_ref, *, nb, rows, heads, scale):
    dh = q_ref.shape[-1] // heads
    n_mem = k_ref.shape[1]
    qrow_head = lax.broadcasted_iota(jnp.int32, (heads * rows, n_mem * heads), 0) // rows
    key_head = lax.broadcasted_iota(jnp.int32, (heads * rows, n_mem * heads), 1) % heads
    own = qrow_head == key_head
    outs = []
    q_all = q_ref[...].astype(F32)
    for i in range(nb):
        qi = q_all[i * rows:(i + 1) * rows, :]
        qs = jnp.concatenate([qi[:, h * dh:(h + 1) * dh] for h in range(heads)], axis=0).astype(BF16)
        k2 = k_ref[i].reshape(n_mem * heads, dh).astype(BF16)
        v2 = v_ref[i].reshape(n_mem * heads, dh).astype(BF16)
        s = lax.dot_general(qs, k2, (((1,), (1,)), ((), ())), preferred_element_type=F32) * scale
        s = jnp.where(own, s, -jnp.inf)
        s = s - jnp.max(s, axis=-1, keepdims=True)
        p = jnp.exp(s)
        p = p / jnp.sum(p, axis=-1, keepdims=True)
        o = jnp.dot(p.astype(BF16), v2, preferred_element_type=F32)
        outs.append(jnp.concatenate([o[h * rows:(h + 1) * rows, :] for h in range(heads)], axis=1))
    o_ref[...] = jnp.concatenate(outs, axis=0).astype(o_ref.dtype)


def _memattn_dec(q, k, v, *, layer, row_offset, n_seq, rows, nb):
    d = q.shape[-1]
    _, _, n_mem, heads, dh = k.shape
    blk = nb * rows
    assert row_offset % blk == 0 and n_seq % nb == 0
    kv_spec = pl.BlockSpec((None, nb, n_mem, heads, dh), lambda s: (layer, s, 0, 0, 0))
    return pl.pallas_call(
        functools.partial(_memattn_dec_body, nb=nb, rows=rows, heads=heads, scale=1.0 / math.sqrt(dh)),
        out_shape=jax.ShapeDtypeStruct((n_seq * rows, d), BF16),
        grid=(n_seq // nb,),
        in_specs=[pl.BlockSpec((blk, d), lambda s: (row_offset // blk + s, 0)), kv_spec, kv_spec],
        out_specs=pl.BlockSpec((blk, d), lambda s: (s, 0)),
        compiler_params=_cparams("parallel"),
        name="memattn_dec",
    )(q, k, v)


def _split3_lanes(v, heads):
    lane = lax.broadcasted_iota(jnp.int32, v.shape, 1)
    hi = v.astype(BF16).astype(F32)
    r1 = v - hi
    mid = r1.astype(BF16).astype(F32)
    lo = r1 - mid
    out = jnp.where(lane < heads, hi, jnp.where(lane < 2 * heads, mid, jnp.where(lane < 3 * heads, lo, 0.0)))
    return out.astype(BF16)


def _ssd_init(conv0_ref, s0_ref, extx_ref, extbc_ref, st_ref, *, has_init):
    d_inner = extx_ref.shape[-1]
    if has_init:
        extx_ref[0:8, :] = conv0_ref[0, :, 0:d_inner]
        extbc_ref[0:8, :] = conv0_ref[0, :, d_inner:]
        st_ref[...] = s0_ref[0].T
    else:
        extx_ref[0:8, :] = jnp.zeros((8, d_inner), F32)
        extbc_ref[0:8, :] = jnp.zeros((8, extbc_ref.shape[-1]), F32)
        st_ref[...] = jnp.zeros(st_ref.shape, F32)


def _ssd_finish(sfin_ref, ctail_ref, extx_ref, extbc_ref, st_ref, prev_ref=None):
    d_inner = extx_ref.shape[-1]
    if prev_ref is None:
        sfin_ref[0] = st_ref[...].T
    else:
        n_prev = prev_ref.shape[0]
        sfin_ref[0:n_prev] = prev_ref[...]
        sfin_ref[n_prev, 0] = st_ref[...].T
    ctail_ref[0, :, 0:d_inner] = extx_ref[0:8, :]
    ctail_ref[0, :, d_inner:] = extbc_ref[0:8, :]


def _scan_chunk(z, x_raw, bc_raw, dt_raw, cw_ref, cb_ref, dtb_ref, alog_ref, dsk_ref, ng_ref, e3_ref,
                y_ref, y_row0, extx_ref, extbc_ref, st_ref, *, T, heads, groups):
    d_inner = extx_ref.shape[-1]
    hpg = heads // groups
    gw = d_inner // groups
    hd = d_inner // heads
    n = D_STATE
    gate = z * jax.nn.sigmoid(z)

    extx_ref[8:8 + T, :] = x_raw
    extbc_ref[8:8 + T, :] = bc_raw

    def conv(ext_ref, lo, width):
        cur = ext_ref[8:8 + T, :]
        tail = ext_ref[0:8, :]
        r8 = lax.broadcasted_iota(jnp.int32, (8, width), 0)
        kw = cw_ref.shape[0]
        acc = cb_ref[:, lo:lo + width] + cur * cw_ref[kw - 1:kw, lo:lo + width]
        for s in range(1, kw):
            rolled = pltpu.roll(cur, s, axis=0)
            head = jnp.where(r8 < s, pltpu.roll(tail, s, axis=0), rolled[0:8])
            shifted = head if T == 8 else jnp.concatenate([head, rolled[8:]], axis=0)
            acc = acc + shifted * cw_ref[kw - 1 - s:kw - s, lo:lo + width]
        return acc * jax.nn.sigmoid(acc)

    xc = conv(extx_ref, 0, d_inner)
    bcc = conv(extbc_ref, d_inner, extbc_ref.shape[-1])
    extx_ref[0:8, :] = extx_ref[T:T + 8, :]
    extbc_ref[0:8, :] = extbc_ref[T:T + 8, :]

    dtr = dt_raw + dtb_ref[...]
    dtv = jnp.maximum(dtr, 0.0) + jnp.log1p(jnp.exp(-jnp.abs(dtr)))
    a = dtv * (-jnp.exp(alog_ref[...]))
    row = lax.broadcasted_iota(jnp.int32, a.shape, 0)
    acs = a
    sh = 1
    while sh < T:
        acs = acs + jnp.where(row >= sh, pltpu.roll(acs, sh, axis=0), 0.0)
        sh *= 2
    a_last = acs[T - 1:T, :]
    e3 = e3_ref[...]

    def expand(v):
        return jnp.dot(_split3_lanes(v, heads), e3, preferred_element_type=F32)

    dt_e = expand(dtv)
    dend_e = expand(jnp.exp(a_last - acs))
    eacs_e = expand(jnp.exp(acs))
    xdt = xc * dt_e
    xdt_b = xdt.astype(BF16)
    xw_b = (xdt * dend_e).astype(BF16)
    cdec_e = eacs_e[T - 1:T, :]

    if T < V7X_LANES:
        acs_p = jnp.concatenate([acs, jnp.zeros((V7X_LANES - T, V7X_LANES), F32)], axis=0)
    else:
        acs_p = acs
    acs_t = acs_p.T
    ti = lax.broadcasted_iota(jnp.int32, (T, T), 0)
    si = lax.broadcasted_iota(jnp.int32, (T, T), 1)
    causal = ti >= si
    lane_g = lax.broadcasted_iota(jnp.int32, (T, gw), 1)

    for g in range(groups):
        bg = bcc[:, g * n:(g + 1) * n]
        cg = bcc[:, groups * n + g * n: groups * n + (g + 1) * n].astype(BF16)
        bg_b = bg.astype(BF16)
        gmat = lax.dot_general(cg, bg_b, (((1,), (1,)), ((), ())), preferred_element_type=F32)
        st_g = st_ref[:, g * gw:(g + 1) * gw]
        y_g = jnp.dot(cg, st_g.astype(BF16), preferred_element_type=F32) * eacs_e[:, g * gw:(g + 1) * gw]
        xg = xdt_b[:, g * gw:(g + 1) * gw]
        for j in range(hpg):
            h = g * hpg + j
            seg = acs[:, h:h + 1] - acs_t[h:h + 1, 0:T]
            decay = jnp.exp(jnp.where(causal, seg, -jnp.inf))
            m_h = (gmat * decay).astype(BF16)
            x_h = jnp.where((lane_g >= j * hd) & (lane_g < (j + 1) * hd), xg, jnp.zeros_like(xg))
            y_g = y_g + jnp.dot(m_h, x_h, preferred_element_type=F32)
        cs_t = jnp.dot(bg.T.astype(BF16), xw_b[:, g * gw:(g + 1) * gw], preferred_element_type=F32)
        st_ref[:, g * gw:(g + 1) * gw] = st_g * cdec_e[:, g * gw:(g + 1) * gw] + cs_t
        y_g = y_g + xc[:, g * gw:(g + 1) * gw] * dsk_ref[:, g * gw:(g + 1) * gw]
        y_g = y_g * gate[:, g * gw:(g + 1) * gw]
        y_g = _rms(y_g, ng_ref[:, g * gw:(g + 1) * gw])
        y_ref[y_row0:y_row0 + T, g * gw:(g + 1) * gw] = y_g.astype(y_ref.dtype)


def _ssd_body(*refs, T, heads, groups, has_init, has_prev):
    (z_ref, x_ref, bc_ref, dt_ref, cw_ref, cb_ref, dtb_ref, alog_ref, dsk_ref, ng_ref, e3_ref,
     conv0_ref, s0_ref) = refs[:13]
    prev_ref = refs[13] if has_prev else None
    y_ref, sfin_ref, ctail_ref, extx_ref, extbc_ref, st_ref = refs[13 + int(has_prev):]
    c = pl.program_id(1)
    pl.when(c == 0)(functools.partial(_ssd_init, conv0_ref, s0_ref, extx_ref, extbc_ref, st_ref, has_init=has_init))
    _scan_chunk(z_ref[...], x_ref[...], bc_ref[...], dt_ref[...],
                cw_ref, cb_ref, dtb_ref, alog_ref, dsk_ref, ng_ref, e3_ref,
                y_ref, 0, extx_ref, extbc_ref, st_ref, T=T, heads=heads, groups=groups)
    pl.when(c == pl.num_programs(1) - 1)(
        functools.partial(_ssd_finish, sfin_ref, ctail_ref, extx_ref, extbc_ref, st_ref, prev_ref))


def _ssd_fused_body(xcur_ref, xnext_ref, g_ref, win_ref, cw_ref, cb_ref, dtb_ref, alog_ref, dsk_ref, ng_ref, e3_ref,
                    y_ref, sfin_ref, ctail_ref,
                    extx_ref, extbc_ref, st_ref, zxa_ref, zxb_ref, *, T, heads, groups):
    k = pl.program_id(1)
    d_inner = extx_ref.shape[-1]
    bcw = extbc_ref.shape[-1]

    def in_proj(rows):
        xn = _rms(rows, g_ref[...]).astype(BF16)
        return jnp.dot(xn, win_ref[...], preferred_element_type=F32)

    def scan(zx_ref, y_row0):
        _scan_chunk(zx_ref[:, 0:d_inner], zx_ref[:, d_inner:2 * d_inner],
                    zx_ref[:, 2 * d_inner:2 * d_inner + bcw],
                    zx_ref[:, 2 * d_inner + bcw:2 * d_inner + bcw + V7X_LANES],
                    cw_ref, cb_ref, dtb_ref, alog_ref, dsk_ref, ng_ref, e3_ref,
                    y_ref, y_row0, extx_ref, extbc_ref, st_ref, T=T, heads=heads, groups=groups)

    @pl.when(k == 0)
    def _():
        _ssd_init(None, None, extx_ref, extbc_ref, st_ref, has_init=False)
        zxa_ref[...] = in_proj(xcur_ref[0:T, :])

    zxb_ref[...] = in_proj(xcur_ref[T:2 * T, :])
    scan(zxa_ref, 0)
    zxa_ref[...] = in_proj(xnext_ref[0:T, :])
    scan(zxb_ref, T)
    pl.when(k == pl.num_programs(1) - 1)(
        functools.partial(_ssd_finish, sfin_ref, ctail_ref, extx_ref, extbc_ref, st_ref))


def _ssd(zx, conv_w, conv_b, dt_bias_rep, a_log_rep, d_exp, norm_g, e3, *, row_offset, nb, L, T,
         conv0=None, s0=None, layer=0, prev_states=None):
    heads = d_exp.shape[-1] // SSD_HEADDIM
    d_inner = d_exp.shape[-1]
    bcw = 2 * SSD_GROUPS * D_STATE
    assert bcw == d_inner, "column blocks are indexed in units of d_inner"
    nc = L // T
    assert row_offset % T == 0
    rb0 = row_offset // T
    has_init = conv0 is not None
    rowmap = lambda col: (lambda b, c: (rb0 + b * nc + c, col))
    const = lambda b, c: (0, 0)
    per_b = lambda b, c: (b, 0, 0)
    if not has_init:
        conv0 = jnp.zeros((1, 8, d_inner + bcw), F32)
        s0 = jnp.zeros((1, 8, D_STATE), F32)
        init_specs = [pl.BlockSpec((1, 8, d_inner + bcw), lambda b, c: (0, 0, 0)),
                      pl.BlockSpec((1, 8, D_STATE), lambda b, c: (0, 0, 0))]
    else:
        init_specs = [pl.BlockSpec((1, 8, d_inner + bcw), per_b),
                      pl.BlockSpec((None, 1, d_inner, D_STATE), lambda b, c: (layer, b, 0, 0))]
    dt_col = (2 * d_inner + bcw) // V7X_LANES
    if prev_states is None:
        prev_args, prev_specs = [], []
        sfin_shape = (nb, d_inner, D_STATE)
        sfin_spec = pl.BlockSpec((1, d_inner, D_STATE), per_b)
    else:
        n_prev = prev_states.shape[0]
        prev_args = [prev_states]
        prev_specs = [pl.BlockSpec((n_prev, 1, d_inner, D_STATE), lambda b, c: (0, b, 0, 0))]
        sfin_shape = (n_prev + 1, nb, d_inner, D_STATE)
        sfin_spec = pl.BlockSpec((n_prev + 1, 1, d_inner, D_STATE), lambda b, c: (0, b, 0, 0))
    y, sfin, ctail = pl.pallas_call(
        functools.partial(_ssd_body, T=T, heads=heads, groups=SSD_GROUPS, has_init=has_init,
                          has_prev=prev_states is not None),
        out_shape=(jax.ShapeDtypeStruct((nb * L, d_inner), BF16),
                   jax.ShapeDtypeStruct(sfin_shape, F32),
                   jax.ShapeDtypeStruct((nb, 8, d_inner + bcw), F32)),
        grid=(nb, nc),
        in_specs=[
            pl.BlockSpec((T, d_inner), rowmap(0)),
            pl.BlockSpec((T, d_inner), rowmap(1)),
            pl.BlockSpec((T, bcw), rowmap(2)),
            pl.BlockSpec((T, V7X_LANES), rowmap(dt_col)),
            pl.BlockSpec((4, d_inner + bcw), const),
            pl.BlockSpec((1, d_inner + bcw), const),
            pl.BlockSpec((1, V7X_LANES), const),
            pl.BlockSpec((1, V7X_LANES), const),
            pl.BlockSpec((1, d_inner), const),
            pl.BlockSpec((1, d_inner), const),
            pl.BlockSpec((V7X_LANES, d_inner), const),
        ] + init_specs + prev_specs,
        out_specs=(pl.BlockSpec((T, d_inner), lambda b, c: (b * nc + c, 0)),
                   sfin_spec,
                   pl.BlockSpec((1, 8, d_inner + bcw), per_b)),
        scratch_shapes=[pltpu.VMEM((T + 8, d_inner), F32), pltpu.VMEM((T + 8, bcw), F32),
                        pltpu.VMEM((D_STATE, d_inner), F32)],
        compiler_params=_cparams("parallel", "arbitrary"),
        name="ssd",
    )(zx, zx, zx, zx, conv_w, conv_b, dt_bias_rep, a_log_rep, d_exp, norm_g, e3, conv0, s0, *prev_args)
    return y, sfin, ctail


def _ssd_fused(x, g, w_in, conv_w, conv_b, dt_bias_rep, a_log_rep, d_exp, norm_g, e3, *, nb, L, T):
    d = x.shape[1]
    heads = d_exp.shape[-1] // SSD_HEADDIM
    d_inner = d_exp.shape[-1]
    bcw = 2 * SSD_GROUPS * D_STATE
    n_in = w_in.shape[1]
    assert L % (2 * T) == 0
    nk = L // (2 * T)
    const = lambda b, k: (0, 0)
    per_b = lambda b, k: (b, 0, 0)
    return pl.pallas_call(
        functools.partial(_ssd_fused_body, T=T, heads=heads, groups=SSD_GROUPS),
        out_shape=(jax.ShapeDtypeStruct((nb * L, d_inner), BF16),
                   jax.ShapeDtypeStruct((nb, d_inner, D_STATE), F32),
                   jax.ShapeDtypeStruct((nb, 8, d_inner + bcw), F32)),
        grid=(nb, nk),
        in_specs=[
            pl.BlockSpec((2 * T, d), lambda b, k: (b * nk + k, 0)),
            pl.BlockSpec((2 * T, d), lambda b, k: (b * nk + jnp.minimum(k + 1, nk - 1), 0)),
            pl.BlockSpec((1, d), const),
            pl.BlockSpec((d, n_in), const, pipeline_mode=pl.Buffered(1)),
            pl.BlockSpec((4, d_inner + bcw), const),
            pl.BlockSpec((1, d_inner + bcw), const),
            pl.BlockSpec((1, V7X_LANES), const),
            pl.BlockSpec((1, V7X_LANES), const),
            pl.BlockSpec((1, d_inner), const),
            pl.BlockSpec((1, d_inner), const),
            pl.BlockSpec((V7X_LANES, d_inner), const),
        ],
        out_specs=(pl.BlockSpec((2 * T, d_inner), lambda b, k: (b * nk + k, 0)),
                   pl.BlockSpec((1, d_inner, D_STATE), per_b),
                   pl.BlockSpec((1, 8, d_inner + bcw), per_b)),
        scratch_shapes=[pltpu.VMEM((T + 8, d_inner), F32), pltpu.VMEM((T + 8, bcw), F32),
                        pltpu.VMEM((D_STATE, d_inner), F32),
                        pltpu.VMEM((T, n_in), F32), pltpu.VMEM((T, n_in), F32)],
        compiler_params=_cparams("parallel", "arbitrary"),
        name="ssd_fused",
    )(x, x, g.reshape(1, d).astype(F32), w_in, conv_w, conv_b, dt_bias_rep, a_log_rep, d_exp, norm_g, e3)


def _ssd_w_in_pad(w_in):
    heads = w_in.shape[1] - (w_in.shape[1] // V7X_LANES) * V7X_LANES
    main = w_in[:, :w_in.shape[1] - heads]
    dt = w_in[:, w_in.shape[1] - heads:]
    return jnp.concatenate([main] + [dt] * (V7X_MXU_COLS // heads), axis=1)


def _in_proj_tile(n):
    return max(t for t in range(V7X_MXU_COLS, 1536 + 1, V7X_MXU_COLS) if n % t == 0)


def _ssd_params(conv_w, conv_b, dt_bias, a_log, d_skip, norm_g):
    heads = dt_bias.shape[0]
    rep = V7X_LANES // heads
    d_inner = heads * SSD_HEADDIM
    src = jnp.arange(V7X_LANES)[:, None]
    dst_head = jnp.arange(d_inner)[None, :] // SSD_HEADDIM
    e3 = ((src % heads == dst_head) & (src < 3 * heads)).astype(BF16)
    return (conv_w.astype(F32), conv_b.reshape(1, -1).astype(F32),
            jnp.tile(dt_bias.reshape(1, heads), (1, rep)).astype(F32),
            jnp.tile(a_log.reshape(1, heads), (1, rep)).astype(F32),
            jnp.repeat(d_skip, SSD_HEADDIM).reshape(1, d_inner).astype(F32),
            norm_g.reshape(1, d_inner).astype(F32), e3)


SLOT = V7X_LANES
VT_SLOT = V_HEAD + 16
LOG2E = math.log2(math.e)


def _mla_proj_body(x_ref, g_ref, wd_ref, qn_ref, kvn_ref, wuq_ref, wuqr_ref, wuk_ref, wuvt_ref, cos_ref, sin_ref,
                   q_ref, qt_ref, kk_ref, vt_ref, lat_ref, kr_ref, *, heads, q_lora, kv_lora, rope, scale):
    xn = _rms(x_ref[...], g_ref[...]).astype(BF16)
    down = jnp.dot(xn, wd_ref[...], preferred_element_type=F32)
    cq = down[:, :q_lora]
    ckv = down[:, q_lora:q_lora + kv_lora]
    krs = down[:, q_lora + kv_lora:q_lora + kv_lora + SLOT]
    krr = down[:, q_lora + kv_lora + SLOT:]
    cos = cos_ref[...]
    sin = sin_ref[...]
    cqn = _rms(cq, qn_ref[...]).astype(BF16)
    qp = jnp.dot(cqn, wuq_ref[...], preferred_element_type=F32)
    qr = jnp.dot(cqn, wuqr_ref[...], preferred_element_type=F32)
    lat = _rms(ckv, kvn_ref[...])
    lat_ref[...] = lat
    kr_rot = krs * cos + krr * sin
    kr_ref[...] = kr_rot[:, QK_NOPE:QK_NOPE + rope]
    latb = lat.astype(BF16)
    kn = jnp.dot(latb, wuk_ref[...], preferred_element_type=F32)
    for h in range(heads):
        sl = slice(h * SLOT, (h + 1) * SLOT)
        q_h = (qp[:, sl] * cos + qr[:, sl] * sin) * scale
        q_ref[:, sl] = q_h.astype(BF16)
        qt_ref[sl, :] = q_h.T.astype(BF16)
        kk_ref[:, sl] = (kn[:, sl] + kr_rot).astype(BF16)
    vt = lax.dot_general(wuvt_ref[...], latb, (((1,), (1,)), ((), ())),
                         preferred_element_type=F32)
    fill_rows = lax.broadcasted_iota(jnp.int32, (VT_SLOT - V_HEAD, vt.shape[1]), 0)
    ones_then_zeros = jnp.where(fill_rows == 0, 1.0, 0.0).astype(BF16)
    for h in range(heads):
        vt_ref[h * VT_SLOT:h * VT_SLOT + V_HEAD, :] = vt[h * V_HEAD:(h + 1) * V_HEAD, :].astype(BF16)
        vt_ref[h * VT_SLOT + V_HEAD:(h + 1) * VT_SLOT, :] = ones_then_zeros


def _mla_proj(x, g, w, cos_tab, sin_tab, tab_index, *, tm=512):
    m, d = x.shape
    heads = w["wuq"].shape[1] // SLOT
    q_lora = w["qn"].shape[-1]
    kv_lora = w["kvn"].shape[-1]
    rope = w["rope"]
    tm = _row_tile(m, tm)
    const = lambda i: (0, 0)
    rowb = lambda i: (i, 0)
    full = lambda a: pl.BlockSpec(a.shape, const)
    return pl.pallas_call(
        functools.partial(_mla_proj_body, heads=heads, q_lora=q_lora, kv_lora=kv_lora, rope=rope,
                          scale=w["scale"]),
        out_shape=(jax.ShapeDtypeStruct((m, heads * SLOT), BF16),
                   jax.ShapeDtypeStruct((heads * SLOT, m), BF16),
                   jax.ShapeDtypeStruct((m, heads * SLOT), BF16),
                   jax.ShapeDtypeStruct((heads * VT_SLOT, m), BF16),
                   jax.ShapeDtypeStruct((m, kv_lora), F32),
                   jax.ShapeDtypeStruct((m, rope), F32)),
        grid=(m // tm,),
        in_specs=[pl.BlockSpec((tm, d), rowb), pl.BlockSpec((1, d), const),
                  full(w["wd"]), full(w["qn"]), full(w["kvn"]), full(w["wuq"]), full(w["wuqr"]),
                  full(w["wuk"]), full(w["wuvt"]),
                  pl.BlockSpec((tm, SLOT), lambda i: (tab_index(i), 0)),
                  pl.BlockSpec((tm, SLOT), lambda i: (tab_index(i), 0))],
        out_specs=(pl.BlockSpec((tm, heads * SLOT), rowb), pl.BlockSpec((heads * SLOT, tm), lambda i: (0, i)),
                   pl.BlockSpec((tm, heads * SLOT), rowb), pl.BlockSpec((heads * VT_SLOT, tm), lambda i: (0, i)),
                   pl.BlockSpec((tm, kv_lora), rowb), pl.BlockSpec((tm, rope), rowb)),
        compiler_params=_cparams("parallel"),
        name="mla_proj",
    )(x, g.reshape(1, d).astype(F32), w["wd"], w["qn"], w["kvn"], w["wuq"], w["wuqr"], w["wuk"], w["wuvt"],
      cos_tab, sin_tab)


def _mla_weights(w_down, q_norm, kv_norm, w_uq, w_uk, w_uv):
    q_lora = q_norm.shape[0]
    kv_lora = kv_norm.shape[0]
    heads = w_uq.shape[1]
    qk = w_uq.shape[2]
    rope = qk - QK_NOPE
    half = rope // 2
    pad = SLOT - qk

    def slot_pair(wr):
        z_lo = jnp.zeros(wr.shape[:-1] + (QK_NOPE,), wr.dtype)
        z_hi = jnp.zeros(wr.shape[:-1] + (pad,), wr.dtype)
        plain = jnp.concatenate([z_lo, wr, z_hi], axis=-1)
        rot = jnp.concatenate([z_lo, -wr[..., half:], wr[..., :half], z_hi], axis=-1)
        return plain, rot

    w_kr = w_down[:, q_lora + kv_lora:]
    kr_plain, kr_rot = slot_pair(w_kr)
    wd = jnp.concatenate([w_down[:, :q_lora + kv_lora], kr_plain, kr_rot], axis=1)
    uq_nope = jnp.concatenate([w_uq[..., :QK_NOPE], jnp.zeros(w_uq.shape[:2] + (SLOT - QK_NOPE,), w_uq.dtype)], -1)
    uq_plain, uq_rot = slot_pair(w_uq[..., QK_NOPE:])
    wuq = (uq_nope + uq_plain).reshape(q_lora, heads * SLOT)
    wuqr = uq_rot.reshape(q_lora, heads * SLOT)
    wuk = jnp.concatenate([w_uk, jnp.zeros(w_uk.shape[:2] + (SLOT - QK_NOPE,), w_uk.dtype)], -1)
    wuk = wuk.reshape(kv_lora, heads * SLOT)
    wuvt = w_uv.reshape(kv_lora, heads * V_HEAD).T
    eye_r = jnp.zeros((SLOT, SLOT), w_uk.dtype).at[QK_NOPE + jnp.arange(rope), jnp.arange(rope)].set(1.0)
    uk_t = jnp.transpose(w_uk, (1, 2, 0))
    uk_t = jnp.concatenate([uk_t, jnp.zeros((heads, SLOT - QK_NOPE, kv_lora), w_uk.dtype)], axis=1)
    wabs = jnp.concatenate([uk_t, jnp.broadcast_to(eye_r, (heads, SLOT, SLOT))], axis=2)
    uv = jnp.transpose(w_uv, (1, 0, 2)).reshape(heads // 2, 2, kv_lora, V_HEAD)
    z = jnp.zeros((heads // 2, kv_lora, V_HEAD), w_uv.dtype)
    wuv_bd = jnp.concatenate([jnp.concatenate([uv[:, 0], z], axis=2), jnp.concatenate([z, uv[:, 1]], axis=2)], axis=1)
    return dict(wd=wd.astype(BF16), qn=q_norm.reshape(1, -1).astype(F32), kvn=kv_norm.reshape(1, -1).astype(F32),
                wuq=wuq.astype(BF16), wuqr=wuqr.astype(BF16), wuk=wuk.astype(BF16), wuvt=wuvt.astype(BF16),
                wabs=wabs.astype(BF16), wuv_bd=wuv_bd.astype(BF16), rope=rope,
                scale=LOG2E / math.sqrt(qk))


def _rope_tables(positions, rope):
    half = rope // 2
    inv = ROPE_THETA ** (-jnp.arange(half, dtype=F32) * (2.0 / rope))
    ang = positions.astype(F32)[:, None] * inv[None, :]
    c, s = jnp.cos(ang), jnp.sin(ang)
    p = positions.shape[0]
    ones = jnp.ones((p, QK_NOPE), F32)
    zeros = jnp.zeros((p, QK_NOPE), F32)
    zpad = jnp.zeros((p, SLOT - QK_NOPE - rope), F32)
    return (jnp.concatenate([ones, c, c, zpad], axis=1), jnp.concatenate([zeros, s, s, zpad], axis=1))


def _headmm_body(x_ref, w_ref, o_ref):
    o_ref[...] = jnp.dot(x_ref[...], w_ref[...], preferred_element_type=F32).astype(o_ref.dtype)


def _headmm(x, w, *, out_dtype=BF16):
    m = x.shape[0]
    g, kb, nb = w.shape
    assert x.shape[1] == g * kb
    return pl.pallas_call(
        _headmm_body,
        out_shape=jax.ShapeDtypeStruct((m, g * nb), out_dtype),
        grid=(g,),
        in_specs=[pl.BlockSpec((m, kb), lambda h: (0, h)), pl.BlockSpec((None, kb, nb), lambda h: (h, 0, 0))],
        out_specs=pl.BlockSpec((m, nb), lambda h: (0, h)),
        compiler_params=_cparams("parallel"),
        name="headmm",
    )(x, w)


def _flash_body(qi_ref, ki_ref, qt_ref, k_ref, vt_ref, o_ref, m_ref, acc_ref, *, heads, tq):
    p_idx = pl.program_id(1)
    qi = qi_ref[p_idx]
    ki = ki_ref[p_idx]

    @pl.when(ki == 0)
    def _():
        m_ref[...] = jnp.full(m_ref.shape, -jnp.inf, F32)
        acc_ref[...] = jnp.zeros(acc_ref.shape, F32)

    def step(masked):
        if masked:
            key_i = lax.broadcasted_iota(jnp.int32, (tq, tq), 0)
            qry_i = lax.broadcasted_iota(jnp.int32, (tq, tq), 1)
            keep = key_i <= qry_i
        def scores(h):
            k = k_ref[:, h * SLOT:(h + 1) * SLOT]
            qt = qt_ref[h * SLOT:(h + 1) * SLOT, :]
            return jnp.dot(k, qt, preferred_element_type=F32)

        ahead = [scores(0), scores(1)]
        for h in range(heads):
            s = ahead.pop(0)
            if h + 2 < heads:
                ahead.append(scores(h + 2))
            if masked:
                s = jnp.where(keep, s, -jnp.inf)
            m_prev = m_ref[h:h + 1, :]
            m_new = jnp.maximum(m_prev, jnp.max(s, axis=0, keepdims=True))
            alpha = jnp.exp2(m_prev - m_new)
            p = jnp.exp2(s - m_new)
            m_ref[h:h + 1, :] = m_new
            vt = vt_ref[h * VT_SLOT:(h + 1) * VT_SLOT, :]
            pv = jnp.dot(vt, p.astype(BF16), preferred_element_type=F32)
            acc_ref[h * VT_SLOT:(h + 1) * VT_SLOT, :] = acc_ref[h * VT_SLOT:(h + 1) * VT_SLOT, :] * alpha + pv

    @pl.when(ki < qi)
    def _():
        step(False)

    @pl.when(ki == qi)
    def _():
        step(True)
        outs = []
        for h in range(heads):
            l = acc_ref[h * VT_SLOT + V_HEAD:h * VT_SLOT + V_HEAD + 1, :]
            outs.append(acc_ref[h * VT_SLOT:h * VT_SLOT + V_HEAD, :] / l)
        o_ref[...] = jnp.concatenate(outs, axis=0).T.astype(o_ref.dtype)


def _flash(qt, kk, vt, *, nb, L, tq=512):
    heads = kk.shape[1] // SLOT
    tq = _row_tile(L, tq)
    nq = L // tq
    pairs = [(i, j) for i in range(nq) for j in range(i + 1)]
    qi_tab = jnp.asarray([p[0] for p in pairs], jnp.int32)
    ki_tab = jnp.asarray([p[1] for p in pairs], jnp.int32)
    grid_spec = pltpu.PrefetchScalarGridSpec(
        num_scalar_prefetch=2,
        grid=(nb, len(pairs)),
        in_specs=[
            pl.BlockSpec((heads * SLOT, tq), lambda b, p, qi, ki: (0, b * nq + qi[p])),
            pl.BlockSpec((tq, heads * SLOT), lambda b, p, qi, ki: (b * nq + ki[p], 0)),
            pl.BlockSpec((heads * VT_SLOT, tq), lambda b, p, qi, ki: (0, b * nq + ki[p])),
        ],
        out_specs=pl.BlockSpec((tq, heads * V_HEAD), lambda b, p, qi, ki: (b * nq + qi[p], 0)),
        scratch_shapes=[pltpu.VMEM((heads, tq), F32), pltpu.VMEM((heads * VT_SLOT, tq), F32)],
    )
    return pl.pallas_call(
        functools.partial(_flash_body, heads=heads, tq=tq),
        out_shape=jax.ShapeDtypeStruct((nb * L, heads * V_HEAD), BF16),
        grid_spec=grid_spec,
        compiler_params=_cparams("parallel", "arbitrary"),
        name="mla_flash",
    )(qi_tab, ki_tab, qt, kk, vt)


def _decode_body(pt_ref, q_ref, cnew_ref, rnew_ref, lat_hbm, ropet_hbm, o_ref,
                 cbuf, rbuf, cb, sem, *, layer, n_pages, page, chunk_pages, kv_lora, rope, heads, ls):
    b = pl.program_id(0)
    nb = pl.num_programs(0)
    slot = lax.rem(b, 2)
    rows = q_ref.shape[0]
    chunk = chunk_pages * page

    def page_copies(seq, sl):
        cps = []
        for i in range(n_pages):
            pg = pt_ref[seq, i]
            cps.append(pltpu.make_async_copy(lat_hbm.at[layer, pg], cbuf.at[sl, pl.ds(i * page, page), :], sem.at[0, sl]))
            cps.append(pltpu.make_async_copy(ropet_hbm.at[layer, pg], rbuf.at[sl, i], sem.at[1, sl]))
        return cps

    @pl.when(b == 0)
    def _():
        for cp in page_copies(0, 0):
            cp.start()

    @pl.when(b + 1 < nb)
    def _():
        for cp in page_copies(b + 1, 1 - slot):
            cp.start()

    for cp in page_copies(b, slot):
        cp.wait()

    q_lat = q_ref[:, :kv_lora]
    q_r = q_ref[:, kv_lora:kv_lora + rope]
    nt = (((1,), (1,)), ((), ()))
    n_chunks = n_pages // chunk_pages

    def scores(ck):
        c_b = cbuf[slot, ck * chunk:(ck + 1) * chunk, :].astype(BF16)
        cb[ck * chunk:(ck + 1) * chunk, :] = c_b
        r_b = jnp.concatenate([rbuf[slot, ck * chunk_pages + i].astype(BF16) for i in range(chunk_pages)],
                              axis=1)
        return (lax.dot_general(q_lat, c_b, nt, preferred_element_type=F32)
                + jnp.dot(q_r, r_b, preferred_element_type=F32))

    cn = cnew_ref[...].astype(BF16)
    rn = rnew_ref[...].astype(BF16)
    sn = (lax.dot_general(q_lat, cn, nt, preferred_element_type=F32)
          + lax.dot_general(q_r, rn, nt, preferred_element_type=F32))
    qpos = lax.broadcasted_iota(jnp.int32, (rows, ls), 0) // heads
    kpos = lax.broadcasted_iota(jnp.int32, (rows, ls), 1)
    sn = jnp.where(qpos >= kpos, sn, -jnp.inf)
    m = jnp.max(sn, axis=-1, keepdims=True)
    pn = jnp.exp2(sn - m)
    l = jnp.sum(pn, axis=-1, keepdims=True)
    acc = jnp.dot(pn.astype(BF16), cn, preferred_element_type=F32)

    s_next = scores(0)
    for ck in range(n_chunks):
        s = s_next
        if ck + 1 < n_chunks:
            s_next = scores(ck + 1)
        m_new = jnp.maximum(m, jnp.max(s, axis=-1, keepdims=True))
        alpha = jnp.exp2(m - m_new)
        p = jnp.exp2(s - m_new)
        l = alpha * l + jnp.sum(p, axis=-1, keepdims=True)
        acc = alpha * acc + jnp.dot(p.astype(BF16), cb[ck * chunk:(ck + 1) * chunk, :], preferred_element_type=F32)
        m = m_new
    o_ref[...] = (acc / l).astype(o_ref.dtype)


def _decode(q_ext, lat_new, kr_new, lat_pool, ropet_pool, page_table, *, layer, bs, ls, heads, new_row_offset,
            chunk_pages=16):
    kv_lora = lat_new.shape[1]
    rope = kr_new.shape[1]
    page = lat_pool.shape[2]
    n_pages = page_table.shape[1]
    chunk_pages = math.gcd(chunk_pages, n_pages)
    assert new_row_offset % ls == 0
    rows = ls * heads
    qw = q_ext.shape[1]
    grid_spec = pltpu.PrefetchScalarGridSpec(
        num_scalar_prefetch=1,
        grid=(bs,),
        in_specs=[pl.BlockSpec((rows, qw), lambda b, pt: (b, 0)),
                  pl.BlockSpec((ls, kv_lora), lambda b, pt: (new_row_offset // ls + b, 0)),
                  pl.BlockSpec((ls, rope), lambda b, pt: (new_row_offset // ls + b, 0)),
                  pl.BlockSpec(memory_space=pl.ANY),
                  pl.BlockSpec(memory_space=pl.ANY)],
        out_specs=pl.BlockSpec((rows, kv_lora), lambda b, pt: (b, 0)),
        scratch_shapes=[pltpu.VMEM((2, n_pages * page, kv_lora), F32),
                        pltpu.VMEM((2, n_pages, rope, page), F32),
                        pltpu.VMEM((n_pages * page, kv_lora), BF16),
                        pltpu.SemaphoreType.DMA((2, 2))],
    )
    return pl.pallas_call(
        functools.partial(_decode_body, layer=layer, n_pages=n_pages, page=page, chunk_pages=chunk_pages,
                          kv_lora=kv_lora, rope=rope, heads=heads, ls=ls),
        out_shape=jax.ShapeDtypeStruct((bs * rows, kv_lora), BF16),
        grid_spec=grid_spec,
        compiler_params=_cparams("arbitrary"),
        name="mla_decode",
    )(page_table, q_ext, lat_new, kr_new, lat_pool, ropet_pool)


MEM_ROWS = 1024
MEM_NB = 4
PROJ_TM = 512


def kernel(x_prompt, x_sample, mem_prompt, state_ssm, state_conv, cache_mla_latent, cache_mla_rope_k, cache_mem_k, cache_mem_v, page_table, norm_mix, norm_mem, norm_memkv, norm_ffn, norm_final, ssd_w_in, ssd_conv_w, ssd_conv_b, ssd_dt_bias, ssd_a_log, ssd_d, ssd_norm, ssd_w_out, mla_w_down, mla_q_norm, mla_kv_norm, mla_w_uq, mla_w_uk, mla_w_uv, mla_w_o, mem_w_q, mem_w_kv, mem_w_o, mlp_w_up, mlp_w_down):
    bp, lp, d = x_prompt.shape
    bs, ls, _ = x_sample.shape
    mp, ms = bp * lp, bs * ls
    depth = norm_mix.shape[0]
    n_mem = mem_prompt.shape[1]
    past_len = page_table.shape[1] * cache_mla_latent.shape[2]
    mla_heads = mla_w_uq.shape[2]
    rope = mla_w_uq.shape[3] - QK_NOPE
    ssd_heads = ssd_dt_bias.shape[1]
    d_inner = ssd_heads * SSD_HEADDIM

    x = jnp.concatenate([x_prompt.reshape(mp, d), x_sample.reshape(ms, d)], axis=0)
    mem_rows = mem_prompt.reshape(bp * n_mem, d)
    ropet_pool = jnp.swapaxes(cache_mla_rope_k, 2, 3)
    ssm0 = state_ssm.reshape(state_ssm.shape[0], bs, d_inner, D_STATE)

    pos = jnp.concatenate([jnp.arange(lp), jnp.tile(past_len + jnp.arange(ls), bs)])
    cos_tab, sin_tab = _rope_tables(pos, rope)
    proj_tm = _row_tile(ms, PROJ_TM)
    assert lp % proj_tm == 0
    npt, tpl = mp // proj_tm, lp // proj_tm
    tab_index = lambda i: jnp.where(i < npt, i % tpl, tpl + (i - npt))

    mem_rows_p = _row_tile(lp, MEM_ROWS)
    mem_nb = _row_tile(bs, MEM_NB)

    p_ssm, p_conv, p_lat, p_rk, p_mk, p_mv = [], [], [], [], [], []
    s_conv, s_lat, s_rk = [], [], []
    s_ssm = None
    for i in range(depth):
        j = i // 2
        if i % 2 == 0:
            w_pad = _ssd_w_in_pad(ssd_w_in[j]).astype(BF16)
            prm = _ssd_params(ssd_conv_w[j], ssd_conv_b[j], ssd_dt_bias[j], ssd_a_log[j], ssd_d[j], ssd_norm[j])
            y_p, st_p, ct_p = _ssd_fused(x, norm_mix[i], w_pad, *prm, nb=bp, L=lp, T=math.gcd(SSD_CHUNK, lp))
            zx_s = _mm(x[mp:], w_pad, g=norm_mix[i], tn=_in_proj_tile(w_pad.shape[1]), name="ssd_in")
            conv0 = jnp.pad(state_conv[j], ((0, 0), (8 - state_conv.shape[2], 0), (0, 0)))
            y_s, s_ssm, ct_s = _ssd(zx_s, *prm, row_offset=0, nb=bs, L=ls, T=math.gcd(SSD_CHUNK, ls),
                                    conv0=conv0, s0=ssm0, layer=j,
                                    prev_states=None if s_ssm is None else s_ssm.reshape(-1, bs, d_inner, D_STATE))
            x = _mm(y_p, ssd_w_out[j].astype(BF16), x_tail=y_s, res=x, name="ssd_out")
            kc = state_conv.shape[2]
            p_ssm.append(st_p.reshape(bp, ssd_heads, SSD_HEADDIM, D_STATE))
            p_conv.append(ct_p[:, 8 - kc:, :])
            s_conv.append(ct_s[:, 8 - kc:, :])
        else:
            w = _mla_weights(mla_w_down[j], mla_q_norm[j], mla_kv_norm[j], mla_w_uq[j], mla_w_uk[j], mla_w_uv[j])
            q, qt, kk, vt, lat, kr = _mla_proj(x, norm_mix[i], w, cos_tab, sin_tab, tab_index, tm=proj_tm)
            o_p = _flash(qt, kk, vt, nb=bp, L=lp)
            q_ext = _headmm(q[mp:], w["wabs"]).reshape(ms * mla_heads, -1)
            o_lat = _decode(q_ext, lat, kr, cache_mla_latent, ropet_pool, page_table, layer=j, bs=bs, ls=ls,
                            heads=mla_heads, new_row_offset=mp)
            o_s = _headmm(o_lat.reshape(ms, -1), w["wuv_bd"])
            x = _mm(o_p, mla_w_o[j].astype(BF16), x_tail=o_s, res=x, name="mla_out")
            p_lat.append(lat[:mp].reshape(bp, lp, -1))
            s_lat.append(lat[mp:].reshape(bs, ls, -1))
            p_rk.append(kr[:mp].reshape(bp, lp, -1))
            s_rk.append(kr[mp:].reshape(bs, ls, -1))
        kv = _mm(mem_rows, mem_w_kv[i].astype(BF16), g=norm_memkv[i], name="mem_kv")
        kp = kv[:, :d].reshape(bp, n_mem, d)
        vp = kv[:, d:].reshape(bp, n_mem, d)
        p_mk.append(kp.reshape(bp, n_mem, MEM_HEADS, d // MEM_HEADS))
        p_mv.append(vp.reshape(bp, n_mem, MEM_HEADS, d // MEM_HEADS))
        wq, wo = mem_w_q[i].astype(BF16), mem_w_o[i].astype(BF16)
        x_p = _mem_fused(x, norm_mem[i], wq, kp.astype(BF16), vp.astype(BF16), wo, n_seq=bp, L=lp, rows=mem_rows_p)
        x_s = x[mp:]
        qm_s = _mm(x_s, wq, g=norm_mem[i], out_dtype=BF16, name="mem_q")
        o_s = _memattn_dec(qm_s, cache_mem_k, cache_mem_v, layer=i, row_offset=0, n_seq=bs, rows=ls, nb=mem_nb)
        x_s = _mm(o_s, wo, res=x_s, name="mem_out")
        x = _mlp(x_p, norm_ffn[i], mlp_w_up[i].astype(BF16), mlp_w_down[i].astype(BF16), x_tail=x_s)
    y_p = _norm(x, norm_final, row_offset=0, rows=mp)
    y_s = _norm(x, norm_final, row_offset=mp, rows=ms)
    return (y_p.reshape(bp, lp, d), y_s.reshape(bs, ls, d),
            jnp.stack(p_ssm), jnp.stack(p_conv), jnp.stack(p_lat), jnp.stack(p_rk), jnp.stack(p_mk), jnp.stack(p_mv),
            s_ssm.reshape(state_ssm.shape), jnp.stack(s_conv), jnp.stack(s_lat), jnp.stack(s_rk))
```

```python
import functools
import math

import jax
import jax.numpy as jnp
from jax import lax
from jax.experimental import pallas as pl
from jax.experimental.pallas import tpu as pltpu

F32 = jnp.float32
BF16 = jnp.bfloat16

EPS = 1e-6
ROPE_THETA = 10000.0

V7X_LANES = 128
V7X_SUBLANES = 8
V7X_MXU_COLS = 256
V7X_VMEM_LIMIT_BYTES = 56 * 1024 * 1024

SSD_HEADDIM = 64
SSD_GROUPS = 8
D_STATE = 128
SSD_CHUNK = 128
QK_NOPE = 64
V_HEAD = 64
MEM_HEADS = 4


def _cparams(*sem):
    return pltpu.CompilerParams(dimension_semantics=sem, vmem_limit_bytes=V7X_VMEM_LIMIT_BYTES)


def _rms(x, g):
    return x * lax.rsqrt(jnp.mean(x * x, axis=-1, keepdims=True) + EPS) * g


def _row_tile(m, pref):
    t = min(pref, m)
    assert m % t == 0, (m, t)
    return t


def _mm_body(*refs, norm, res, act, head_tiles):
    it = iter(refs)
    x_ref = next(it)
    t_ref = next(it) if head_tiles is not None else None
    g_ref = next(it) if norm else None
    w_ref = next(it)
    r_ref = next(it) if res else None
    o_ref = next(it)
    xn_ref = next(it)

    def stage(src_ref):
        x = src_ref[...].astype(F32)
        if norm:
            x = _rms(x, g_ref[...])
        xn_ref[...] = x.astype(BF16)

    first = pl.program_id(1) == 0
    if head_tiles is None:
        pl.when(first)(lambda: stage(x_ref))
    else:
        in_head = pl.program_id(0) < head_tiles
        pl.when(first & in_head)(lambda: stage(x_ref))
        pl.when(first & jnp.logical_not(in_head))(lambda: stage(t_ref))

    acc = jnp.dot(xn_ref[...], w_ref[...], preferred_element_type=F32)
    if act == "relu2":
        acc = jnp.square(jnp.maximum(acc, 0.0))
    if res:
        acc = r_ref[...] + acc
    o_ref[...] = acc.astype(o_ref.dtype)


def _mm(x, w, *, x_tail=None, g=None, res=None, act=None, out_dtype=F32, tm=1024, tn=1024, name="mm"):
    m, k = x.shape
    k2, n = w.shape
    assert k == k2
    head_tiles = None
    if x_tail is None:
        tm = _row_tile(m, tm)
        in_specs = [pl.BlockSpec((tm, k), lambda i, j: (i, 0))]
        args = [x]
    else:
        m_tail = x_tail.shape[0]
        tm = math.gcd(math.gcd(m, m_tail), tm)
        head_tiles = m // tm
        m = m + m_tail
        in_specs = [pl.BlockSpec((tm, k), lambda i, j: (jnp.minimum(i, head_tiles - 1), 0)),
                    pl.BlockSpec((tm, k), lambda i, j: (jnp.maximum(i - head_tiles, 0), 0))]
        args = [x, x_tail]
    tn = _row_tile(n, tn)
    if g is not None:
        in_specs.append(pl.BlockSpec((1, k), lambda i, j: (0, 0)))
        args.append(g.reshape(1, k).astype(F32))
    in_specs.append(pl.BlockSpec((k, tn), lambda i, j: (0, j)))
    args.append(w)
    if res is not None:
        in_specs.append(pl.BlockSpec((tm, tn), lambda i, j: (i, j)))
        args.append(res)
    return pl.pallas_call(
        functools.partial(_mm_body, norm=g is not None, res=res is not None, act=act, head_tiles=head_tiles),
        out_shape=jax.ShapeDtypeStruct((m, n), out_dtype),
        grid=(m // tm, n // tn),
        in_specs=in_specs,
        out_specs=pl.BlockSpec((tm, tn), lambda i, j: (i, j)),
        scratch_shapes=[pltpu.VMEM((tm, k), BF16)],
        compiler_params=_cparams("parallel", "arbitrary"),
        name=name,
    )(*args)


def _mlp_body(*refs, head_tiles, final_norm):
    x_ref = refs[0]
    t_ref = refs[1] if head_tiles is not None else None
    refs = refs[1 + int(head_tiles is not None):]
    g_ref, wu_ref, wd_ref = refs[:3]
    fg_ref = refs[3] if final_norm else None
    refs = refs[3 + int(final_norm):]
    o_ref = refs[0]
    ot_ref = refs[1] if final_norm else None
    xn_ref, acc_ref = refs[1 + int(final_norm):]
    k = pl.program_id(1)

    def stage(src_ref):
        x = src_ref[...]
        xn_ref[...] = _rms(x, g_ref[...]).astype(BF16)
        acc_ref[...] = x

    if head_tiles is None:
        pl.when(k == 0)(lambda: stage(x_ref))
    else:
        in_head = pl.program_id(0) < head_tiles
        pl.when((k == 0) & in_head)(lambda: stage(x_ref))
        pl.when((k == 0) & jnp.logical_not(in_head))(lambda: stage(t_ref))

    h = jnp.dot(xn_ref[...], wu_ref[...], preferred_element_type=F32)
    h = jnp.square(jnp.maximum(h, 0.0)).astype(BF16)
    acc_ref[...] += jnp.dot(h, wd_ref[...], preferred_element_type=F32)

    last = k == pl.num_programs(1) - 1
    if not final_norm:
        @pl.when(last)
        def _():
            o_ref[...] = acc_ref[...]
    else:
        in_head = pl.program_id(0) < head_tiles

        @pl.when(last & in_head)
        def _():
            o_ref[...] = _rms(acc_ref[...], fg_ref[...])

        @pl.when(last & jnp.logical_not(in_head))
        def _():
            ot_ref[...] = _rms(acc_ref[...], fg_ref[...])


def _mlp(x, g, w_up, w_down, *, x_tail=None, final_g=None, tm=1024, tf=1024):
    m, d = x.shape
    ff = w_up.shape[1]
    head_tiles = None
    assert final_g is None or x_tail is not None
    if x_tail is None:
        tm = _row_tile(m, tm)
        row_specs, row_args = [pl.BlockSpec((tm, d), lambda i, k: (i, 0))], [x]
    else:
        m_tail = x_tail.shape[0]
        tm = math.gcd(math.gcd(m, m_tail), tm)
        head_tiles = m // tm
        m = m + m_tail
        row_specs = [pl.BlockSpec((tm, d), lambda i, k: (jnp.minimum(i, head_tiles - 1), 0)),
                     pl.BlockSpec((tm, d), lambda i, k: (jnp.maximum(i - head_tiles, 0), 0))]
        row_args = [x, x_tail]
    tf = _row_tile(ff, tf)
    if final_g is None:
        out_shape = jax.ShapeDtypeStruct((m, d), F32)
        out_specs = pl.BlockSpec((tm, d), lambda i, k: (i, 0))
        fg_specs, fg_args = [], []
    else:
        out_shape = (jax.ShapeDtypeStruct(x.shape, F32), jax.ShapeDtypeStruct(x_tail.shape, F32))
        out_specs = tuple(row_specs)
        fg_specs, fg_args = [pl.BlockSpec((1, d), lambda i, k: (0, 0))], [final_g.reshape(1, d).astype(F32)]
    return pl.pallas_call(
        functools.partial(_mlp_body, head_tiles=head_tiles, final_norm=final_g is not None),
        out_shape=out_shape,
        grid=(m // tm, ff // tf),
        in_specs=row_specs + [
            pl.BlockSpec((1, d), lambda i, k: (0, 0)),
            pl.BlockSpec((d, tf), lambda i, k: (0, k)),
            pl.BlockSpec((tf, d), lambda i, k: (k, 0)),
        ] + fg_specs,
        out_specs=out_specs,
        scratch_shapes=[pltpu.VMEM((tm, d), BF16), pltpu.VMEM((tm, d), F32)],
        compiler_params=_cparams("parallel" if final_g is None else "arbitrary", "arbitrary"),
        name="mlp",
    )(*row_args, g.reshape(1, d).astype(F32), w_up, w_down, *fg_args)


def _mem_fused_body(x_ref, g_ref, wq_ref, k_ref, v_ref, wo_ref, o_ref, *, heads, scale):
    x = x_ref[...]
    q = jnp.dot(_rms(x, g_ref[...]).astype(BF16), wq_ref[...], preferred_element_type=F32)
    dh = q.shape[-1] // heads
    outs = []
    for h in range(heads):
        q_h = q[:, h * dh:(h + 1) * dh].astype(BF16)
        k_h = k_ref[:, h * dh:(h + 1) * dh]
        v_h = v_ref[:, h * dh:(h + 1) * dh]
        s = lax.dot_general(q_h, k_h, (((1,), (1,)), ((), ())), preferred_element_type=F32) * scale
        s = s - jnp.max(s, axis=-1, keepdims=True)
        p = jnp.exp(s)
        p = p / jnp.sum(p, axis=-1, keepdims=True)
        outs.append(jnp.dot(p.astype(BF16), v_h, preferred_element_type=F32).astype(BF16))
    o = jnp.concatenate(outs, axis=1)
    o_ref[...] = x + jnp.dot(o, wo_ref[...], preferred_element_type=F32)


def _mem_fused(x, g, wq, k, v, wo, *, n_seq, L, rows):
    d = x.shape[-1]
    n_mem = k.shape[-2]
    per_seq = L // rows
    const = lambda s: (0, 0)
    weight = lambda: pl.BlockSpec((d, d), const, pipeline_mode=pl.Buffered(1))
    return pl.pallas_call(
        functools.partial(_mem_fused_body, heads=MEM_HEADS, scale=1.0 / math.sqrt(d // MEM_HEADS)),
        out_shape=jax.ShapeDtypeStruct((n_seq * L, d), F32),
        grid=(n_seq * per_seq,),
        in_specs=[pl.BlockSpec((rows, d), lambda s: (s, 0)),
                  pl.BlockSpec((1, d), const),
                  weight(),
                  pl.BlockSpec((None, n_mem, d), lambda s: (s // per_seq, 0, 0)),
                  pl.BlockSpec((None, n_mem, d), lambda s: (s // per_seq, 0, 0)),
                  weight()],
        out_specs=pl.BlockSpec((rows, d), lambda s: (s, 0)),
        compiler_params=_cparams("parallel"),
        name="mem_fused",
    )(x, g.reshape(1, d).astype(F32), wq, k, v, wo)


def _memattn_dec_body(q_ref, k_ref, v_ref, o_ref, *, nb, rows, heads, scale):
    dh = q_ref.shape[-1] // heads
    n_mem = k_ref.shape[1]
    qrow_head = lax.broadcasted_iota(jnp.int32, (heads * rows, n_mem * heads), 0) // rows
    key_head = lax.broadcasted_iota(jnp.int32, (heads * rows, n_mem * heads), 1) % heads
    own = qrow_head == key_head
    outs = []
    q_all = q_ref[...].astype(F32)
    for i in range(nb):
        qi = q_all[i * rows:(i + 1) * rows, :]
        qs = jnp.concatenate([qi[:, h * dh:(h + 1) * dh] for h in range(heads)], axis=0).astype(BF16)
        k2 = k_ref[i].reshape(n_mem * heads, dh).astype(BF16)
        v2 = v_ref[i].reshape(n_mem * heads, dh).astype(BF16)
        s = lax.dot_general(qs, k2, (((1,), (1,)), ((), ())), preferred_element_type=F32) * scale
        s = jnp.where(own, s, -jnp.inf)
        s = s - jnp.max(s, axis=-1, keepdims=True)
        p = jnp.exp(s)
        p = p / jnp.sum(p, axis=-1, keepdims=True)
        o = jnp.dot(p.astype(BF16), v2, preferred_element_type=F32)
        outs.append(jnp.concatenate([o[h * rows:(h + 1) * rows, :] for h in range(heads)], axis=1))
    o_ref[...] = jnp.concatenate(outs, axis=0).astype(o_ref.dtype)


def _memattn_dec(q, k, v, *, layer, row_offset, n_seq, rows, nb):
    d = q.shape[-1]
    _, _, n_mem, heads, dh = k.shape
    blk = nb * rows
    assert row_offset % blk == 0 and n_seq % nb == 0
    kv_spec = pl.BlockSpec((None, nb, n_mem, heads, dh), lambda s: (layer, s, 0, 0, 0))
    return pl.pallas_call(
        functools.partial(_memattn_dec_body, nb=nb, rows=rows, heads=heads, scale=1.0 / math.sqrt(dh)),
        out_shape=jax.ShapeDtypeStruct((n_seq * rows, d), BF16),
        grid=(n_seq // nb,),
        in_specs=[pl.BlockSpec((blk, d), lambda s: (row_offset // blk + s, 0)), kv_spec, kv_spec],
        out_specs=pl.BlockSpec((blk, d), lambda s: (s, 0)),
        compiler_params=_cparams("parallel"),
        name="memattn_dec",
    )(q, k, v)


def _split3_lanes(v, heads):
    lane = lax.broadcasted_iota(jnp.int32, v.shape, 1)
    hi = v.astype(BF16).astype(F32)
    r1 = v - hi
    mid = r1.astype(BF16).astype(F32)
    lo = r1 - mid
    out = jnp.where(lane < heads, hi, jnp.where(lane < 2 * heads, mid, jnp.where(lane < 3 * heads, lo, 0.0)))
    return out.astype(BF16)


def _ssd_init(conv0_ref, s0_ref, extx_ref, extbc_ref, st_ref):
    d_inner = extx_ref.shape[-1]
    if conv0_ref is not None:
        extx_ref[0:8, :] = conv0_ref[:, 0:d_inner]
        extbc_ref[0:8, :] = conv0_ref[:, d_inner:]
        st_ref[...] = s0_ref[...].T
    else:
        extx_ref[0:8, :] = jnp.zeros((8, d_inner), F32)
        extbc_ref[0:8, :] = jnp.zeros((8, extbc_ref.shape[-1]), F32)
        st_ref[...] = jnp.zeros(st_ref.shape, F32)


def _ssd_finish(sfin_ref, ctail_ref, extx_ref, extbc_ref, st_ref):
    d_inner = extx_ref.shape[-1]
    sfin_ref[...] = st_ref[...].T
    ctail_ref[:, 0:d_inner] = extx_ref[0:8, :]
    ctail_ref[:, d_inner:] = extbc_ref[0:8, :]


def _scan_chunk(z, x_raw, bc_raw, dt_raw, cw_ref, cb_ref, dtb_ref, alog_ref, dsk_ref, ng_ref, e3_ref,
                y_ref, y_row0, extx_ref, extbc_ref, st_ref, *, T, heads, groups):
    d_inner = extx_ref.shape[-1]
    hpg = heads // groups
    gw = d_inner // groups
    hd = d_inner // heads
    n = D_STATE
    gate = z * jax.nn.sigmoid(z)

    extx_ref[8:8 + T, :] = x_raw
    extbc_ref[8:8 + T, :] = bc_raw

    def conv(ext_ref, lo, width):
        cur = ext_ref[8:8 + T, :]
        tail = ext_ref[0:8, :]
        r8 = lax.broadcasted_iota(jnp.int32, (8, width), 0)
        kw = cw_ref.shape[0]
        acc = cb_ref[:, lo:lo + width] + cur * cw_ref[kw - 1:kw, lo:lo + width]
        for s in range(1, kw):
            rolled = pltpu.roll(cur, s, axis=0)
            head = jnp.where(r8 < s, pltpu.roll(tail, s, axis=0), rolled[0:8])
            shifted = head if T == 8 else jnp.concatenate([head, rolled[8:]], axis=0)
            acc = acc + shifted * cw_ref[kw - 1 - s:kw - s, lo:lo + width]
        return acc * jax.nn.sigmoid(acc)

    xc = conv(extx_ref, 0, d_inner)
    bcc = conv(extbc_ref, d_inner, extbc_ref.shape[-1])
    extx_ref[0:8, :] = extx_ref[T:T + 8, :]
    extbc_ref[0:8, :] = extbc_ref[T:T + 8, :]

    dtr = dt_raw + dtb_ref[...]
    dtv = jnp.maximum(dtr, 0.0) + jnp.log1p(jnp.exp(-jnp.abs(dtr)))
    a = dtv * (-jnp.exp(alog_ref[...]))
    row = lax.broadcasted_iota(jnp.int32, a.shape, 0)
    acs = a
    sh = 1
    while sh < T:
        acs = acs + jnp.where(row >= sh, pltpu.roll(acs, sh, axis=0), 0.0)
        sh *= 2
    a_last = acs[T - 1:T, :]
    e3 = e3_ref[...]

    def expand(v):
        return jnp.dot(_split3_lanes(v, heads), e3, preferred_element_type=F32)

    dt_e = expand(dtv)
    dend_e = expand(jnp.exp(a_last - acs))
    eacs_e = expand(jnp.exp(acs))
    xdt = xc * dt_e
    xdt_b = xdt.astype(BF16)
    xw_b = (xdt * dend_e).astype(BF16)
    cdec_e = eacs_e[T - 1:T, :]

    if T < V7X_LANES:
        acs_p = jnp.concatenate([acs, jnp.zeros((V7X_LANES - T, V7X_LANES), F32)], axis=0)
    else:
        acs_p = acs
    acs_t = acs_p.T
    ti = lax.broadcasted_iota(jnp.int32, (T, T), 0)
    si = lax.broadcasted_iota(jnp.int32, (T, T), 1)
    causal = ti >= si
    lane_g = lax.broadcasted_iota(jnp.int32, (T, gw), 1)

    for g in range(groups):
        bg = bcc[:, g * n:(g + 1) * n]
        cg = bcc[:, groups * n + g * n: groups * n + (g + 1) * n].astype(BF16)
        bg_b = bg.astype(BF16)
        gmat = lax.dot_general(cg, bg_b, (((1,), (1,)), ((), ())), preferred_element_type=F32)
        st_g = st_ref[:, g * gw:(g + 1) * gw]
        y_g = jnp.dot(cg, st_g.astype(BF16), preferred_element_type=F32) * eacs_e[:, g * gw:(g + 1) * gw]
        xg = xdt_b[:, g * gw:(g + 1) * gw]
        for j in range(hpg):
            h = g * hpg + j
            seg = acs[:, h:h + 1] - acs_t[h:h + 1, 0:T]
            decay = jnp.exp(jnp.where(causal, seg, -jnp.inf))
            m_h = (gmat * decay).astype(BF16)
            x_h = jnp.where((lane_g >= j * hd) & (lane_g < (j + 1) * hd), xg, jnp.zeros_like(xg))
            y_g = y_g + jnp.dot(m_h, x_h, preferred_element_type=F32)
        cs_t = jnp.dot(bg.T.astype(BF16), xw_b[:, g * gw:(g + 1) * gw], preferred_element_type=F32)
        st_ref[:, g * gw:(g + 1) * gw] = st_g * cdec_e[:, g * gw:(g + 1) * gw] + cs_t
        y_g = y_g + xc[:, g * gw:(g + 1) * gw] * dsk_ref[:, g * gw:(g + 1) * gw]
        y_g = y_g * gate[:, g * gw:(g + 1) * gw]
        y_g = _rms(y_g, ng_ref[:, g * gw:(g + 1) * gw])
        y_ref[y_row0:y_row0 + T, g * gw:(g + 1) * gw] = y_g.astype(y_ref.dtype)


def _ssd_body(*refs, T, heads, groups, nseq, has_prev):
    (z_ref, x_ref, bc_ref, dt_ref, cw_ref, cb_ref, dtb_ref, alog_ref, dsk_ref, ng_ref, e3_ref,
     conv0_ref, s0_ref) = refs[:13]
    prev_ref = refs[13] if has_prev else None
    y_ref, sfin_ref, ctail_ref, extx_ref, extbc_ref, st_ref = refs[13 + int(has_prev):]
    c = pl.program_id(1)

    @pl.when(c == 0)
    def _():
        for i in range(nseq):
            _ssd_init(conv0_ref.at[i], s0_ref.at[i], extx_ref.at[i], extbc_ref.at[i], st_ref.at[i])

    for i in range(nseq):
        rows = slice(i * T, (i + 1) * T)
        _scan_chunk(z_ref[rows, :], x_ref[rows, :], bc_ref[rows, :], dt_ref[rows, :],
                    cw_ref, cb_ref, dtb_ref, alog_ref, dsk_ref, ng_ref, e3_ref,
                    y_ref, i * T, extx_ref.at[i], extbc_ref.at[i], st_ref.at[i], T=T, heads=heads, groups=groups)

    @pl.when(c == pl.num_programs(1) - 1)
    def _():
        for i in range(nseq):
            if has_prev:
                n_prev = prev_ref.shape[0]
                sfin_ref[0:n_prev, i] = prev_ref[:, i]
                sfin_i = sfin_ref.at[n_prev, i]
            else:
                sfin_i = sfin_ref.at[i]
            _ssd_finish(sfin_i, ctail_ref.at[i], extx_ref.at[i], extbc_ref.at[i], st_ref.at[i])


def _ssd_fused_body(xcur_ref, xnext_ref, g_ref, win_ref, cw_ref, cb_ref, dtb_ref, alog_ref, dsk_ref, ng_ref, e3_ref,
                    y_ref, sfin_ref, ctail_ref,
                    extx_ref, extbc_ref, st_ref, zxa_ref, zxb_ref, *, T, heads, groups):
    k = pl.program_id(1)
    d_inner = extx_ref.shape[-1]
    bcw = extbc_ref.shape[-1]

    def in_proj(rows):
        xn = _rms(rows, g_ref[...]).astype(BF16)
        return jnp.dot(xn, win_ref[...], preferred_element_type=F32)

    def scan(zx_ref, y_row0):
        _scan_chunk(zx_ref[:, 0:d_inner], zx_ref[:, d_inner:2 * d_inner],
                    zx_ref[:, 2 * d_inner:2 * d_inner + bcw],
                    zx_ref[:, 2 * d_inner + bcw:2 * d_inner + bcw + V7X_LANES],
                    cw_ref, cb_ref, dtb_ref, alog_ref, dsk_ref, ng_ref, e3_ref,
                    y_ref, y_row0, extx_ref, extbc_ref, st_ref, T=T, heads=heads, groups=groups)

    @pl.when(k == 0)
    def _():
        _ssd_init(None, None, extx_ref, extbc_ref, st_ref)
        zxa_ref[...] = in_proj(xcur_ref[0:T, :])

    zxb_ref[...] = in_proj(xcur_ref[T:2 * T, :])
    scan(zxa_ref, 0)
    zxa_ref[...] = in_proj(xnext_ref[0:T, :])
    scan(zxb_ref, T)
    pl.when(k == pl.num_programs(1) - 1)(
        functools.partial(_ssd_finish, sfin_ref.at[0], ctail_ref.at[0], extx_ref, extbc_ref, st_ref))


def _ssd(zx, conv_w, conv_b, dt_bias_rep, a_log_rep, d_exp, norm_g, e3, conv0, s0, *, row_offset, nb, L, T,
         layer, nseq=1, prev_states=None):
    heads = d_exp.shape[-1] // SSD_HEADDIM
    d_inner = d_exp.shape[-1]
    bcw = 2 * SSD_GROUPS * D_STATE
    assert bcw == d_inner, "column blocks are indexed in units of d_inner"
    nc = L // T
    rows = nseq * T
    assert nb % nseq == 0 and row_offset % rows == 0 and (nseq == 1 or nc == 1)
    rb0 = row_offset // rows
    rowmap = lambda col: (lambda b, c: (rb0 + b * nc + c, col))
    const = lambda b, c: (0, 0)
    per_b = lambda b, c: (b, 0, 0)
    dt_col = (2 * d_inner + bcw) // V7X_LANES
    if prev_states is None:
        prev_args, prev_specs = [], []
        sfin_shape = (nb, d_inner, D_STATE)
        sfin_spec = pl.BlockSpec((nseq, d_inner, D_STATE), per_b)
    else:
        n_prev = prev_states.shape[0]
        prev_args = [prev_states]
        prev_specs = [pl.BlockSpec((n_prev, nseq, d_inner, D_STATE), lambda b, c: (0, b, 0, 0))]
        sfin_shape = (n_prev + 1, nb, d_inner, D_STATE)
        sfin_spec = pl.BlockSpec((n_prev + 1, nseq, d_inner, D_STATE), lambda b, c: (0, b, 0, 0))
    y, sfin, ctail = pl.pallas_call(
        functools.partial(_ssd_body, T=T, heads=heads, groups=SSD_GROUPS, nseq=nseq,
                          has_prev=prev_states is not None),
        out_shape=(jax.ShapeDtypeStruct((nb * L, d_inner), BF16),
                   jax.ShapeDtypeStruct(sfin_shape, F32),
                   jax.ShapeDtypeStruct((nb, 8, d_inner + bcw), F32)),
        grid=(nb // nseq, nc),
        in_specs=[
            pl.BlockSpec((rows, d_inner), rowmap(0)),
            pl.BlockSpec((rows, d_inner), rowmap(1)),
            pl.BlockSpec((rows, bcw), rowmap(2)),
            pl.BlockSpec((rows, V7X_LANES), rowmap(dt_col)),
            pl.BlockSpec((4, d_inner + bcw), const),
            pl.BlockSpec((1, d_inner + bcw), const),
            pl.BlockSpec((1, V7X_LANES), const),
            pl.BlockSpec((1, V7X_LANES), const),
            pl.BlockSpec((1, d_inner), const),
            pl.BlockSpec((1, d_inner), const),
            pl.BlockSpec((V7X_LANES, d_inner), const),
            pl.BlockSpec((nseq, 8, d_inner + bcw), per_b),
            pl.BlockSpec((None, nseq, d_inner, D_STATE), lambda b, c: (layer, b, 0, 0)),
        ] + prev_specs,
        out_specs=(pl.BlockSpec((rows, d_inner), lambda b, c: (b * nc + c, 0)),
                   sfin_spec,
                   pl.BlockSpec((nseq, 8, d_inner + bcw), per_b)),
        scratch_shapes=[pltpu.VMEM((nseq, T + 8, d_inner), F32), pltpu.VMEM((nseq, T + 8, bcw), F32),
                        pltpu.VMEM((nseq, D_STATE, d_inner), F32)],
        compiler_params=_cparams("parallel", "arbitrary"),
        name="ssd",
    )(zx, zx, zx, zx, conv_w, conv_b, dt_bias_rep, a_log_rep, d_exp, norm_g, e3, conv0, s0, *prev_args)
    return y, sfin, ctail


def _ssd_fused(x, g, w_in, conv_w, conv_b, dt_bias_rep, a_log_rep, d_exp, norm_g, e3, *, nb, L, T):
    d = x.shape[1]
    heads = d_exp.shape[-1] // SSD_HEADDIM
    d_inner = d_exp.shape[-1]
    bcw = 2 * SSD_GROUPS * D_STATE
    n_in = w_in.shape[1]
    assert L % (2 * T) == 0
    nk = L // (2 * T)
    const = lambda b, k: (0, 0)
    per_b = lambda b, k: (b, 0, 0)
    return pl.pallas_call(
        functools.partial(_ssd_fused_body, T=T, heads=heads, groups=SSD_GROUPS),
        out_shape=(jax.ShapeDtypeStruct((nb * L, d_inner), BF16),
                   jax.ShapeDtypeStruct((nb, d_inner, D_STATE), F32),
                   jax.ShapeDtypeStruct((nb, 8, d_inner + bcw), F32)),
        grid=(nb, nk),
        in_specs=[
            pl.BlockSpec((2 * T, d), lambda b, k: (b * nk + k, 0)),
            pl.BlockSpec((2 * T, d), lambda b, k: (b * nk + jnp.minimum(k + 1, nk - 1), 0)),
            pl.BlockSpec((1, d), const),
            pl.BlockSpec((d, n_in), const, pipeline_mode=pl.Buffered(1)),
            pl.BlockSpec((4, d_inner + bcw), const),
            pl.BlockSpec((1, d_inner + bcw), const),
            pl.BlockSpec((1, V7X_LANES), const),
            pl.BlockSpec((1, V7X_LANES), const),
            pl.BlockSpec((1, d_inner), const),
            pl.BlockSpec((1, d_inner), const),
            pl.BlockSpec((V7X_LANES, d_inner), const),
        ],
        out_specs=(pl.BlockSpec((2 * T, d_inner), lambda b, k: (b * nk + k, 0)),
                   pl.BlockSpec((1, d_inner, D_STATE), per_b),
                   pl.BlockSpec((1, 8, d_inner + bcw), per_b)),
        scratch_shapes=[pltpu.VMEM((T + 8, d_inner), F32), pltpu.VMEM((T + 8, bcw), F32),
                        pltpu.VMEM((D_STATE, d_inner), F32),
                        pltpu.VMEM((T, n_in), F32), pltpu.VMEM((T, n_in), F32)],
        compiler_params=_cparams("parallel", "arbitrary"),
        name="ssd_fused",
    )(x, x, g.reshape(1, d).astype(F32), w_in, conv_w, conv_b, dt_bias_rep, a_log_rep, d_exp, norm_g, e3)


def _ssd_w_in_pad(w_in):
    heads = w_in.shape[1] - (w_in.shape[1] // V7X_LANES) * V7X_LANES
    main = w_in[:, :w_in.shape[1] - heads]
    dt = w_in[:, w_in.shape[1] - heads:]
    return jnp.concatenate([main] + [dt] * (V7X_MXU_COLS // heads), axis=1)


def _in_proj_tile(n):
    return max(t for t in range(V7X_MXU_COLS, 1536 + 1, V7X_MXU_COLS) if n % t == 0)


def _ssd_params(conv_w, conv_b, dt_bias, a_log, d_skip, norm_g):
    heads = dt_bias.shape[0]
    rep = V7X_LANES // heads
    d_inner = heads * SSD_HEADDIM
    src = jnp.arange(V7X_LANES)[:, None]
    dst_head = jnp.arange(d_inner)[None, :] // SSD_HEADDIM
    e3 = ((src % heads == dst_head) & (src < 3 * heads)).astype(BF16)
    return (conv_w.astype(F32), conv_b.reshape(1, -1).astype(F32),
            jnp.tile(dt_bias.reshape(1, heads), (1, rep)).astype(F32),
            jnp.tile(a_log.reshape(1, heads), (1, rep)).astype(F32),
            jnp.repeat(d_skip, SSD_HEADDIM).reshape(1, d_inner).astype(F32),
            norm_g.reshape(1, d_inner).astype(F32), e3)


SLOT = V7X_LANES
VT_SLOT = V_HEAD + 16
LOG2E = math.log2(math.e)


def _mla_proj_body(x_ref, g_ref, wd_ref, qn_ref, kvn_ref, wuq_ref, wuqr_ref, wuk_ref, wuvt_ref, cos_ref, sin_ref,
                   q_ref, qt_ref, kk_ref, vt_ref, lat_ref, kr_ref, *, heads, q_lora, kv_lora, rope, scale):
    xn = _rms(x_ref[...], g_ref[...]).astype(BF16)
    down = jnp.dot(xn, wd_ref[...], preferred_element_type=F32)
    cq = down[:, :q_lora]
    ckv = down[:, q_lora:q_lora + kv_lora]
    krs = down[:, q_lora + kv_lora:q_lora + kv_lora + SLOT]
    krr = down[:, q_lora + kv_lora + SLOT:]
    cos = cos_ref[...]
    sin = sin_ref[...]
    cqn = _rms(cq, qn_ref[...]).astype(BF16)
    qp = jnp.dot(cqn, wuq_ref[...], preferred_element_type=F32)
    qr = jnp.dot(cqn, wuqr_ref[...], preferred_element_type=F32)
    lat = _rms(ckv, kvn_ref[...])
    lat_ref[...] = lat
    kr_rot = krs * cos + krr * sin
    kr_ref[...] = kr_rot[:, QK_NOPE:QK_NOPE + rope]
    latb = lat.astype(BF16)
    kn = jnp.dot(latb, wuk_ref[...], preferred_element_type=F32)
    for h in range(heads):
        sl = slice(h * SLOT, (h + 1) * SLOT)
        q_h = (qp[:, sl] * cos + qr[:, sl] * sin) * scale
        q_ref[:, sl] = q_h.astype(BF16)
        qt_ref[sl, :] = q_h.T.astype(BF16)
        kk_ref[:, sl] = (kn[:, sl] + kr_rot).astype(BF16)
    vt = lax.dot_general(wuvt_ref[...], latb, (((1,), (1,)), ((), ())),
                         preferred_element_type=F32)
    fill_rows = lax.broadcasted_iota(jnp.int32, (VT_SLOT - V_HEAD, vt.shape[1]), 0)
    ones_then_zeros = jnp.where(fill_rows == 0, 1.0, 0.0).astype(BF16)
    for h in range(heads):
        vt_ref[h * VT_SLOT:h * VT_SLOT + V_HEAD, :] = vt[h * V_HEAD:(h + 1) * V_HEAD, :].astype(BF16)
        vt_ref[h * VT_SLOT + V_HEAD:(h + 1) * VT_SLOT, :] = ones_then_zeros


def _mla_proj(x, g, w, cos_tab, sin_tab, tab_index, *, tm=512):
    m, d = x.shape
    heads = w["wuq"].shape[1] // SLOT
    q_lora = w["qn"].shape[-1]
    kv_lora = w["kvn"].shape[-1]
    rope = w["rope"]
    tm = _row_tile(m, tm)
    const = lambda i: (0, 0)
    rowb = lambda i: (i, 0)
    full = lambda a: pl.BlockSpec(a.shape, const)
    return pl.pallas_call(
        functools.partial(_mla_proj_body, heads=heads, q_lora=q_lora, kv_lora=kv_lora, rope=rope,
                          scale=w["scale"]),
        out_shape=(jax.ShapeDtypeStruct((m, heads * SLOT), BF16),
                   jax.ShapeDtypeStruct((heads * SLOT, m), BF16),
                   jax.ShapeDtypeStruct((m, heads * SLOT), BF16),
                   jax.ShapeDtypeStruct((heads * VT_SLOT, m), BF16),
                   jax.ShapeDtypeStruct((m, kv_lora), F32),
                   jax.ShapeDtypeStruct((m, rope), F32)),
        grid=(m // tm,),
        in_specs=[pl.BlockSpec((tm, d), rowb), pl.BlockSpec((1, d), const),
                  full(w["wd"]), full(w["qn"]), full(w["kvn"]), full(w["wuq"]), full(w["wuqr"]),
                  full(w["wuk"]), full(w["wuvt"]),
                  pl.BlockSpec((tm, SLOT), lambda i: (tab_index(i), 0)),
                  pl.BlockSpec((tm, SLOT), lambda i: (tab_index(i), 0))],
        out_specs=(pl.BlockSpec((tm, heads * SLOT), rowb), pl.BlockSpec((heads * SLOT, tm), lambda i: (0, i)),
                   pl.BlockSpec((tm, heads * SLOT), rowb), pl.BlockSpec((heads * VT_SLOT, tm), lambda i: (0, i)),
                   pl.BlockSpec((tm, kv_lora), rowb), pl.BlockSpec((tm, rope), rowb)),
        compiler_params=_cparams("parallel"),
        name="mla_proj",
    )(x, g.reshape(1, d).astype(F32), w["wd"], w["qn"], w["kvn"], w["wuq"], w["wuqr"], w["wuk"], w["wuvt"],
      cos_tab, sin_tab)


def _mla_weights(w_down, q_norm, kv_norm, w_uq, w_uk, w_uv):
    q_lora = q_norm.shape[0]
    kv_lora = kv_norm.shape[0]
    heads = w_uq.shape[1]
    qk = w_uq.shape[2]
    rope = qk - QK_NOPE
    half = rope // 2
    pad = SLOT - qk

    def slot_pair(wr):
        z_lo = jnp.zeros(wr.shape[:-1] + (QK_NOPE,), wr.dtype)
        z_hi = jnp.zeros(wr.shape[:-1] + (pad,), wr.dtype)
        plain = jnp.concatenate([z_lo, wr, z_hi], axis=-1)
        rot = jnp.concatenate([z_lo, -wr[..., half:], wr[..., :half], z_hi], axis=-1)
        return plain, rot

    w_kr = w_down[:, q_lora + kv_lora:]
    kr_plain, kr_rot = slot_pair(w_kr)
    wd = jnp.concatenate([w_down[:, :q_lora + kv_lora], kr_plain, kr_rot], axis=1)
    uq_nope = jnp.concatenate([w_uq[..., :QK_NOPE], jnp.zeros(w_uq.shape[:2] + (SLOT - QK_NOPE,), w_uq.dtype)], -1)
    uq_plain, uq_rot = slot_pair(w_uq[..., QK_NOPE:])
    wuq = (uq_nope + uq_plain).reshape(q_lora, heads * SLOT)
    wuqr = uq_rot.reshape(q_lora, heads * SLOT)
    wuk = jnp.concatenate([w_uk, jnp.zeros(w_uk.shape[:2] + (SLOT - QK_NOPE,), w_uk.dtype)], -1)
    wuk = wuk.reshape(kv_lora, heads * SLOT)
    wuvt = w_uv.reshape(kv_lora, heads * V_HEAD).T
    eye_r = jnp.zeros((SLOT, SLOT), w_uk.dtype).at[QK_NOPE + jnp.arange(rope), jnp.arange(rope)].set(1.0)
    uk_t = jnp.transpose(w_uk, (1, 2, 0))
    uk_t = jnp.concatenate([uk_t, jnp.zeros((heads, SLOT - QK_NOPE, kv_lora), w_uk.dtype)], axis=1)
    wabs = jnp.concatenate([uk_t, jnp.broadcast_to(eye_r, (heads, SLOT, SLOT))], axis=2)
    uv = jnp.transpose(w_uv, (1, 0, 2)).reshape(heads // 2, 2, kv_lora, V_HEAD)
    z = jnp.zeros((heads // 2, kv_lora, V_HEAD), w_uv.dtype)
    wuv_bd = jnp.concatenate([jnp.concatenate([uv[:, 0], z], axis=2), jnp.concatenate([z, uv[:, 1]], axis=2)], axis=1)
    return dict(wd=wd.astype(BF16), qn=q_norm.reshape(1, -1).astype(F32), kvn=kv_norm.reshape(1, -1).astype(F32),
                wuq=wuq.astype(BF16), wuqr=wuqr.astype(BF16), wuk=wuk.astype(BF16), wuvt=wuvt.astype(BF16),
                wabs=wabs.astype(BF16), wuv_bd=wuv_bd.astype(BF16), rope=rope,
                scale=LOG2E / math.sqrt(qk))


def _rope_tables(positions, rope):
    half = rope // 2
    inv = ROPE_THETA ** (-jnp.arange(half, dtype=F32) * (2.0 / rope))
    ang = positions.astype(F32)[:, None] * inv[None, :]
    c, s = jnp.cos(ang), jnp.sin(ang)
    p = positions.shape[0]
    ones = jnp.ones((p, QK_NOPE), F32)
    zeros = jnp.zeros((p, QK_NOPE), F32)
    zpad = jnp.zeros((p, SLOT - QK_NOPE - rope), F32)
    return (jnp.concatenate([ones, c, c, zpad], axis=1), jnp.concatenate([zeros, s, s, zpad], axis=1))


def _headmm_body(x_ref, w_ref, o_ref):
    o_ref[...] = jnp.dot(x_ref[...], w_ref[...], preferred_element_type=F32).astype(o_ref.dtype)


def _headmm(x, w, *, out_dtype=BF16):
    m = x.shape[0]
    g, kb, nb = w.shape
    assert x.shape[1] == g * kb
    return pl.pallas_call(
        _headmm_body,
        out_shape=jax.ShapeDtypeStruct((m, g * nb), out_dtype),
        grid=(g,),
        in_specs=[pl.BlockSpec((m, kb), lambda h: (0, h)), pl.BlockSpec((None, kb, nb), lambda h: (h, 0, 0))],
        out_specs=pl.BlockSpec((m, nb), lambda h: (0, h)),
        compiler_params=_cparams("parallel"),
        name="headmm",
    )(x, w)


def _flash_body(qi_ref, ki_ref, qt_ref, k_ref, vt_ref, o_ref, m_ref, acc_ref, *, heads, tq):
    p_idx = pl.program_id(1)
    qi = qi_ref[p_idx]
    ki = ki_ref[p_idx]

    @pl.when(ki == 0)
    def _():
        m_ref[...] = jnp.full(m_ref.shape, -jnp.inf, F32)
        acc_ref[...] = jnp.zeros(acc_ref.shape, F32)

    def step(masked):
        if masked:
            key_i = lax.broadcasted_iota(jnp.int32, (tq, tq), 0)
            qry_i = lax.broadcasted_iota(jnp.int32, (tq, tq), 1)
            keep = key_i <= qry_i
        def scores(h):
            k = k_ref[:, h * SLOT:(h + 1) * SLOT]
            qt = qt_ref[h * SLOT:(h + 1) * SLOT, :]
            return jnp.dot(k, qt, preferred_element_type=F32)

        ahead = [scores(0), scores(1)]
        for h in range(heads):
            s = ahead.pop(0)
            if h + 2 < heads:
                ahead.append(scores(h + 2))
            if masked:
                s = jnp.where(keep, s, -jnp.inf)
            m_prev = m_ref[h:h + 1, :]
            m_new = jnp.maximum(m_prev, jnp.max(s, axis=0, keepdims=True))
            alpha = jnp.exp2(m_prev - m_new)
            p = jnp.exp2(s - m_new)
            m_ref[h:h + 1, :] = m_new
            vt = vt_ref[h * VT_SLOT:(h + 1) * VT_SLOT, :]
            pv = jnp.dot(vt, p.astype(BF16), preferred_element_type=F32)
            acc_ref[h * VT_SLOT:(h + 1) * VT_SLOT, :] = acc_ref[h * VT_SLOT:(h + 1) * VT_SLOT, :] * alpha + pv

    @pl.when(ki < qi)
    def _():
        step(False)

    @pl.when(ki == qi)
    def _():
        step(True)
        outs = []
        for h in range(heads):
            l = acc_ref[h * VT_SLOT + V_HEAD:h * VT_SLOT + V_HEAD + 1, :]
            outs.append(acc_ref[h * VT_SLOT:h * VT_SLOT + V_HEAD, :] / l)
        o_ref[...] = jnp.concatenate(outs, axis=0).T.astype(o_ref.dtype)


def _flash(qt, kk, vt, *, nb, L, tq=512):
    heads = kk.shape[1] // SLOT
    tq = _row_tile(L, tq)
    nq = L // tq
    pairs = [(i, j) for i in range(nq) for j in range(i + 1)]
    qi_tab = jnp.asarray([p[0] for p in pairs], jnp.int32)
    ki_tab = jnp.asarray([p[1] for p in pairs], jnp.int32)
    grid_spec = pltpu.PrefetchScalarGridSpec(
        num_scalar_prefetch=2,
        grid=(nb, len(pairs)),
        in_specs=[
            pl.BlockSpec((heads * SLOT, tq), lambda b, p, qi, ki: (0, b * nq + qi[p])),
            pl.BlockSpec((tq, heads * SLOT), lambda b, p, qi, ki: (b * nq + ki[p], 0)),
            pl.BlockSpec((heads * VT_SLOT, tq), lambda b, p, qi, ki: (0, b * nq + ki[p])),
        ],
        out_specs=pl.BlockSpec((tq, heads * V_HEAD), lambda b, p, qi, ki: (b * nq + qi[p], 0)),
        scratch_shapes=[pltpu.VMEM((heads, tq), F32), pltpu.VMEM((heads * VT_SLOT, tq), F32)],
    )
    return pl.pallas_call(
        functools.partial(_flash_body, heads=heads, tq=tq),
        out_shape=jax.ShapeDtypeStruct((nb * L, heads * V_HEAD), BF16),
        grid_spec=grid_spec,
        compiler_params=_cparams("parallel", "arbitrary"),
        name="mla_flash",
    )(qi_tab, ki_tab, qt, kk, vt)


def _decode_body(pt_ref, q_ref, cnew_ref, rnew_ref, lat_hbm, ropet_hbm, o_ref,
                 cbuf, rbuf, cb, sem, *, layer, n_pages, page, chunk_pages, kv_lora, rope, heads, ls):
    b = pl.program_id(0)
    nb = pl.num_programs(0)
    slot = lax.rem(b, 2)
    rows = q_ref.shape[0]
    chunk = chunk_pages * page

    def page_copies(seq, sl):
        cps = []
        for i in range(n_pages):
            pg = pt_ref[seq, i]
            cps.append(pltpu.make_async_copy(lat_hbm.at[layer, pg], cbuf.at[sl, pl.ds(i * page, page), :], sem.at[0, sl]))
            cps.append(pltpu.make_async_copy(ropet_hbm.at[layer, pg], rbuf.at[sl, i], sem.at[1, sl]))
        return cps

    @pl.when(b == 0)
    def _():
        for cp in page_copies(0, 0):
            cp.start()

    @pl.when(b + 1 < nb)
    def _():
        for cp in page_copies(b + 1, 1 - slot):
            cp.start()

    for cp in page_copies(b, slot):
        cp.wait()

    q_lat = q_ref[:, :kv_lora]
    q_r = q_ref[:, kv_lora:kv_lora + rope]
    nt = (((1,), (1,)), ((), ()))
    n_chunks = n_pages // chunk_pages

    def scores(ck):
        c_b = cbuf[slot, ck * chunk:(ck + 1) * chunk, :].astype(BF16)
        cb[ck * chunk:(ck + 1) * chunk, :] = c_b
        r_b = jnp.concatenate([rbuf[slot, ck * chunk_pages + i].astype(BF16) for i in range(chunk_pages)],
                              axis=1)
        return (lax.dot_general(q_lat, c_b, nt, preferred_element_type=F32)
                + jnp.dot(q_r, r_b, preferred_element_type=F32))

    cn = cnew_ref[...].astype(BF16)
    rn = rnew_ref[...].astype(BF16)
    sn = (lax.dot_general(q_lat, cn, nt, preferred_element_type=F32)
          + lax.dot_general(q_r, rn, nt, preferred_element_type=F32))
    qpos = lax.broadcasted_iota(jnp.int32, (rows, ls), 0) // heads
    kpos = lax.broadcasted_iota(jnp.int32, (rows, ls), 1)
    sn = jnp.where(qpos >= kpos, sn, -jnp.inf)
    m = jnp.max(sn, axis=-1, keepdims=True)
    pn = jnp.exp2(sn - m)
    l = jnp.sum(pn, axis=-1, keepdims=True)
    acc = jnp.dot(pn.astype(BF16), cn, preferred_element_type=F32)

    s_next = scores(0)
    for ck in range(n_chunks):
        s = s_next
        if ck + 1 < n_chunks:
            s_next = scores(ck + 1)
        m_new = jnp.maximum(m, jnp.max(s, axis=-1, keepdims=True))
        alpha = jnp.exp2(m - m_new)
        p = jnp.exp2(s - m_new)
        l = alpha * l + jnp.sum(p, axis=-1, keepdims=True)
        acc = alpha * acc + jnp.dot(p.astype(BF16), cb[ck * chunk:(ck + 1) * chunk, :], preferred_element_type=F32)
        m = m_new
    o_ref[...] = (acc / l).astype(o_ref.dtype)


def _decode(q_ext, lat_new, kr_new, lat_pool, ropet_pool, page_table, *, layer, bs, ls, heads, new_row_offset,
            chunk_pages=16):
    kv_lora = lat_new.shape[1]
    rope = kr_new.shape[1]
    page = lat_pool.shape[2]
    n_pages = page_table.shape[1]
    chunk_pages = math.gcd(chunk_pages, n_pages)
    assert new_row_offset % ls == 0
    rows = ls * heads
    qw = q_ext.shape[1]
    grid_spec = pltpu.PrefetchScalarGridSpec(
        num_scalar_prefetch=1,
        grid=(bs,),
        in_specs=[pl.BlockSpec((rows, qw), lambda b, pt: (b, 0)),
                  pl.BlockSpec((ls, kv_lora), lambda b, pt: (new_row_offset // ls + b, 0)),
                  pl.BlockSpec((ls, rope), lambda b, pt: (new_row_offset // ls + b, 0)),
                  pl.BlockSpec(memory_space=pl.ANY),
                  pl.BlockSpec(memory_space=pl.ANY)],
        out_specs=pl.BlockSpec((rows, kv_lora), lambda b, pt: (b, 0)),
        scratch_shapes=[pltpu.VMEM((2, n_pages * page, kv_lora), F32),
                        pltpu.VMEM((2, n_pages, rope, page), F32),
                        pltpu.VMEM((n_pages * page, kv_lora), BF16),
                        pltpu.SemaphoreType.DMA((2, 2))],
    )
    return pl.pallas_call(
        functools.partial(_decode_body, layer=layer, n_pages=n_pages, page=page, chunk_pages=chunk_pages,
                          kv_lora=kv_lora, rope=rope, heads=heads, ls=ls),
        out_shape=jax.ShapeDtypeStruct((bs * rows, kv_lora), BF16),
        grid_spec=grid_spec,
        compiler_params=_cparams("arbitrary"),
        name="mla_decode",
    )(page_table, q_ext, lat_new, kr_new, lat_pool, ropet_pool)


MEM_ROWS = 1024
MEM_NB = 4
PROJ_TM = 512
SSD_SAMPLE_NSEQ = 2


def kernel(x_prompt, x_sample, mem_prompt, state_ssm, state_conv, cache_mla_latent, cache_mla_rope_k, cache_mem_k, cache_mem_v, page_table, norm_mix, norm_mem, norm_memkv, norm_ffn, norm_final, ssd_w_in, ssd_conv_w, ssd_conv_b, ssd_dt_bias, ssd_a_log, ssd_d, ssd_norm, ssd_w_out, mla_w_down, mla_q_norm, mla_kv_norm, mla_w_uq, mla_w_uk, mla_w_uv, mla_w_o, mem_w_q, mem_w_kv, mem_w_o, mlp_w_up, mlp_w_down):
    bp, lp, d = x_prompt.shape
    bs, ls, _ = x_sample.shape
    mp, ms = bp * lp, bs * ls
    depth = norm_mix.shape[0]
    n_mem = mem_prompt.shape[1]
    past_len = page_table.shape[1] * cache_mla_latent.shape[2]
    mla_heads = mla_w_uq.shape[2]
    rope = mla_w_uq.shape[3] - QK_NOPE
    ssd_heads = ssd_dt_bias.shape[1]
    d_inner = ssd_heads * SSD_HEADDIM

    x = jnp.concatenate([x_prompt.reshape(mp, d), x_sample.reshape(ms, d)], axis=0)
    mem_rows = mem_prompt.reshape(bp * n_mem, d)
    ropet_pool = jnp.swapaxes(cache_mla_rope_k, 2, 3)
    ssm0 = state_ssm.reshape(state_ssm.shape[0], bs, d_inner, D_STATE)

    pos = jnp.concatenate([jnp.arange(lp), jnp.tile(past_len + jnp.arange(ls), bs)])
    cos_tab, sin_tab = _rope_tables(pos, rope)
    proj_tm = _row_tile(ms, PROJ_TM)
    assert lp % proj_tm == 0
    npt, tpl = mp // proj_tm, lp // proj_tm
    tab_index = lambda i: jnp.where(i < npt, i % tpl, tpl + (i - npt))

    mem_rows_p = _row_tile(lp, MEM_ROWS)
    mem_nb = _row_tile(bs, MEM_NB)

    p_ssm, p_conv, p_lat, p_rk, p_mk, p_mv = [], [], [], [], [], []
    s_conv, s_lat, s_rk = [], [], []
    s_ssm = None
    for i in range(depth):
        j = i // 2
        if i % 2 == 0:
            w_pad = _ssd_w_in_pad(ssd_w_in[j]).astype(BF16)
            prm = _ssd_params(ssd_conv_w[j], ssd_conv_b[j], ssd_dt_bias[j], ssd_a_log[j], ssd_d[j], ssd_norm[j])
            y_p, st_p, ct_p = _ssd_fused(x, norm_mix[i], w_pad, *prm, nb=bp, L=lp, T=math.gcd(SSD_CHUNK, lp))
            zx_s = _mm(x[mp:], w_pad, g=norm_mix[i], tn=_in_proj_tile(w_pad.shape[1]), name="ssd_in")
            conv0 = jnp.pad(state_conv[j], ((0, 0), (8 - state_conv.shape[2], 0), (0, 0)))
            t_s = math.gcd(SSD_CHUNK, ls)
            y_s, s_ssm, ct_s = _ssd(zx_s, *prm, conv0, ssm0, row_offset=0, nb=bs, L=ls, T=t_s, layer=j,
                                    nseq=math.gcd(SSD_SAMPLE_NSEQ, bs) if ls == t_s else 1,
                                    prev_states=None if s_ssm is None else s_ssm.reshape(-1, bs, d_inner, D_STATE))
            x = _mm(y_p, ssd_w_out[j].astype(BF16), x_tail=y_s, res=x, name="ssd_out")
            kc = state_conv.shape[2]
            p_ssm.append(st_p.reshape(bp, ssd_heads, SSD_HEADDIM, D_STATE))
            p_conv.append(ct_p[:, 8 - kc:, :])
            s_conv.append(ct_s[:, 8 - kc:, :])
        else:
            w = _mla_weights(mla_w_down[j], mla_q_norm[j], mla_kv_norm[j], mla_w_uq[j], mla_w_uk[j], mla_w_uv[j])
            q, qt, kk, vt, lat, kr = _mla_proj(x, norm_mix[i], w, cos_tab, sin_tab, tab_index, tm=proj_tm)
            o_p = _flash(qt, kk, vt, nb=bp, L=lp)
            q_ext = _headmm(q[mp:], w["wabs"]).reshape(ms * mla_heads, -1)
            o_lat = _decode(q_ext, lat, kr, cache_mla_latent, ropet_pool, page_table, layer=j, bs=bs, ls=ls,
                            heads=mla_heads, new_row_offset=mp)
            o_s = _headmm(o_lat.reshape(ms, -1), w["wuv_bd"])
            x = _mm(o_p, mla_w_o[j].astype(BF16), x_tail=o_s, res=x, name="mla_out")
            p_lat.append(lat[:mp].reshape(bp, lp, -1))
            s_lat.append(lat[mp:].reshape(bs, ls, -1))
            p_rk.append(kr[:mp].reshape(bp, lp, -1))
            s_rk.append(kr[mp:].reshape(bs, ls, -1))
        kv = _mm(mem_rows, mem_w_kv[i].astype(BF16), g=norm_memkv[i], name="mem_kv")
        kp = kv[:, :d].reshape(bp, n_mem, d)
        vp = kv[:, d:].reshape(bp, n_mem, d)
        p_mk.append(kp.reshape(bp, n_mem, MEM_HEADS, d // MEM_HEADS))
        p_mv.append(vp.reshape(bp, n_mem, MEM_HEADS, d // MEM_HEADS))
        wq, wo = mem_w_q[i].astype(BF16), mem_w_o[i].astype(BF16)
        x_p = _mem_fused(x, norm_mem[i], wq, kp.astype(BF16), vp.astype(BF16), wo, n_seq=bp, L=lp, rows=mem_rows_p)
        x_s = x[mp:]
        qm_s = _mm(x_s, wq, g=norm_mem[i], out_dtype=BF16, name="mem_q")
        o_s = _memattn_dec(qm_s, cache_mem_k, cache_mem_v, layer=i, row_offset=0, n_seq=bs, rows=ls, nb=mem_nb)
        x_s = _mm(o_s, wo, res=x_s, name="mem_out")
        x = _mlp(x_p, norm_ffn[i], mlp_w_up[i].astype(BF16), mlp_w_down[i].astype(BF16), x_tail=x_s,
                 final_g=norm_final if i == depth - 1 else None)
    y_p, y_s = x
    return (y_p.reshape(bp, lp, d), y_s.reshape(bs, ls, d),
            jnp.stack(p_ssm), jnp.stack(p_conv), jnp.stack(p_lat), jnp.stack(p_rk), jnp.stack(p_mk), jnp.stack(p_mv),
            s_ssm.reshape(state_ssm.shape), jnp.stack(s_conv), jnp.stack(s_lat), jnp.stack(s_rk))
```

```python
import functools
import math

import jax
import jax.numpy as jnp
from jax import lax
from jax.experimental import pallas as pl
from jax.experimental.pallas import tpu as pltpu

F32 = jnp.float32
BF16 = jnp.bfloat16

EPS = 1e-6
ROPE_THETA = 10000.0

V7X_LANES = 128
V7X_SUBLANES = 8
V7X_MXU_COLS = 256
V7X_VMEM_LIMIT_BYTES = 56 * 1024 * 1024

SSD_HEADDIM = 64
SSD_GROUPS = 8
D_STATE = 128
SSD_CHUNK = 128
QK_NOPE = 64
V_HEAD = 64
MEM_HEADS = 4


def _cparams(*sem):
    return pltpu.CompilerParams(dimension_semantics=sem, vmem_limit_bytes=V7X_VMEM_LIMIT_BYTES)


def _rms(x, g):
    return x * lax.rsqrt(jnp.mean(x * x, axis=-1, keepdims=True) + EPS) * g


def _row_tile(m, pref):
    t = min(pref, m)
    assert m % t == 0, (m, t)
    return t


def _mm_body(*refs, norm, res, res_split, head_tiles):
    it = iter(refs)
    x_ref = next(it)
    t_ref = next(it) if head_tiles is not None else None
    g_ref = next(it) if norm else None
    w_ref = next(it)
    r_ref = next(it) if res else None
    rt_ref = next(it) if res_split else None
    o_ref = next(it)
    xn_ref = next(it)

    def stage(src_ref):
        x = src_ref[...].astype(F32)
        if norm:
            x = _rms(x, g_ref[...])
        xn_ref[...] = x.astype(BF16)

    first = pl.program_id(1) == 0
    if head_tiles is None:
        pl.when(first)(lambda: stage(x_ref))
    else:
        in_head = pl.program_id(0) < head_tiles
        pl.when(first & in_head)(lambda: stage(x_ref))
        pl.when(first & jnp.logical_not(in_head))(lambda: stage(t_ref))

    acc = jnp.dot(xn_ref[...], w_ref[...], preferred_element_type=F32)
    if res_split:
        in_head = pl.program_id(0) < head_tiles

        @pl.when(in_head)
        def _():
            o_ref[...] = (r_ref[...] + acc).astype(o_ref.dtype)

        @pl.when(jnp.logical_not(in_head))
        def _():
            o_ref[...] = (rt_ref[...] + acc).astype(o_ref.dtype)
    else:
        if res:
            acc = r_ref[...] + acc
        o_ref[...] = acc.astype(o_ref.dtype)


def _mm(x, w, *, x_tail=None, g=None, res=None, res_tail=None, out_dtype=F32, tm=1024, tn=1024, name="mm"):
    m, k = x.shape
    k2, n = w.shape
    assert k == k2
    assert res_tail is None or (x_tail is not None and res is not None
                                and res.shape[0] == m and res_tail.shape[0] == x_tail.shape[0])
    head_tiles = None
    if x_tail is None:
        tm = _row_tile(m, tm)
        in_specs = [pl.BlockSpec((tm, k), lambda i, j: (i, 0))]
        args = [x]
    else:
        m_tail = x_tail.shape[0]
        tm = math.gcd(math.gcd(m, m_tail), tm)
        head_tiles = m // tm
        m = m + m_tail
        in_specs = [pl.BlockSpec((tm, k), lambda i, j: (jnp.minimum(i, head_tiles - 1), 0)),
                    pl.BlockSpec((tm, k), lambda i, j: (jnp.maximum(i - head_tiles, 0), 0))]
        args = [x, x_tail]
    tn = _row_tile(n, tn)
    if g is not None:
        in_specs.append(pl.BlockSpec((1, k), lambda i, j: (0, 0)))
        args.append(g.reshape(1, k).astype(F32))
    in_specs.append(pl.BlockSpec((k, tn), lambda i, j: (0, j)))
    args.append(w)
    if res_tail is not None:
        in_specs += [pl.BlockSpec((tm, tn), lambda i, j: (jnp.minimum(i, head_tiles - 1), j)),
                     pl.BlockSpec((tm, tn), lambda i, j: (jnp.maximum(i - head_tiles, 0), j))]
        args += [res, res_tail]
    elif res is not None:
        in_specs.append(pl.BlockSpec((tm, tn), lambda i, j: (i, j)))
        args.append(res)
    return pl.pallas_call(
        functools.partial(_mm_body, norm=g is not None, res=res is not None, res_split=res_tail is not None,
                          head_tiles=head_tiles),
        out_shape=jax.ShapeDtypeStruct((m, n), out_dtype),
        grid=(m // tm, n // tn),
        in_specs=in_specs,
        out_specs=pl.BlockSpec((tm, tn), lambda i, j: (i, j)),
        scratch_shapes=[pltpu.VMEM((tm, k), BF16)],
        compiler_params=_cparams("parallel", "arbitrary"),
        name=name,
    )(*args)


def _mlp_body(*refs, head_tiles, final_norm):
    x_ref = refs[0]
    t_ref = refs[1] if head_tiles is not None else None
    refs = refs[1 + int(head_tiles is not None):]
    g_ref, wu_ref, wd_ref = refs[:3]
    fg_ref = refs[3] if final_norm else None
    refs = refs[3 + int(final_norm):]
    o_ref = refs[0]
    ot_ref = refs[1] if final_norm else None
    xn_ref, acc_ref = refs[1 + int(final_norm):]
    k = pl.program_id(1)

    def stage(src_ref):
        x = src_ref[...]
        xn_ref[...] = _rms(x, g_ref[...]).astype(BF16)
        acc_ref[...] = x

    if head_tiles is None:
        pl.when(k == 0)(lambda: stage(x_ref))
    else:
        in_head = pl.program_id(0) < head_tiles
        pl.when((k == 0) & in_head)(lambda: stage(x_ref))
        pl.when((k == 0) & jnp.logical_not(in_head))(lambda: stage(t_ref))

    h = jnp.dot(xn_ref[...], wu_ref[...], preferred_element_type=F32)
    h = jnp.square(jnp.maximum(h, 0.0)).astype(BF16)
    acc_ref[...] += jnp.dot(h, wd_ref[...], preferred_element_type=F32)

    last = k == pl.num_programs(1) - 1
    if not final_norm:
        @pl.when(last)
        def _():
            o_ref[...] = acc_ref[...]
    else:
        in_head = pl.program_id(0) < head_tiles

        @pl.when(last & in_head)
        def _():
            o_ref[...] = _rms(acc_ref[...], fg_ref[...])

        @pl.when(last & jnp.logical_not(in_head))
        def _():
            ot_ref[...] = _rms(acc_ref[...], fg_ref[...])


def _mlp(x, g, w_up, w_down, *, x_tail=None, final_g=None, tm=1024, tf=1024):
    m, d = x.shape
    ff = w_up.shape[1]
    head_tiles = None
    assert final_g is None or x_tail is not None
    if x_tail is None:
        tm = _row_tile(m, tm)
        row_specs, row_args = [pl.BlockSpec((tm, d), lambda i, k: (i, 0))], [x]
    else:
        m_tail = x_tail.shape[0]
        tm = math.gcd(math.gcd(m, m_tail), tm)
        head_tiles = m // tm
        m = m + m_tail
        row_specs = [pl.BlockSpec((tm, d), lambda i, k: (jnp.minimum(i, head_tiles - 1), 0)),
                     pl.BlockSpec((tm, d), lambda i, k: (jnp.maximum(i - head_tiles, 0), 0))]
        row_args = [x, x_tail]
    tf = _row_tile(ff, tf)
    if final_g is None:
        out_shape = jax.ShapeDtypeStruct((m, d), F32)
        out_specs = pl.BlockSpec((tm, d), lambda i, k: (i, 0))
        fg_specs, fg_args = [], []
    else:
        out_shape = (jax.ShapeDtypeStruct(x.shape, F32), jax.ShapeDtypeStruct(x_tail.shape, F32))
        out_specs = tuple(row_specs)
        fg_specs, fg_args = [pl.BlockSpec((1, d), lambda i, k: (0, 0))], [final_g.reshape(1, d).astype(F32)]
    return pl.pallas_call(
        functools.partial(_mlp_body, head_tiles=head_tiles, final_norm=final_g is not None),
        out_shape=out_shape,
        grid=(m // tm, ff // tf),
        in_specs=row_specs + [
            pl.BlockSpec((1, d), lambda i, k: (0, 0)),
            pl.BlockSpec((d, tf), lambda i, k: (0, k)),
            pl.BlockSpec((tf, d), lambda i, k: (k, 0)),
        ] + fg_specs,
        out_specs=out_specs,
        scratch_shapes=[pltpu.VMEM((tm, d), BF16), pltpu.VMEM((tm, d), F32)],
        compiler_params=_cparams("parallel" if final_g is None else "arbitrary", "arbitrary"),
        name="mlp",
    )(*row_args, g.reshape(1, d).astype(F32), w_up, w_down, *fg_args)


def _mem_fused_body(x_ref, g_ref, wq_ref, k_ref, v_ref, wo_ref, o_ref, *, heads, scale):
    x = x_ref[...]
    q = jnp.dot(_rms(x, g_ref[...]).astype(BF16), wq_ref[...], preferred_element_type=F32)
    dh = q.shape[-1] // heads
    outs = []
    for h in range(heads):
        q_h = q[:, h * dh:(h + 1) * dh].astype(BF16)
        k_h = k_ref[:, h * dh:(h + 1) * dh]
        v_h = v_ref[:, h * dh:(h + 1) * dh]
        s = lax.dot_general(q_h, k_h, (((1,), (1,)), ((), ())), preferred_element_type=F32) * scale
        s = s - jnp.max(s, axis=-1, keepdims=True)
        p = jnp.exp(s)
        p = p / jnp.sum(p, axis=-1, keepdims=True)
        outs.append(jnp.dot(p.astype(BF16), v_h, preferred_element_type=F32).astype(BF16))
    o = jnp.concatenate(outs, axis=1)
    o_ref[...] = x + jnp.dot(o, wo_ref[...], preferred_element_type=F32)


def _mem_fused(x, g, wq, k, v, wo, *, n_seq, L, rows):
    d = x.shape[-1]
    n_mem = k.shape[-2]
    per_seq = L // rows
    const = lambda s: (0, 0)
    weight = lambda: pl.BlockSpec((d, d), const, pipeline_mode=pl.Buffered(1))
    return pl.pallas_call(
        functools.partial(_mem_fused_body, heads=MEM_HEADS, scale=1.0 / math.sqrt(d // MEM_HEADS)),
        out_shape=jax.ShapeDtypeStruct((n_seq * L, d), F32),
        grid=(n_seq * per_seq,),
        in_specs=[pl.BlockSpec((rows, d), lambda s: (s, 0)),
                  pl.BlockSpec((1, d), const),
                  weight(),
                  pl.BlockSpec((None, n_mem, d), lambda s: (s // per_seq, 0, 0)),
                  pl.BlockSpec((None, n_mem, d), lambda s: (s // per_seq, 0, 0)),
                  weight()],
        out_specs=pl.BlockSpec((rows, d), lambda s: (s, 0)),
        compiler_params=_cparams("parallel"),
        name="mem_fused",
    )(x, g.reshape(1, d).astype(F32), wq, k, v, wo)


def _memattn_dec_body(q_ref, k_ref, v_ref, o_ref, *, nb, rows, heads, scale):
    dh = q_ref.shape[-1] // heads
    n_mem = k_ref.shape[1]
    qrow_head = lax.broadcasted_iota(jnp.int32, (heads * rows, n_mem * heads), 0) // rows
    key_head = lax.broadcasted_iota(jnp.int32, (heads * rows, n_mem * heads), 1) % heads
    own = qrow_head == key_head
    outs = []
    q_all = q_ref[...].astype(F32)
    for i in range(nb):
        qi = q_all[i * rows:(i + 1) * rows, :]
        qs = jnp.concatenate([qi[:, h * dh:(h + 1) * dh] for h in range(heads)], axis=0).astype(BF16)
        k2 = k_ref[i].reshape(n_mem * heads, dh).astype(BF16)
        v2 = v_ref[i].reshape(n_mem * heads, dh).astype(BF16)
        s = lax.dot_general(qs, k2, (((1,), (1,)), ((), ())), preferred_element_type=F32) * scale
        s = jnp.where(own, s, -jnp.inf)
        s = s - jnp.max(s, axis=-1, keepdims=True)
        p = jnp.exp(s)
        p = p / jnp.sum(p, axis=-1, keepdims=True)
        o = jnp.dot(p.astype(BF16), v2, preferred_element_type=F32)
        outs.append(jnp.concatenate([o[h * rows:(h + 1) * rows, :] for h in range(heads)], axis=1))
    o_ref[...] = jnp.concatenate(outs, axis=0).astype(o_ref.dtype)


def _memattn_dec(q, k, v, *, layer, row_offset, n_seq, rows, nb):
    d = q.shape[-1]
    _, _, n_mem, heads, dh = k.shape
    blk = nb * rows
    assert row_offset % blk == 0 and n_seq % nb == 0
    kv_spec = pl.BlockSpec((None, nb, n_mem, heads, dh), lambda s: (layer, s, 0, 0, 0))
    return pl.pallas_call(
        functools.partial(_memattn_dec_body, nb=nb, rows=rows, heads=heads, scale=1.0 / math.sqrt(dh)),
        out_shape=jax.ShapeDtypeStruct((n_seq * rows, d), BF16),
        grid=(n_seq // nb,),
        in_specs=[pl.BlockSpec((blk, d), lambda s: (row_offset // blk + s, 0)), kv_spec, kv_spec],
        out_specs=pl.BlockSpec((blk, d), lambda s: (s, 0)),
        compiler_params=_cparams("parallel"),
        name="memattn_dec",
    )(q, k, v)


def _split3_lanes(v, heads):
    lane = lax.broadcasted_iota(jnp.int32, v.shape, 1)
    hi = v.astype(BF16).astype(F32)
    r1 = v - hi
    mid = r1.astype(BF16).astype(F32)
    lo = r1 - mid
    out = jnp.where(lane < heads, hi, jnp.where(lane < 2 * heads, mid, jnp.where(lane < 3 * heads, lo, 0.0)))
    return out.astype(BF16)


def _ssd_init(conv0_ref, s0_ref, extx_ref, extbc_ref, st_ref):
    d_inner = extx_ref.shape[-1]
    if conv0_ref is not None:
        extx_ref[0:8, :] = conv0_ref[:, 0:d_inner]
        extbc_ref[0:8, :] = conv0_ref[:, d_inner:]
        st_ref[...] = s0_ref[...].T
    else:
        extx_ref[0:8, :] = jnp.zeros((8, d_inner), F32)
        extbc_ref[0:8, :] = jnp.zeros((8, extbc_ref.shape[-1]), F32)
        st_ref[...] = jnp.zeros(st_ref.shape, F32)


def _ssd_finish(sfin_ref, ctail_ref, extx_ref, extbc_ref, st_ref):
    d_inner = extx_ref.shape[-1]
    sfin_ref[...] = st_ref[...].T
    ctail_ref[:, 0:d_inner] = extx_ref[0:8, :]
    ctail_ref[:, d_inner:] = extbc_ref[0:8, :]


def _scan_chunk(z, x_raw, bc_raw, dt_raw, cw_ref, cb_ref, dtb_ref, alog_ref, dsk_ref, ng_ref, e3_ref,
                y_ref, y_row0, extx_ref, extbc_ref, st_ref, *, T, heads, groups):
    d_inner = extx_ref.shape[-1]
    hpg = heads // groups
    gw = d_inner // groups
    hd = d_inner // heads
    n = D_STATE
    gate = z * jax.nn.sigmoid(z)

    extx_ref[8:8 + T, :] = x_raw
    extbc_ref[8:8 + T, :] = bc_raw

    def conv(ext_ref, lo, width):
        cur = ext_ref[8:8 + T, :]
        tail = ext_ref[0:8, :]
        r8 = lax.broadcasted_iota(jnp.int32, (8, width), 0)
        kw = cw_ref.shape[0]
        acc = cb_ref[:, lo:lo + width] + cur * cw_ref[kw - 1:kw, lo:lo + width]
        for s in range(1, kw):
            rolled = pltpu.roll(cur, s, axis=0)
            head = jnp.where(r8 < s, pltpu.roll(tail, s, axis=0), rolled[0:8])
            shifted = head if T == 8 else jnp.concatenate([head, rolled[8:]], axis=0)
            acc = acc + shifted * cw_ref[kw - 1 - s:kw - s, lo:lo + width]
        return acc * jax.nn.sigmoid(acc)

    xc = conv(extx_ref, 0, d_inner)
    bcc = conv(extbc_ref, d_inner, extbc_ref.shape[-1])
    extx_ref[0:8, :] = extx_ref[T:T + 8, :]
    extbc_ref[0:8, :] = extbc_ref[T:T + 8, :]

    dtr = dt_raw + dtb_ref[...]
    dtv = jnp.maximum(dtr, 0.0) + jnp.log1p(jnp.exp(-jnp.abs(dtr)))
    a = dtv * (-jnp.exp(alog_ref[...]))
    row = lax.broadcasted_iota(jnp.int32, a.shape, 0)
    acs = a
    sh = 1
    while sh < T:
        acs = acs + jnp.where(row >= sh, pltpu.roll(acs, sh, axis=0), 0.0)
        sh *= 2
    a_last = acs[T - 1:T, :]
    e3 = e3_ref[...]

    per_head = [dtv, jnp.exp(a_last - acs), jnp.exp(acs)]
    if T % V7X_LANES == 0:
        stacked = jnp.concatenate([_split3_lanes(v, heads) for v in per_head], axis=0)
        expanded = jnp.dot(stacked, e3, preferred_element_type=F32)
        dt_e, dend_e, eacs_e = expanded[0:T], expanded[T:2 * T], expanded[2 * T:3 * T]
    else:
        dt_e, dend_e, eacs_e = [jnp.dot(_split3_lanes(v, heads), e3, preferred_element_type=F32) for v in per_head]
    xdt = xc * dt_e
    xdt_b = xdt.astype(BF16)
    xw_b = (xdt * dend_e).astype(BF16)
    cdec_e = eacs_e[T - 1:T, :]

    acs2 = acs * LOG2E
    if T < V7X_LANES:
        acs_p = jnp.concatenate([acs2, jnp.zeros((V7X_LANES - T, V7X_LANES), F32)], axis=0)
    else:
        acs_p = acs2
    acs_t = acs_p.T
    ti = lax.broadcasted_iota(jnp.int32, (T, T), 0)
    si = lax.broadcasted_iota(jnp.int32, (T, T), 1)
    causal = ti >= si
    lane_g = lax.broadcasted_iota(jnp.int32, (T, gw), 1)

    for g in range(groups):
        bg = bcc[:, g * n:(g + 1) * n]
        cg = bcc[:, groups * n + g * n: groups * n + (g + 1) * n].astype(BF16)
        bg_b = bg.astype(BF16)
        gmat = lax.dot_general(cg, bg_b, (((1,), (1,)), ((), ())), preferred_element_type=F32)
        st_g = st_ref[:, g * gw:(g + 1) * gw]
        y_g = jnp.dot(cg, st_g.astype(BF16), preferred_element_type=F32) * eacs_e[:, g * gw:(g + 1) * gw]
        xg = xdt_b[:, g * gw:(g + 1) * gw]
        m_heads, x_heads = [], []
        for j in range(hpg):
            h = g * hpg + j
            seg = acs2[:, h:h + 1] - acs_t[h:h + 1, 0:T]
            decay = jnp.exp2(jnp.where(causal, seg, -jnp.inf))
            m_heads.append((gmat * decay).astype(BF16))
            x_heads.append(jnp.where((lane_g >= j * hd) & (lane_g < (j + 1) * hd), xg, jnp.zeros_like(xg)))
        if T % V7X_LANES == 0:
            y_g = y_g + jnp.dot(jnp.concatenate(m_heads, axis=1), jnp.concatenate(x_heads, axis=0),
                                preferred_element_type=F32)
        else:
            for m_h, x_h in zip(m_heads, x_heads):
                y_g = y_g + jnp.dot(m_h, x_h, preferred_element_type=F32)
        cs_t = jnp.dot(bg.T.astype(BF16), xw_b[:, g * gw:(g + 1) * gw], preferred_element_type=F32)
        st_ref[:, g * gw:(g + 1) * gw] = st_g * cdec_e[:, g * gw:(g + 1) * gw] + cs_t
        y_g = y_g + xc[:, g * gw:(g + 1) * gw] * dsk_ref[:, g * gw:(g + 1) * gw]
        y_g = y_g * gate[:, g * gw:(g + 1) * gw]
        y_g = _rms(y_g, ng_ref[:, g * gw:(g + 1) * gw])
        y_ref[y_row0:y_row0 + T, g * gw:(g + 1) * gw] = y_g.astype(y_ref.dtype)


def _ssd_body(*refs, T, heads, groups, nseq, has_prev):
    (z_ref, x_ref, bc_ref, dt_ref, cw_ref, cb_ref, dtb_ref, alog_ref, dsk_ref, ng_ref, e3_ref,
     conv0_ref, s0_ref) = refs[:13]
    prev_ref = refs[13] if has_prev else None
    y_ref, sfin_ref, ctail_ref, extx_ref, extbc_ref, st_ref = refs[13 + int(has_prev):]
    c = pl.program_id(1)

    @pl.when(c == 0)
    def _():
        for i in range(nseq):
            _ssd_init(conv0_ref.at[i], s0_ref.at[i], extx_ref.at[i], extbc_ref.at[i], st_ref.at[i])

    for i in range(nseq):
        rows = slice(i * T, (i + 1) * T)
        _scan_chunk(z_ref[rows, :], x_ref[rows, :], bc_ref[rows, :], dt_ref[rows, :],
                    cw_ref, cb_ref, dtb_ref, alog_ref, dsk_ref, ng_ref, e3_ref,
                    y_ref, i * T, extx_ref.at[i], extbc_ref.at[i], st_ref.at[i], T=T, heads=heads, groups=groups)

    @pl.when(c == pl.num_programs(1) - 1)
    def _():
        for i in range(nseq):
            if has_prev:
                n_prev = prev_ref.shape[0]
                sfin_ref[0:n_prev, i] = prev_ref[:, i]
                sfin_i = sfin_ref.at[n_prev, i]
            else:
                sfin_i = sfin_ref.at[i]
            _ssd_finish(sfin_i, ctail_ref.at[i], extx_ref.at[i], extbc_ref.at[i], st_ref.at[i])


def _ssd_fused_body(xcur_ref, xnext_ref, g_ref, win_ref, cw_ref, cb_ref, dtb_ref, alog_ref, dsk_ref, ng_ref, e3_ref,
                    y_ref, sfin_ref, ctail_ref,
                    extx_ref, extbc_ref, st_ref, zxa_ref, zxb_ref, *, T, heads, groups):
    k = pl.program_id(1)
    d_inner = extx_ref.shape[-1]
    bcw = extbc_ref.shape[-1]

    def in_proj(rows):
        xn = _rms(rows, g_ref[...]).astype(BF16)
        return jnp.dot(xn, win_ref[...], preferred_element_type=F32)

    def scan(zx_ref, y_row0):
        _scan_chunk(zx_ref[:, 0:d_inner], zx_ref[:, d_inner:2 * d_inner],
                    zx_ref[:, 2 * d_inner:2 * d_inner + bcw],
                    zx_ref[:, 2 * d_inner + bcw:2 * d_inner + bcw + V7X_LANES],
                    cw_ref, cb_ref, dtb_ref, alog_ref, dsk_ref, ng_ref, e3_ref,
                    y_ref, y_row0, extx_ref, extbc_ref, st_ref, T=T, heads=heads, groups=groups)

    @pl.when(k == 0)
    def _():
        _ssd_init(None, None, extx_ref, extbc_ref, st_ref)
        zxa_ref[...] = in_proj(xcur_ref[0:T, :])

    zxb_ref[...] = in_proj(xcur_ref[T:2 * T, :])
    scan(zxa_ref, 0)
    zxa_ref[...] = in_proj(xnext_ref[0:T, :])
    scan(zxb_ref, T)
    pl.when(k == pl.num_programs(1) - 1)(
        functools.partial(_ssd_finish, sfin_ref.at[0], ctail_ref.at[0], extx_ref, extbc_ref, st_ref))


def _ssd(zx, conv_w, conv_b, dt_bias_rep, a_log_rep, d_exp, norm_g, e3, conv0, s0, *, row_offset, nb, L, T,
         layer, nseq=1, prev_states=None):
    heads = d_exp.shape[-1] // SSD_HEADDIM
    d_inner = d_exp.shape[-1]
    bcw = 2 * SSD_GROUPS * D_STATE
    assert bcw == d_inner, "column blocks are indexed in units of d_inner"
    nc = L // T
    rows = nseq * T
    assert nb % nseq == 0 and row_offset % rows == 0 and (nseq == 1 or nc == 1)
    rb0 = row_offset // rows
    rowmap = lambda col: (lambda b, c: (rb0 + b * nc + c, col))
    const = lambda b, c: (0, 0)
    per_b = lambda b, c: (b, 0, 0)
    dt_col = (2 * d_inner + bcw) // V7X_LANES
    if prev_states is None:
        prev_args, prev_specs = [], []
        sfin_shape = (nb, d_inner, D_STATE)
        sfin_spec = pl.BlockSpec((nseq, d_inner, D_STATE), per_b)
    else:
        n_prev = prev_states.shape[0]
        prev_args = [prev_states]
        prev_specs = [pl.BlockSpec((n_prev, nseq, d_inner, D_STATE), lambda b, c: (0, b, 0, 0))]
        sfin_shape = (n_prev + 1, nb, d_inner, D_STATE)
        sfin_spec = pl.BlockSpec((n_prev + 1, nseq, d_inner, D_STATE), lambda b, c: (0, b, 0, 0))
    y, sfin, ctail = pl.pallas_call(
        functools.partial(_ssd_body, T=T, heads=heads, groups=SSD_GROUPS, nseq=nseq,
                          has_prev=prev_states is not None),
        out_shape=(jax.ShapeDtypeStruct((nb * L, d_inner), BF16),
                   jax.ShapeDtypeStruct(sfin_shape, F32),
                   jax.ShapeDtypeStruct((nb, 8, d_inner + bcw), F32)),
        grid=(nb // nseq, nc),
        in_specs=[
            pl.BlockSpec((rows, d_inner), rowmap(0)),
            pl.BlockSpec((rows, d_inner), rowmap(1)),
            pl.BlockSpec((rows, bcw), rowmap(2)),
            pl.BlockSpec((rows, V7X_LANES), rowmap(dt_col)),
            pl.BlockSpec((4, d_inner + bcw), const),
            pl.BlockSpec((1, d_inner + bcw), const),
            pl.BlockSpec((1, V7X_LANES), const),
            pl.BlockSpec((1, V7X_LANES), const),
            pl.BlockSpec((1, d_inner), const),
            pl.BlockSpec((1, d_inner), const),
            pl.BlockSpec((V7X_LANES, d_inner), const),
            pl.BlockSpec((nseq, 8, d_inner + bcw), per_b),
            pl.BlockSpec((None, nseq, d_inner, D_STATE), lambda b, c: (layer, b, 0, 0)),
        ] + prev_specs,
        out_specs=(pl.BlockSpec((rows, d_inner), lambda b, c: (b * nc + c, 0)),
                   sfin_spec,
                   pl.BlockSpec((nseq, 8, d_inner + bcw), per_b)),
        scratch_shapes=[pltpu.VMEM((nseq, T + 8, d_inner), F32), pltpu.VMEM((nseq, T + 8, bcw), F32),
                        pltpu.VMEM((nseq, D_STATE, d_inner), F32)],
        compiler_params=_cparams("parallel", "arbitrary"),
        name="ssd",
    )(zx, zx, zx, zx, conv_w, conv_b, dt_bias_rep, a_log_rep, d_exp, norm_g, e3, conv0, s0, *prev_args)
    return y, sfin, ctail


def _ssd_fused(x, g, w_in, conv_w, conv_b, dt_bias_rep, a_log_rep, d_exp, norm_g, e3, *, nb, L, T):
    d = x.shape[1]
    heads = d_exp.shape[-1] // SSD_HEADDIM
    d_inner = d_exp.shape[-1]
    bcw = 2 * SSD_GROUPS * D_STATE
    n_in = w_in.shape[1]
    assert L % (2 * T) == 0
    nk = L // (2 * T)
    const = lambda b, k: (0, 0)
    per_b = lambda b, k: (b, 0, 0)
    return pl.pallas_call(
        functools.partial(_ssd_fused_body, T=T, heads=heads, groups=SSD_GROUPS),
        out_shape=(jax.ShapeDtypeStruct((nb * L, d_inner), BF16),
                   jax.ShapeDtypeStruct((nb, d_inner, D_STATE), F32),
                   jax.ShapeDtypeStruct((nb, 8, d_inner + bcw), F32)),
        grid=(nb, nk),
        in_specs=[
            pl.BlockSpec((2 * T, d), lambda b, k: (b * nk + k, 0)),
            pl.BlockSpec((2 * T, d), lambda b, k: (b * nk + jnp.minimum(k + 1, nk - 1), 0)),
            pl.BlockSpec((1, d), const),
            pl.BlockSpec((d, n_in), const, pipeline_mode=pl.Buffered(1)),
            pl.BlockSpec((4, d_inner + bcw), const),
            pl.BlockSpec((1, d_inner + bcw), const),
            pl.BlockSpec((1, V7X_LANES), const),
            pl.BlockSpec((1, V7X_LANES), const),
            pl.BlockSpec((1, d_inner), const),
            pl.BlockSpec((1, d_inner), const),
            pl.BlockSpec((V7X_LANES, d_inner), const),
        ],
        out_specs=(pl.BlockSpec((2 * T, d_inner), lambda b, k: (b * nk + k, 0)),
                   pl.BlockSpec((1, d_inner, D_STATE), per_b),
                   pl.BlockSpec((1, 8, d_inner + bcw), per_b)),
        scratch_shapes=[pltpu.VMEM((T + 8, d_inner), F32), pltpu.VMEM((T + 8, bcw), F32),
                        pltpu.VMEM((D_STATE, d_inner), F32),
                        pltpu.VMEM((T, n_in), F32), pltpu.VMEM((T, n_in), F32)],
        compiler_params=_cparams("parallel", "arbitrary"),
        name="ssd_fused",
    )(x, x, g.reshape(1, d).astype(F32), w_in, conv_w, conv_b, dt_bias_rep, a_log_rep, d_exp, norm_g, e3)


def _ssd_w_in_pad(w_in):
    heads = w_in.shape[1] - (w_in.shape[1] // V7X_LANES) * V7X_LANES
    main = w_in[:, :w_in.shape[1] - heads]
    dt = w_in[:, w_in.shape[1] - heads:]
    return jnp.concatenate([main] + [dt] * (V7X_MXU_COLS // heads), axis=1)


def _in_proj_tile(n):
    return max(t for t in range(V7X_MXU_COLS, 1536 + 1, V7X_MXU_COLS) if n % t == 0)


def _ssd_params(conv_w, conv_b, dt_bias, a_log, d_skip, norm_g):
    heads = dt_bias.shape[0]
    rep = V7X_LANES // heads
    d_inner = heads * SSD_HEADDIM
    src = jnp.arange(V7X_LANES)[:, None]
    dst_head = jnp.arange(d_inner)[None, :] // SSD_HEADDIM
    e3 = ((src % heads == dst_head) & (src < 3 * heads)).astype(BF16)
    return (conv_w.astype(F32), conv_b.reshape(1, -1).astype(F32),
            jnp.tile(dt_bias.reshape(1, heads), (1, rep)).astype(F32),
            jnp.tile(a_log.reshape(1, heads), (1, rep)).astype(F32),
            jnp.repeat(d_skip, SSD_HEADDIM).reshape(1, d_inner).astype(F32),
            norm_g.reshape(1, d_inner).astype(F32), e3)


SLOT = V7X_LANES
VT_SLOT = V_HEAD + 16
LOG2E = math.log2(math.e)
FLASH_SCORES_AHEAD = 2


def _mla_proj_body(x_ref, g_ref, wd_ref, qn_ref, kvn_ref, wuq_ref, wuqr_ref, wuk_ref, wuvt_ref, cos_ref, sin_ref,
                   q_ref, qt_ref, kk_ref, vt_ref, lat_ref, kr_ref, *, heads, q_lora, kv_lora, rope, scale):
    xn = _rms(x_ref[...], g_ref[...]).astype(BF16)
    down = jnp.dot(xn, wd_ref[...], preferred_element_type=F32)
    cq = down[:, :q_lora]
    ckv = down[:, q_lora:q_lora + kv_lora]
    krs = down[:, q_lora + kv_lora:q_lora + kv_lora + SLOT]
    krr = down[:, q_lora + kv_lora + SLOT:]
    cos = cos_ref[...]
    sin = sin_ref[...]
    cqn = _rms(cq, qn_ref[...]).astype(BF16)
    qp = jnp.dot(cqn, wuq_ref[...], preferred_element_type=F32)
    qr = jnp.dot(cqn, wuqr_ref[...], preferred_element_type=F32)
    lat = _rms(ckv, kvn_ref[...])
    lat_ref[...] = lat
    kr_rot = krs * cos + krr * sin
    kr_ref[...] = kr_rot[:, QK_NOPE:QK_NOPE + rope]
    latb = lat.astype(BF16)
    kn = jnp.dot(latb, wuk_ref[...], preferred_element_type=F32)
    for h in range(heads):
        sl = slice(h * SLOT, (h + 1) * SLOT)
        q_h = (qp[:, sl] * cos + qr[:, sl] * sin) * scale
        q_ref[:, sl] = q_h.astype(BF16)
        qt_ref[sl, :] = q_h.T.astype(BF16)
        kk_ref[:, sl] = (kn[:, sl] + kr_rot).astype(BF16)
    vt = lax.dot_general(wuvt_ref[...], latb, (((1,), (1,)), ((), ())),
                         preferred_element_type=F32)
    fill_rows = lax.broadcasted_iota(jnp.int32, (VT_SLOT - V_HEAD, vt.shape[1]), 0)
    ones_then_zeros = jnp.where(fill_rows == 0, 1.0, 0.0).astype(BF16)
    for h in range(heads):
        vt_ref[h * VT_SLOT:h * VT_SLOT + V_HEAD, :] = vt[h * V_HEAD:(h + 1) * V_HEAD, :].astype(BF16)
        vt_ref[h * VT_SLOT + V_HEAD:(h + 1) * VT_SLOT, :] = ones_then_zeros


def _mla_proj(x, g, w, cos_tab, sin_tab, tab_index, *, tm=512):
    m, d = x.shape
    heads = w["wuq"].shape[1] // SLOT
    q_lora = w["qn"].shape[-1]
    kv_lora = w["kvn"].shape[-1]
    rope = w["rope"]
    tm = _row_tile(m, tm)
    const = lambda i: (0, 0)
    rowb = lambda i: (i, 0)
    full = lambda a: pl.BlockSpec(a.shape, const)
    return pl.pallas_call(
        functools.partial(_mla_proj_body, heads=heads, q_lora=q_lora, kv_lora=kv_lora, rope=rope,
                          scale=w["scale"]),
        out_shape=(jax.ShapeDtypeStruct((m, heads * SLOT), BF16),
                   jax.ShapeDtypeStruct((heads * SLOT, m), BF16),
                   jax.ShapeDtypeStruct((m, heads * SLOT), BF16),
                   jax.ShapeDtypeStruct((heads * VT_SLOT, m), BF16),
                   jax.ShapeDtypeStruct((m, kv_lora), F32),
                   jax.ShapeDtypeStruct((m, rope), F32)),
        grid=(m // tm,),
        in_specs=[pl.BlockSpec((tm, d), rowb), pl.BlockSpec((1, d), const),
                  full(w["wd"]), full(w["qn"]), full(w["kvn"]), full(w["wuq"]), full(w["wuqr"]),
                  full(w["wuk"]), full(w["wuvt"]),
                  pl.BlockSpec((tm, SLOT), lambda i: (tab_index(i), 0)),
                  pl.BlockSpec((tm, SLOT), lambda i: (tab_index(i), 0))],
        out_specs=(pl.BlockSpec((tm, heads * SLOT), rowb), pl.BlockSpec((heads * SLOT, tm), lambda i: (0, i)),
                   pl.BlockSpec((tm, heads * SLOT), rowb), pl.BlockSpec((heads * VT_SLOT, tm), lambda i: (0, i)),
                   pl.BlockSpec((tm, kv_lora), rowb), pl.BlockSpec((tm, rope), rowb)),
        compiler_params=_cparams("parallel"),
        name="mla_proj",
    )(x, g.reshape(1, d).astype(F32), w["wd"], w["qn"], w["kvn"], w["wuq"], w["wuqr"], w["wuk"], w["wuvt"],
      cos_tab, sin_tab)


def _mla_weights(w_down, q_norm, kv_norm, w_uq, w_uk, w_uv):
    q_lora = q_norm.shape[0]
    kv_lora = kv_norm.shape[0]
    heads = w_uq.shape[1]
    qk = w_uq.shape[2]
    rope = qk - QK_NOPE
    half = rope // 2
    pad = SLOT - qk

    def slot_pair(wr):
        z_lo = jnp.zeros(wr.shape[:-1] + (QK_NOPE,), wr.dtype)
        z_hi = jnp.zeros(wr.shape[:-1] + (pad,), wr.dtype)
        plain = jnp.concatenate([z_lo, wr, z_hi], axis=-1)
        rot = jnp.concatenate([z_lo, -wr[..., half:], wr[..., :half], z_hi], axis=-1)
        return plain, rot

    w_kr = w_down[:, q_lora + kv_lora:]
    kr_plain, kr_rot = slot_pair(w_kr)
    wd = jnp.concatenate([w_down[:, :q_lora + kv_lora], kr_plain, kr_rot], axis=1)
    uq_nope = jnp.concatenate([w_uq[..., :QK_NOPE], jnp.zeros(w_uq.shape[:2] + (SLOT - QK_NOPE,), w_uq.dtype)], -1)
    uq_plain, uq_rot = slot_pair(w_uq[..., QK_NOPE:])
    wuq = (uq_nope + uq_plain).reshape(q_lora, heads * SLOT)
    wuqr = uq_rot.reshape(q_lora, heads * SLOT)
    wuk = jnp.concatenate([w_uk, jnp.zeros(w_uk.shape[:2] + (SLOT - QK_NOPE,), w_uk.dtype)], -1)
    wuk = wuk.reshape(kv_lora, heads * SLOT)
    wuvt = w_uv.reshape(kv_lora, heads * V_HEAD).T
    eye_r = jnp.zeros((SLOT, SLOT), w_uk.dtype).at[QK_NOPE + jnp.arange(rope), jnp.arange(rope)].set(1.0)
    uk_t = jnp.transpose(w_uk, (1, 2, 0))
    uk_t = jnp.concatenate([uk_t, jnp.zeros((heads, SLOT - QK_NOPE, kv_lora), w_uk.dtype)], axis=1)
    wabs = jnp.concatenate([uk_t, jnp.broadcast_to(eye_r, (heads, SLOT, SLOT))], axis=2)
    uv = jnp.transpose(w_uv, (1, 0, 2)).reshape(heads // 2, 2, kv_lora, V_HEAD)
    z = jnp.zeros((heads // 2, kv_lora, V_HEAD), w_uv.dtype)
    wuv_bd = jnp.concatenate([jnp.concatenate([uv[:, 0], z], axis=2), jnp.concatenate([z, uv[:, 1]], axis=2)], axis=1)
    return dict(wd=wd.astype(BF16), qn=q_norm.reshape(1, -1).astype(F32), kvn=kv_norm.reshape(1, -1).astype(F32),
                wuq=wuq.astype(BF16), wuqr=wuqr.astype(BF16), wuk=wuk.astype(BF16), wuvt=wuvt.astype(BF16),
                wabs=wabs.astype(BF16), wuv_bd=wuv_bd.astype(BF16), rope=rope,
                scale=LOG2E / math.sqrt(qk))


def _rope_tables(positions, rope):
    half = rope // 2
    inv = ROPE_THETA ** (-jnp.arange(half, dtype=F32) * (2.0 / rope))
    ang = positions.astype(F32)[:, None] * inv[None, :]
    c, s = jnp.cos(ang), jnp.sin(ang)
    p = positions.shape[0]
    ones = jnp.ones((p, QK_NOPE), F32)
    zeros = jnp.zeros((p, QK_NOPE), F32)
    zpad = jnp.zeros((p, SLOT - QK_NOPE - rope), F32)
    return (jnp.concatenate([ones, c, c, zpad], axis=1), jnp.concatenate([zeros, s, s, zpad], axis=1))


def _headmm_body(x_ref, w_ref, o_ref):
    o_ref[...] = jnp.dot(x_ref[...], w_ref[...], preferred_element_type=F32).astype(o_ref.dtype)


def _headmm(x, w, *, out_dtype=BF16):
    m = x.shape[0]
    g, kb, nb = w.shape
    assert x.shape[1] == g * kb
    return pl.pallas_call(
        _headmm_body,
        out_shape=jax.ShapeDtypeStruct((m, g * nb), out_dtype),
        grid=(g,),
        in_specs=[pl.BlockSpec((m, kb), lambda h: (0, h)), pl.BlockSpec((None, kb, nb), lambda h: (h, 0, 0))],
        out_specs=pl.BlockSpec((m, nb), lambda h: (0, h)),
        compiler_params=_cparams("parallel"),
        name="headmm",
    )(x, w)


def _flash_body(qi_ref, ki_ref, qt_ref, k_ref, vt_ref, o_ref, m_ref, acc_ref, *, heads, tq):
    p_idx = pl.program_id(1)
    qi = qi_ref[p_idx]
    ki = ki_ref[p_idx]

    @pl.when(ki == 0)
    def _():
        m_ref[...] = jnp.full(m_ref.shape, -jnp.inf, F32)
        acc_ref[...] = jnp.zeros(acc_ref.shape, F32)

    def step(masked):
        if masked:
            key_i = lax.broadcasted_iota(jnp.int32, (tq, tq), 0)
            qry_i = lax.broadcasted_iota(jnp.int32, (tq, tq), 1)
            keep = key_i <= qry_i
        def scores(h):
            k = k_ref[:, h * SLOT:(h + 1) * SLOT]
            qt = qt_ref[h * SLOT:(h + 1) * SLOT, :]
            return jnp.dot(k, qt, preferred_element_type=F32)

        ahead = [scores(h) for h in range(FLASH_SCORES_AHEAD)]
        for h in range(heads):
            s = ahead.pop(0)
            if h + FLASH_SCORES_AHEAD < heads:
                ahead.append(scores(h + FLASH_SCORES_AHEAD))
            if masked:
                s = jnp.where(keep, s, -jnp.inf)
            m_prev = m_ref[h:h + 1, :]
            m_new = jnp.maximum(m_prev, jnp.max(s, axis=0, keepdims=True))
            alpha = jnp.exp2(m_prev - m_new)
            p = jnp.exp2(s - m_new)
            m_ref[h:h + 1, :] = m_new
            vt = vt_ref[h * VT_SLOT:(h + 1) * VT_SLOT, :]
            pv = jnp.dot(vt, p.astype(BF16), preferred_element_type=F32)
            acc_ref[h * VT_SLOT:(h + 1) * VT_SLOT, :] = acc_ref[h * VT_SLOT:(h + 1) * VT_SLOT, :] * alpha + pv

    @pl.when(ki < qi)
    def _():
        step(False)

    @pl.when(ki == qi)
    def _():
        step(True)
        outs = []
        for h in range(heads):
            l = acc_ref[h * VT_SLOT + V_HEAD:h * VT_SLOT + V_HEAD + 1, :]
            outs.append(acc_ref[h * VT_SLOT:h * VT_SLOT + V_HEAD, :] / l)
        o_ref[...] = jnp.concatenate(outs, axis=0).T.astype(o_ref.dtype)


def _flash(qt, kk, vt, *, nb, L, tq=512):
    heads = kk.shape[1] // SLOT
    tq = _row_tile(L, tq)
    nq = L // tq
    pairs = [(i, j) for i in range(nq) for j in range(i + 1)]
    qi_tab = jnp.asarray([p[0] for p in pairs], jnp.int32)
    ki_tab = jnp.asarray([p[1] for p in pairs], jnp.int32)
    grid_spec = pltpu.PrefetchScalarGridSpec(
        num_scalar_prefetch=2,
        grid=(nb, len(pairs)),
        in_specs=[
            pl.BlockSpec((heads * SLOT, tq), lambda b, p, qi, ki: (0, b * nq + qi[p])),
            pl.BlockSpec((tq, heads * SLOT), lambda b, p, qi, ki: (b * nq + ki[p], 0)),
            pl.BlockSpec((heads * VT_SLOT, tq), lambda b, p, qi, ki: (0, b * nq + ki[p])),
        ],
        out_specs=pl.BlockSpec((tq, heads * V_HEAD), lambda b, p, qi, ki: (b * nq + qi[p], 0)),
        scratch_shapes=[pltpu.VMEM((heads, tq), F32), pltpu.VMEM((heads * VT_SLOT, tq), F32)],
    )
    return pl.pallas_call(
        functools.partial(_flash_body, heads=heads, tq=tq),
        out_shape=jax.ShapeDtypeStruct((nb * L, heads * V_HEAD), BF16),
        grid_spec=grid_spec,
        compiler_params=_cparams("parallel", "arbitrary"),
        name="mla_flash",
    )(qi_tab, ki_tab, qt, kk, vt)


def _decode_body(pt_ref, q_ref, cnew_ref, rnew_ref, lat_hbm, ropet_hbm, o_ref,
                 cbuf, rbuf, cb, sem, *, layer, n_pages, page, chunk_pages, kv_lora, rope, heads, ls):
    b = pl.program_id(0)
    nb = pl.num_programs(0)
    slot = lax.rem(b, 2)
    rows = q_ref.shape[0]
    chunk = chunk_pages * page

    def page_copies(seq, sl):
        cps = []
        for i in range(n_pages):
            pg = pt_ref[seq, i]
            cps.append(pltpu.make_async_copy(lat_hbm.at[layer, pg], cbuf.at[sl, pl.ds(i * page, page), :], sem.at[0, sl]))
            cps.append(pltpu.make_async_copy(ropet_hbm.at[layer, pg], rbuf.at[sl, i], sem.at[1, sl]))
        return cps

    @pl.when(b == 0)
    def _():
        for cp in page_copies(0, 0):
            cp.start()

    @pl.when(b + 1 < nb)
    def _():
        for cp in page_copies(b + 1, 1 - slot):
            cp.start()

    for cp in page_copies(b, slot):
        cp.wait()

    q_lat = q_ref[:, :kv_lora]
    q_r = q_ref[:, kv_lora:kv_lora + rope]
    nt = (((1,), (1,)), ((), ()))
    n_chunks = n_pages // chunk_pages

    def scores(ck):
        c_b = cbuf[slot, ck * chunk:(ck + 1) * chunk, :].astype(BF16)
        cb[ck * chunk:(ck + 1) * chunk, :] = c_b
        r_b = jnp.concatenate([rbuf[slot, ck * chunk_pages + i].astype(BF16) for i in range(chunk_pages)],
                              axis=1)
        return (lax.dot_general(q_lat, c_b, nt, preferred_element_type=F32)
                + jnp.dot(q_r, r_b, preferred_element_type=F32))

    cn = cnew_ref[...].astype(BF16)
    rn = rnew_ref[...].astype(BF16)
    sn = (lax.dot_general(q_lat, cn, nt, preferred_element_type=F32)
          + lax.dot_general(q_r, rn, nt, preferred_element_type=F32))
    qpos = lax.broadcasted_iota(jnp.int32, (rows, ls), 0) // heads
    kpos = lax.broadcasted_iota(jnp.int32, (rows, ls), 1)
    sn = jnp.where(qpos >= kpos, sn, -jnp.inf)
    m = jnp.max(sn, axis=-1, keepdims=True)
    pn = jnp.exp2(sn - m)
    l = jnp.sum(pn, axis=-1, keepdims=True)
    acc = jnp.dot(pn.astype(BF16), cn, preferred_element_type=F32)

    s_next = scores(0)
    for ck in range(n_chunks):
        s = s_next
        if ck + 1 < n_chunks:
            s_next = scores(ck + 1)
        m_new = jnp.maximum(m, jnp.max(s, axis=-1, keepdims=True))
        alpha = jnp.exp2(m - m_new)
        p = jnp.exp2(s - m_new)
        l = alpha * l + jnp.sum(p, axis=-1, keepdims=True)
        acc = alpha * acc + jnp.dot(p.astype(BF16), cb[ck * chunk:(ck + 1) * chunk, :], preferred_element_type=F32)
        m = m_new
    o_ref[...] = (acc / l).astype(o_ref.dtype)


def _decode(q_ext, lat_new, kr_new, lat_pool, ropet_pool, page_table, *, layer, bs, ls, heads, new_row_offset,
            chunk_pages=16):
    kv_lora = lat_new.shape[1]
    rope = kr_new.shape[1]
    page = lat_pool.shape[2]
    n_pages = page_table.shape[1]
    chunk_pages = math.gcd(chunk_pages, n_pages)
    assert new_row_offset % ls == 0
    rows = ls * heads
    qw = q_ext.shape[1]
    grid_spec = pltpu.PrefetchScalarGridSpec(
        num_scalar_prefetch=1,
        grid=(bs,),
        in_specs=[pl.BlockSpec((rows, qw), lambda b, pt: (b, 0)),
                  pl.BlockSpec((ls, kv_lora), lambda b, pt: (new_row_offset // ls + b, 0)),
                  pl.BlockSpec((ls, rope), lambda b, pt: (new_row_offset // ls + b, 0)),
                  pl.BlockSpec(memory_space=pl.ANY),
                  pl.BlockSpec(memory_space=pl.ANY)],
        out_specs=pl.BlockSpec((rows, kv_lora), lambda b, pt: (b, 0)),
        scratch_shapes=[pltpu.VMEM((2, n_pages * page, kv_lora), F32),
                        pltpu.VMEM((2, n_pages, rope, page), F32),
                        pltpu.VMEM((n_pages * page, kv_lora), BF16),
                        pltpu.SemaphoreType.DMA((2, 2))],
    )
    return pl.pallas_call(
        functools.partial(_decode_body, layer=layer, n_pages=n_pages, page=page, chunk_pages=chunk_pages,
                          kv_lora=kv_lora, rope=rope, heads=heads, ls=ls),
        out_shape=jax.ShapeDtypeStruct((bs * rows, kv_lora), BF16),
        grid_spec=grid_spec,
        compiler_params=_cparams("arbitrary"),
        name="mla_decode",
    )(page_table, q_ext, lat_new, kr_new, lat_pool, ropet_pool)


MEM_ROWS = 1024
MEM_NB = 4
PROJ_TM = 512
SSD_SAMPLE_NSEQ = 2


def kernel(x_prompt, x_sample, mem_prompt, state_ssm, state_conv, cache_mla_latent, cache_mla_rope_k, cache_mem_k, cache_mem_v, page_table, norm_mix, norm_mem, norm_memkv, norm_ffn, norm_final, ssd_w_in, ssd_conv_w, ssd_conv_b, ssd_dt_bias, ssd_a_log, ssd_d, ssd_norm, ssd_w_out, mla_w_down, mla_q_norm, mla_kv_norm, mla_w_uq, mla_w_uk, mla_w_uv, mla_w_o, mem_w_q, mem_w_kv, mem_w_o, mlp_w_up, mlp_w_down):
    bp, lp, d = x_prompt.shape
    bs, ls, _ = x_sample.shape
    mp, ms = bp * lp, bs * ls
    depth = norm_mix.shape[0]
    n_mem = mem_prompt.shape[1]
    past_len = page_table.shape[1] * cache_mla_latent.shape[2]
    mla_heads = mla_w_uq.shape[2]
    rope = mla_w_uq.shape[3] - QK_NOPE
    ssd_heads = ssd_dt_bias.shape[1]
    d_inner = ssd_heads * SSD_HEADDIM

    x = None
    x_in = (x_prompt.reshape(mp, d), x_sample.reshape(ms, d))
    mem_rows = mem_prompt.reshape(bp * n_mem, d)
    ropet_pool = jnp.swapaxes(cache_mla_rope_k, 2, 3)
    ssm0 = state_ssm.reshape(state_ssm.shape[0], bs, d_inner, D_STATE)

    pos = jnp.concatenate([jnp.arange(lp), jnp.tile(past_len + jnp.arange(ls), bs)])
    cos_tab, sin_tab = _rope_tables(pos, rope)
    proj_tm = _row_tile(ms, PROJ_TM)
    assert lp % proj_tm == 0
    npt, tpl = mp // proj_tm, lp // proj_tm
    tab_index = lambda i: jnp.where(i < npt, i % tpl, tpl + (i - npt))

    mem_rows_p = _row_tile(lp, MEM_ROWS)
    mem_nb = _row_tile(bs, MEM_NB)

    p_ssm, p_conv, p_lat, p_rk, p_mk, p_mv = [], [], [], [], [], []
    s_conv, s_lat, s_rk = [], [], []
    s_ssm = None
    for i in range(depth):
        j = i // 2
        if i % 2 == 0:
            w_pad = _ssd_w_in_pad(ssd_w_in[j]).astype(BF16)
            prm = _ssd_params(ssd_conv_w[j], ssd_conv_b[j], ssd_dt_bias[j], ssd_a_log[j], ssd_d[j], ssd_norm[j])
            x_head, x_tail = x_in if x is None else (x, x[mp:])
            y_p, st_p, ct_p = _ssd_fused(x_head, norm_mix[i], w_pad, *prm, nb=bp, L=lp, T=math.gcd(SSD_CHUNK, lp))
            zx_s = _mm(x_tail, w_pad, g=norm_mix[i], tn=_in_proj_tile(w_pad.shape[1]), name="ssd_in")
            conv0 = jnp.pad(state_conv[j], ((0, 0), (8 - state_conv.shape[2], 0), (0, 0)))
            t_s = math.gcd(SSD_CHUNK, ls)
            y_s, s_ssm, ct_s = _ssd(zx_s, *prm, conv0, ssm0, row_offset=0, nb=bs, L=ls, T=t_s, layer=j,
                                    nseq=math.gcd(SSD_SAMPLE_NSEQ, bs) if ls == t_s else 1,
                                    prev_states=None if s_ssm is None else s_ssm.reshape(-1, bs, d_inner, D_STATE))
            x = _mm(y_p, ssd_w_out[j].astype(BF16), x_tail=y_s, name="ssd_out",
                    **(dict(res=x_head, res_tail=x_tail) if x is None else dict(res=x)))
            kc = state_conv.shape[2]
            p_ssm.append(st_p.reshape(bp, ssd_heads, SSD_HEADDIM, D_STATE))
            p_conv.append(ct_p[:, 8 - kc:, :])
            s_conv.append(ct_s[:, 8 - kc:, :])
        else:
            w = _mla_weights(mla_w_down[j], mla_q_norm[j], mla_kv_norm[j], mla_w_uq[j], mla_w_uk[j], mla_w_uv[j])
            q, qt, kk, vt, lat, kr = _mla_proj(x, norm_mix[i], w, cos_tab, sin_tab, tab_index, tm=proj_tm)
            o_p = _flash(qt, kk, vt, nb=bp, L=lp)
            q_ext = _headmm(q[mp:], w["wabs"]).reshape(ms * mla_heads, -1)
            o_lat = _decode(q_ext, lat, kr, cache_mla_latent, ropet_pool, page_table, layer=j, bs=bs, ls=ls,
                            heads=mla_heads, new_row_offset=mp)
            o_s = _headmm(o_lat.reshape(ms, -1), w["wuv_bd"])
            x = _mm(o_p, mla_w_o[j].astype(BF16), x_tail=o_s, res=x, name="mla_out")
            p_lat.append(lat[:mp].reshape(bp, lp, -1))
            s_lat.append(lat[mp:].reshape(bs, ls, -1))
            p_rk.append(kr[:mp].reshape(bp, lp, -1))
            s_rk.append(kr[mp:].reshape(bs, ls, -1))
        kv = _mm(mem_rows, mem_w_kv[i].astype(BF16), g=norm_memkv[i], name="mem_kv")
        kp = kv[:, :d].reshape(bp, n_mem, d)
        vp = kv[:, d:].reshape(bp, n_mem, d)
        p_mk.append(kp.reshape(bp, n_mem, MEM_HEADS, d // MEM_HEADS))
        p_mv.append(vp.reshape(bp, n_mem, MEM_HEADS, d // MEM_HEADS))
        wq, wo = mem_w_q[i].astype(BF16), mem_w_o[i].astype(BF16)
        x_p = _mem_fused(x, norm_mem[i], wq, kp.astype(BF16), vp.astype(BF16), wo, n_seq=bp, L=lp, rows=mem_rows_p)
        x_s = x[mp:]
        qm_s = _mm(x_s, wq, g=norm_mem[i], out_dtype=BF16, name="mem_q")
        o_s = _memattn_dec(qm_s, cache_mem_k, cache_mem_v, layer=i, row_offset=0, n_seq=bs, rows=ls, nb=mem_nb)
        x_s = _mm(o_s, wo, res=x_s, name="mem_out")
        x = _mlp(x_p, norm_ffn[i], mlp_w_up[i].astype(BF16), mlp_w_down[i].astype(BF16), x_tail=x_s,
                 final_g=norm_final if i == depth - 1 else None)
    y_p, y_s = x
    return (y_p.reshape(bp, lp, d), y_s.reshape(bs, ls, d),
            jnp.stack(p_ssm), jnp.stack(p_conv), jnp.stack(p_lat), jnp.stack(p_rk), jnp.stack(p_mk), jnp.stack(p_mv),
            s_ssm.reshape(state_ssm.shape), jnp.stack(s_conv), jnp.stack(s_lat), jnp.stack(s_rk))
```

```python
import functools
import math

import jax
import jax.numpy as jnp
from jax import lax
from jax.experimental import pallas as pl
from jax.experimental.pallas import tpu as pltpu

F32 = jnp.float32
BF16 = jnp.bfloat16

EPS = 1e-6
ROPE_THETA = 10000.0

V7X_LANES = 128
V7X_MXU_COLS = 256
V7X_VMEM_BYTES = 64 * 1024 * 1024
V7X_VMEM_LIMIT_BYTES = V7X_VMEM_BYTES * 7 // 8

SSD_HEADDIM = 64
SSD_GROUPS = 8
D_STATE = 128
SSD_CHUNK = 128
QK_NOPE = 64
V_HEAD = 64
MEM_HEADS = 4

ROW_TILE = 1024
COL_TILE = 1024
IN_PROJ_MAX_TILE = 1536
ATTN_TILE = 512
DECODE_CHUNK_PAGES = 16
MEM_DEC_SEQS = 4
SSD_SAMPLE_SEQS = 2


def _cparams(*sem):
    return pltpu.CompilerParams(dimension_semantics=sem, vmem_limit_bytes=V7X_VMEM_LIMIT_BYTES)


def _rms(x, g):
    return x * lax.rsqrt(jnp.mean(x * x, axis=-1, keepdims=True) + EPS) * g


def _row_tile(m, pref):
    t = min(pref, m)
    assert m % t == 0, (m, t)
    return t


def _mm_body(*refs, norm, res, res_split, head_tiles):
    it = iter(refs)
    x_ref = next(it)
    t_ref = next(it) if head_tiles is not None else None
    g_ref = next(it) if norm else None
    w_ref = next(it)
    r_ref = next(it) if res else None
    rt_ref = next(it) if res_split else None
    o_ref = next(it)
    xn_ref = next(it)

    def stage(src_ref):
        x = src_ref[...].astype(F32)
        if norm:
            x = _rms(x, g_ref[...])
        xn_ref[...] = x.astype(BF16)

    first = pl.program_id(1) == 0
    if head_tiles is None:
        pl.when(first)(lambda: stage(x_ref))
    else:
        in_head = pl.program_id(0) < head_tiles
        pl.when(first & in_head)(lambda: stage(x_ref))
        pl.when(first & jnp.logical_not(in_head))(lambda: stage(t_ref))

    acc = jnp.dot(xn_ref[...], w_ref[...], preferred_element_type=F32)
    if res_split:
        in_head = pl.program_id(0) < head_tiles

        @pl.when(in_head)
        def _():
            o_ref[...] = (r_ref[...] + acc).astype(o_ref.dtype)

        @pl.when(jnp.logical_not(in_head))
        def _():
            o_ref[...] = (rt_ref[...] + acc).astype(o_ref.dtype)
    else:
        if res:
            acc = r_ref[...] + acc
        o_ref[...] = acc.astype(o_ref.dtype)


def _mm(x, w, *, x_tail=None, g=None, res=None, res_tail=None, out_dtype=F32, tm=ROW_TILE, tn=COL_TILE, name="mm"):
    m, k = x.shape
    k2, n = w.shape
    assert k == k2
    assert res_tail is None or (x_tail is not None and res is not None
                                and res.shape[0] == m and res_tail.shape[0] == x_tail.shape[0])
    head_tiles = None
    if x_tail is None:
        tm = _row_tile(m, tm)
        in_specs = [pl.BlockSpec((tm, k), lambda i, j: (i, 0))]
        args = [x]
    else:
        m_tail = x_tail.shape[0]
        tm = math.gcd(math.gcd(m, m_tail), tm)
        head_tiles = m // tm
        m = m + m_tail
        in_specs = [pl.BlockSpec((tm, k), lambda i, j: (jnp.minimum(i, head_tiles - 1), 0)),
                    pl.BlockSpec((tm, k), lambda i, j: (jnp.maximum(i - head_tiles, 0), 0))]
        args = [x, x_tail]
    tn = _row_tile(n, tn)
    if g is not None:
        in_specs.append(pl.BlockSpec((1, k), lambda i, j: (0, 0)))
        args.append(g.reshape(1, k).astype(F32))
    in_specs.append(pl.BlockSpec((k, tn), lambda i, j: (0, j)))
    args.append(w)
    if res_tail is not None:
        in_specs += [pl.BlockSpec((tm, tn), lambda i, j: (jnp.minimum(i, head_tiles - 1), j)),
                     pl.BlockSpec((tm, tn), lambda i, j: (jnp.maximum(i - head_tiles, 0), j))]
        args += [res, res_tail]
    elif res is not None:
        in_specs.append(pl.BlockSpec((tm, tn), lambda i, j: (i, j)))
        args.append(res)
    return pl.pallas_call(
        functools.partial(_mm_body, norm=g is not None, res=res is not None, res_split=res_tail is not None,
                          head_tiles=head_tiles),
        out_shape=jax.ShapeDtypeStruct((m, n), out_dtype),
        grid=(m // tm, n // tn),
        in_specs=in_specs,
        out_specs=pl.BlockSpec((tm, tn), lambda i, j: (i, j)),
        scratch_shapes=[pltpu.VMEM((tm, k), BF16)],
        compiler_params=_cparams("parallel", "arbitrary"),
        name=name,
    )(*args)


def _mlp_body(*refs, head_tiles, final_norm):
    x_ref = refs[0]
    t_ref = refs[1] if head_tiles is not None else None
    refs = refs[1 + int(head_tiles is not None):]
    g_ref, wu_ref, wd_ref = refs[:3]
    fg_ref = refs[3] if final_norm else None
    refs = refs[3 + int(final_norm):]
    o_ref = refs[0]
    ot_ref = refs[1] if final_norm else None
    xn_ref, acc_ref = refs[1 + int(final_norm):]
    k = pl.program_id(1)

    def stage(src_ref):
        x = src_ref[...]
        xn_ref[...] = _rms(x, g_ref[...]).astype(BF16)
        acc_ref[...] = x

    if head_tiles is None:
        pl.when(k == 0)(lambda: stage(x_ref))
    else:
        in_head = pl.program_id(0) < head_tiles
        pl.when((k == 0) & in_head)(lambda: stage(x_ref))
        pl.when((k == 0) & jnp.logical_not(in_head))(lambda: stage(t_ref))

    h = jnp.dot(xn_ref[...], wu_ref[...], preferred_element_type=F32)
    h = jnp.square(jnp.maximum(h, 0.0)).astype(BF16)
    acc_ref[...] += jnp.dot(h, wd_ref[...], preferred_element_type=F32)

    last = k == pl.num_programs(1) - 1
    if not final_norm:
        @pl.when(last)
        def _():
            o_ref[...] = acc_ref[...]
    else:
        in_head = pl.program_id(0) < head_tiles

        @pl.when(last & in_head)
        def _():
            o_ref[...] = _rms(acc_ref[...], fg_ref[...])

        @pl.when(last & jnp.logical_not(in_head))
        def _():
            ot_ref[...] = _rms(acc_ref[...], fg_ref[...])


def _mlp(x, g, w_up, w_down, *, x_tail=None, final_g=None, tm=ROW_TILE, tf=COL_TILE):
    m, d = x.shape
    ff = w_up.shape[1]
    head_tiles = None
    assert final_g is None or x_tail is not None
    if x_tail is None:
        tm = _row_tile(m, tm)
        row_specs, row_args = [pl.BlockSpec((tm, d), lambda i, k: (i, 0))], [x]
    else:
        m_tail = x_tail.shape[0]
        tm = math.gcd(math.gcd(m, m_tail), tm)
        head_tiles = m // tm
        m = m + m_tail
        row_specs = [pl.BlockSpec((tm, d), lambda i, k: (jnp.minimum(i, head_tiles - 1), 0)),
                     pl.BlockSpec((tm, d), lambda i, k: (jnp.maximum(i - head_tiles, 0), 0))]
        row_args = [x, x_tail]
    tf = _row_tile(ff, tf)
    if final_g is None:
        out_shape = jax.ShapeDtypeStruct((m, d), F32)
        out_specs = pl.BlockSpec((tm, d), lambda i, k: (i, 0))
        fg_specs, fg_args = [], []
    else:
        out_shape = (jax.ShapeDtypeStruct(x.shape, F32), jax.ShapeDtypeStruct(x_tail.shape, F32))
        out_specs = tuple(row_specs)
        fg_specs, fg_args = [pl.BlockSpec((1, d), lambda i, k: (0, 0))], [final_g.reshape(1, d).astype(F32)]
    return pl.pallas_call(
        functools.partial(_mlp_body, head_tiles=head_tiles, final_norm=final_g is not None),
        out_shape=out_shape,
        grid=(m // tm, ff // tf),
        in_specs=row_specs + [
            pl.BlockSpec((1, d), lambda i, k: (0, 0)),
            pl.BlockSpec((d, tf), lambda i, k: (0, k)),
            pl.BlockSpec((tf, d), lambda i, k: (k, 0)),
        ] + fg_specs,
        out_specs=out_specs,
        scratch_shapes=[pltpu.VMEM((tm, d), BF16), pltpu.VMEM((tm, d), F32)],
        compiler_params=_cparams("parallel" if final_g is None else "arbitrary", "arbitrary"),
        name="mlp",
    )(*row_args, g.reshape(1, d).astype(F32), w_up, w_down, *fg_args)


def _mem_fused_body(x_ref, g_ref, wq_ref, k_ref, v_ref, wo_ref, o_ref, *, heads, scale):
    x = x_ref[...]
    q = jnp.dot(_rms(x, g_ref[...]).astype(BF16), wq_ref[...], preferred_element_type=F32)
    dh = q.shape[-1] // heads
    outs = []
    for h in range(heads):
        q_h = q[:, h * dh:(h + 1) * dh].astype(BF16)
        k_h = k_ref[:, h * dh:(h + 1) * dh]
        v_h = v_ref[:, h * dh:(h + 1) * dh]
        s = lax.dot_general(q_h, k_h, (((1,), (1,)), ((), ())), preferred_element_type=F32) * scale
        s = s - jnp.max(s, axis=-1, keepdims=True)
        p = jnp.exp(s)
        p = p / jnp.sum(p, axis=-1, keepdims=True)
        outs.append(jnp.dot(p.astype(BF16), v_h, preferred_element_type=F32).astype(BF16))
    o = jnp.concatenate(outs, axis=1)
    o_ref[...] = x + jnp.dot(o, wo_ref[...], preferred_element_type=F32)


def _mem_fused(x, g, wq, k, v, wo, *, n_seq, L, rows):
    d = x.shape[-1]
    n_mem = k.shape[-2]
    per_seq = L // rows
    const = lambda s: (0, 0)
    weight = lambda: pl.BlockSpec((d, d), const, pipeline_mode=pl.Buffered(1))
    return pl.pallas_call(
        functools.partial(_mem_fused_body, heads=MEM_HEADS, scale=1.0 / math.sqrt(d // MEM_HEADS)),
        out_shape=jax.ShapeDtypeStruct((n_seq * L, d), F32),
        grid=(n_seq * per_seq,),
        in_specs=[pl.BlockSpec((rows, d), lambda s: (s, 0)),
                  pl.BlockSpec((1, d), const),
                  weight(),
                  pl.BlockSpec((None, n_mem, d), lambda s: (s // per_seq, 0, 0)),
                  pl.BlockSpec((None, n_mem, d), lambda s: (s // per_seq, 0, 0)),
                  weight()],
        out_specs=pl.BlockSpec((rows, d), lambda s: (s, 0)),
        compiler_params=_cparams("parallel"),
        name="mem_fused",
    )(x, g.reshape(1, d).astype(F32), wq, k, v, wo)


def _memattn_dec_body(q_ref, k_ref, v_ref, o_ref, *, nb, rows, heads, scale):
    dh = q_ref.shape[-1] // heads
    n_mem = k_ref.shape[1]
    qrow_head = lax.broadcasted_iota(jnp.int32, (heads * rows, n_mem * heads), 0) // rows
    key_head = lax.broadcasted_iota(jnp.int32, (heads * rows, n_mem * heads), 1) % heads
    own = qrow_head == key_head
    outs = []
    q_all = q_ref[...].astype(F32)
    for i in range(nb):
        qi = q_all[i * rows:(i + 1) * rows, :]
        qs = jnp.concatenate([qi[:, h * dh:(h + 1) * dh] for h in range(heads)], axis=0).astype(BF16)
        k2 = k_ref[i].reshape(n_mem * heads, dh).astype(BF16)
        v2 = v_ref[i].reshape(n_mem * heads, dh).astype(BF16)
        s = lax.dot_general(qs, k2, (((1,), (1,)), ((), ())), preferred_element_type=F32) * scale
        s = jnp.where(own, s, -jnp.inf)
        s = s - jnp.max(s, axis=-1, keepdims=True)
        p = jnp.exp(s)
        p = p / jnp.sum(p, axis=-1, keepdims=True)
        o = jnp.dot(p.astype(BF16), v2, preferred_element_type=F32)
        outs.append(jnp.concatenate([o[h * rows:(h + 1) * rows, :] for h in range(heads)], axis=1))
    o_ref[...] = jnp.concatenate(outs, axis=0).astype(o_ref.dtype)


def _memattn_dec(q, k, v, *, layer, row_offset, n_seq, rows, nb):
    d = q.shape[-1]
    _, _, n_mem, heads, dh = k.shape
    blk = nb * rows
    assert row_offset % blk == 0 and n_seq % nb == 0
    kv_spec = pl.BlockSpec((None, nb, n_mem, heads, dh), lambda s: (layer, s, 0, 0, 0))
    return pl.pallas_call(
        functools.partial(_memattn_dec_body, nb=nb, rows=rows, heads=heads, scale=1.0 / math.sqrt(dh)),
        out_shape=jax.ShapeDtypeStruct((n_seq * rows, d), BF16),
        grid=(n_seq // nb,),
        in_specs=[pl.BlockSpec((blk, d), lambda s: (row_offset // blk + s, 0)), kv_spec, kv_spec],
        out_specs=pl.BlockSpec((blk, d), lambda s: (s, 0)),
        compiler_params=_cparams("parallel"),
        name="memattn_dec",
    )(q, k, v)


def _split3_lanes(v, heads):
    lane = lax.broadcasted_iota(jnp.int32, v.shape, 1)
    hi = v.astype(BF16).astype(F32)
    r1 = v - hi
    mid = r1.astype(BF16).astype(F32)
    lo = r1 - mid
    out = jnp.where(lane < heads, hi, jnp.where(lane < 2 * heads, mid, jnp.where(lane < 3 * heads, lo, 0.0)))
    return out.astype(BF16)


def _ssd_init(conv0_ref, s0_ref, extx_ref, extbc_ref, st_ref):
    d_inner = extx_ref.shape[-1]
    if conv0_ref is not None:
        extx_ref[0:8, :] = conv0_ref[:, 0:d_inner]
        extbc_ref[0:8, :] = conv0_ref[:, d_inner:]
        st_ref[...] = s0_ref[...].T
    else:
        extx_ref[0:8, :] = jnp.zeros((8, d_inner), F32)
        extbc_ref[0:8, :] = jnp.zeros((8, extbc_ref.shape[-1]), F32)
        st_ref[...] = jnp.zeros(st_ref.shape, F32)


def _ssd_finish(sfin_ref, ctail_ref, extx_ref, extbc_ref, st_ref):
    d_inner = extx_ref.shape[-1]
    sfin_ref[...] = st_ref[...].T
    ctail_ref[:, 0:d_inner] = extx_ref[0:8, :]
    ctail_ref[:, d_inner:] = extbc_ref[0:8, :]


def _scan_chunk(z, x_raw, bc_raw, dt_raw, cw_ref, cb_ref, dtb_ref, alog_ref, dsk_ref, ng_ref, e3_ref,
                y_ref, y_row0, extx_ref, extbc_ref, st_ref, *, T, heads, groups):
    d_inner = extx_ref.shape[-1]
    hpg = heads // groups
    gw = d_inner // groups
    hd = d_inner // heads
    n = D_STATE
    gate = z * jax.nn.sigmoid(z)

    extx_ref[8:8 + T, :] = x_raw
    extbc_ref[8:8 + T, :] = bc_raw

    def conv(ext_ref, lo, width):
        cur = ext_ref[8:8 + T, :]
        tail = ext_ref[0:8, :]
        r8 = lax.broadcasted_iota(jnp.int32, (8, width), 0)
        kw = cw_ref.shape[0]
        acc = cb_ref[:, lo:lo + width] + cur * cw_ref[kw - 1:kw, lo:lo + width]
        for s in range(1, kw):
            rolled = pltpu.roll(cur, s, axis=0)
            head = jnp.where(r8 < s, pltpu.roll(tail, s, axis=0), rolled[0:8])
            shifted = head if T == 8 else jnp.concatenate([head, rolled[8:]], axis=0)
            acc = acc + shifted * cw_ref[kw - 1 - s:kw - s, lo:lo + width]
        return acc * jax.nn.sigmoid(acc)

    xc = conv(extx_ref, 0, d_inner)
    bcc = conv(extbc_ref, d_inner, extbc_ref.shape[-1])
    extx_ref[0:8, :] = extx_ref[T:T + 8, :]
    extbc_ref[0:8, :] = extbc_ref[T:T + 8, :]

    dtr = dt_raw + dtb_ref[...]
    dtv = jnp.maximum(dtr, 0.0) + jnp.log1p(jnp.exp(-jnp.abs(dtr)))
    a = dtv * (-jnp.exp(alog_ref[...]))
    row = lax.broadcasted_iota(jnp.int32, a.shape, 0)
    acs = a
    sh = 1
    while sh < T:
        acs = acs + jnp.where(row >= sh, pltpu.roll(acs, sh, axis=0), 0.0)
        sh *= 2
    a_last = acs[T - 1:T, :]
    e3 = e3_ref[...]

    per_head = [dtv, jnp.exp(a_last - acs), jnp.exp(acs)]
    if T % V7X_LANES == 0:
        stacked = jnp.concatenate([_split3_lanes(v, heads) for v in per_head], axis=0)
        expanded = jnp.dot(stacked, e3, preferred_element_type=F32)
        dt_e, dend_e, eacs_e = expanded[0:T], expanded[T:2 * T], expanded[2 * T:3 * T]
    else:
        dt_e, dend_e, eacs_e = [jnp.dot(_split3_lanes(v, heads), e3, preferred_element_type=F32) for v in per_head]
    xdt = xc * dt_e
    xdt_b = xdt.astype(BF16)
    xw_b = (xdt * dend_e).astype(BF16)
    cdec_e = eacs_e[T - 1:T, :]

    acs2 = acs * LOG2E
    if T < V7X_LANES:
        acs_p = jnp.concatenate([acs2, jnp.zeros((V7X_LANES - T, V7X_LANES), F32)], axis=0)
    else:
        acs_p = acs2
    acs_t = acs_p.T
    ti = lax.broadcasted_iota(jnp.int32, (T, T), 0)
    si = lax.broadcasted_iota(jnp.int32, (T, T), 1)
    causal = ti >= si
    lane_g = lax.broadcasted_iota(jnp.int32, (T, gw), 1)

    for g in range(groups):
        bg = bcc[:, g * n:(g + 1) * n]
        cg = bcc[:, groups * n + g * n: groups * n + (g + 1) * n].astype(BF16)
        bg_b = bg.astype(BF16)
        gmat = lax.dot_general(cg, bg_b, (((1,), (1,)), ((), ())), preferred_element_type=F32)
        st_g = st_ref[:, g * gw:(g + 1) * gw]
        y_g = jnp.dot(cg, st_g.astype(BF16), preferred_element_type=F32) * eacs_e[:, g * gw:(g + 1) * gw]
        xg = xdt_b[:, g * gw:(g + 1) * gw]
        m_heads, x_heads = [], []
        for j in range(hpg):
            h = g * hpg + j
            seg = acs2[:, h:h + 1] - acs_t[h:h + 1, 0:T]
            decay = jnp.exp2(jnp.where(causal, seg, -jnp.inf))
            m_heads.append((gmat * decay).astype(BF16))
            x_heads.append(jnp.where((lane_g >= j * hd) & (lane_g < (j + 1) * hd), xg, jnp.zeros_like(xg)))
        if T % V7X_LANES == 0:
            y_g = y_g + jnp.dot(jnp.concatenate(m_heads, axis=1), jnp.concatenate(x_heads, axis=0),
                                preferred_element_type=F32)
        else:
            for m_h, x_h in zip(m_heads, x_heads):
                y_g = y_g + jnp.dot(m_h, x_h, preferred_element_type=F32)
        cs_t = jnp.dot(bg.T.astype(BF16), xw_b[:, g * gw:(g + 1) * gw], preferred_element_type=F32)
        st_ref[:, g * gw:(g + 1) * gw] = st_g * cdec_e[:, g * gw:(g + 1) * gw] + cs_t
        y_g = y_g + xc[:, g * gw:(g + 1) * gw] * dsk_ref[:, g * gw:(g + 1) * gw]
        y_g = y_g * gate[:, g * gw:(g + 1) * gw]
        y_g = _rms(y_g, ng_ref[:, g * gw:(g + 1) * gw])
        y_ref[y_row0:y_row0 + T, g * gw:(g + 1) * gw] = y_g.astype(y_ref.dtype)


def _ssd_body(*refs, T, heads, groups, nseq, has_prev):
    (z_ref, x_ref, bc_ref, dt_ref, cw_ref, cb_ref, dtb_ref, alog_ref, dsk_ref, ng_ref, e3_ref,
     conv0_ref, s0_ref) = refs[:13]
    prev_ref = refs[13] if has_prev else None
    y_ref, sfin_ref, ctail_ref, extx_ref, extbc_ref, st_ref = refs[13 + int(has_prev):]
    c = pl.program_id(1)

    @pl.when(c == 0)
    def _():
        for i in range(nseq):
            _ssd_init(conv0_ref.at[i], s0_ref.at[i], extx_ref.at[i], extbc_ref.at[i], st_ref.at[i])

    for i in range(nseq):
        rows = slice(i * T, (i + 1) * T)
        _scan_chunk(z_ref[rows, :], x_ref[rows, :], bc_ref[rows, :], dt_ref[rows, :],
                    cw_ref, cb_ref, dtb_ref, alog_ref, dsk_ref, ng_ref, e3_ref,
                    y_ref, i * T, extx_ref.at[i], extbc_ref.at[i], st_ref.at[i], T=T, heads=heads, groups=groups)

    @pl.when(c == pl.num_programs(1) - 1)
    def _():
        for i in range(nseq):
            if has_prev:
                n_prev = prev_ref.shape[0]
                sfin_ref[0:n_prev, i] = prev_ref[:, i]
                sfin_i = sfin_ref.at[n_prev, i]
            else:
                sfin_i = sfin_ref.at[i]
            _ssd_finish(sfin_i, ctail_ref.at[i], extx_ref.at[i], extbc_ref.at[i], st_ref.at[i])


def _ssd_fused_body(xcur_ref, xnext_ref, g_ref, win_ref, cw_ref, cb_ref, dtb_ref, alog_ref, dsk_ref, ng_ref, e3_ref,
                    y_ref, sfin_ref, ctail_ref,
                    extx_ref, extbc_ref, st_ref, zxa_ref, zxb_ref, *, T, heads, groups):
    k = pl.program_id(1)
    d_inner = extx_ref.shape[-1]
    bcw = extbc_ref.shape[-1]

    def in_proj(rows):
        xn = _rms(rows, g_ref[...]).astype(BF16)
        return jnp.dot(xn, win_ref[...], preferred_element_type=F32)

    def scan(zx_ref, y_row0):
        _scan_chunk(zx_ref[:, 0:d_inner], zx_ref[:, d_inner:2 * d_inner],
                    zx_ref[:, 2 * d_inner:2 * d_inner + bcw],
                    zx_ref[:, 2 * d_inner + bcw:2 * d_inner + bcw + V7X_LANES],
                    cw_ref, cb_ref, dtb_ref, alog_ref, dsk_ref, ng_ref, e3_ref,
                    y_ref, y_row0, extx_ref, extbc_ref, st_ref, T=T, heads=heads, groups=groups)

    @pl.when(k == 0)
    def _():
        _ssd_init(None, None, extx_ref, extbc_ref, st_ref)
        zxa_ref[...] = in_proj(xcur_ref[0:T, :])

    zxb_ref[...] = in_proj(xcur_ref[T:2 * T, :])
    scan(zxa_ref, 0)
    zxa_ref[...] = in_proj(xnext_ref[0:T, :])
    scan(zxb_ref, T)
    pl.when(k == pl.num_programs(1) - 1)(
        functools.partial(_ssd_finish, sfin_ref.at[0], ctail_ref.at[0], extx_ref, extbc_ref, st_ref))


def _ssd(zx, conv_w, conv_b, dt_bias_rep, a_log_rep, d_exp, norm_g, e3, conv0, s0, *, row_offset, nb, L, T,
         layer, nseq=1, prev_states=None):
    heads = d_exp.shape[-1] // SSD_HEADDIM
    d_inner = d_exp.shape[-1]
    bcw = 2 * SSD_GROUPS * D_STATE
    assert bcw == d_inner, "column blocks are indexed in units of d_inner"
    nc = L // T
    rows = nseq * T
    assert nb % nseq == 0 and row_offset % rows == 0 and (nseq == 1 or nc == 1)
    rb0 = row_offset // rows
    rowmap = lambda col: (lambda b, c: (rb0 + b * nc + c, col))
    const = lambda b, c: (0, 0)
    per_b = lambda b, c: (b, 0, 0)
    dt_col = (2 * d_inner + bcw) // V7X_LANES
    if prev_states is None:
        prev_args, prev_specs = [], []
        sfin_shape = (nb, d_inner, D_STATE)
        sfin_spec = pl.BlockSpec((nseq, d_inner, D_STATE), per_b)
    else:
        n_prev = prev_states.shape[0]
        prev_args = [prev_states]
        prev_specs = [pl.BlockSpec((n_prev, nseq, d_inner, D_STATE), lambda b, c: (0, b, 0, 0))]
        sfin_shape = (n_prev + 1, nb, d_inner, D_STATE)
        sfin_spec = pl.BlockSpec((n_prev + 1, nseq, d_inner, D_STATE), lambda b, c: (0, b, 0, 0))
    y, sfin, ctail = pl.pallas_call(
        functools.partial(_ssd_body, T=T, heads=heads, groups=SSD_GROUPS, nseq=nseq,
                          has_prev=prev_states is not None),
        out_shape=(jax.ShapeDtypeStruct((nb * L, d_inner), BF16),
                   jax.ShapeDtypeStruct(sfin_shape, F32),
                   jax.ShapeDtypeStruct((nb, 8, d_inner + bcw), F32)),
        grid=(nb // nseq, nc),
        in_specs=[
            pl.BlockSpec((rows, d_inner), rowmap(0)),
            pl.BlockSpec((rows, d_inner), rowmap(1)),
            pl.BlockSpec((rows, bcw), rowmap(2)),
            pl.BlockSpec((rows, V7X_LANES), rowmap(dt_col)),
            pl.BlockSpec((4, d_inner + bcw), const),
            pl.BlockSpec((1, d_inner + bcw), const),
            pl.BlockSpec((1, V7X_LANES), const),
            pl.BlockSpec((1, V7X_LANES), const),
            pl.BlockSpec((1, d_inner), const),
            pl.BlockSpec((1, d_inner), const),
            pl.BlockSpec((V7X_LANES, d_inner), const),
            pl.BlockSpec((nseq, 8, d_inner + bcw), per_b),
            pl.BlockSpec((None, nseq, d_inner, D_STATE), lambda b, c: (layer, b, 0, 0)),
        ] + prev_specs,
        out_specs=(pl.BlockSpec((rows, d_inner), lambda b, c: (b * nc + c, 0)),
                   sfin_spec,
                   pl.BlockSpec((nseq, 8, d_inner + bcw), per_b)),
        scratch_shapes=[pltpu.VMEM((nseq, T + 8, d_inner), F32), pltpu.VMEM((nseq, T + 8, bcw), F32),
                        pltpu.VMEM((nseq, D_STATE, d_inner), F32)],
        compiler_params=_cparams("parallel", "arbitrary"),
        name="ssd",
    )(zx, zx, zx, zx, conv_w, conv_b, dt_bias_rep, a_log_rep, d_exp, norm_g, e3, conv0, s0, *prev_args)
    return y, sfin, ctail


def _ssd_fused(x, g, w_in, conv_w, conv_b, dt_bias_rep, a_log_rep, d_exp, norm_g, e3, *, nb, L, T):
    d = x.shape[1]
    heads = d_exp.shape[-1] // SSD_HEADDIM
    d_inner = d_exp.shape[-1]
    bcw = 2 * SSD_GROUPS * D_STATE
    n_in = w_in.shape[1]
    assert L % (2 * T) == 0
    nk = L // (2 * T)
    const = lambda b, k: (0, 0)
    per_b = lambda b, k: (b, 0, 0)
    return pl.pallas_call(
        functools.partial(_ssd_fused_body, T=T, heads=heads, groups=SSD_GROUPS),
        out_shape=(jax.ShapeDtypeStruct((nb * L, d_inner), BF16),
                   jax.ShapeDtypeStruct((nb, d_inner, D_STATE), F32),
                   jax.ShapeDtypeStruct((nb, 8, d_inner + bcw), F32)),
        grid=(nb, nk),
        in_specs=[
            pl.BlockSpec((2 * T, d), lambda b, k: (b * nk + k, 0)),
            pl.BlockSpec((2 * T, d), lambda b, k: (b * nk + jnp.minimum(k + 1, nk - 1), 0)),
            pl.BlockSpec((1, d), const),
            pl.BlockSpec((d, n_in), const, pipeline_mode=pl.Buffered(1)),
            pl.BlockSpec((4, d_inner + bcw), const),
            pl.BlockSpec((1, d_inner + bcw), const),
            pl.BlockSpec((1, V7X_LANES), const),
            pl.BlockSpec((1, V7X_LANES), const),
            pl.BlockSpec((1, d_inner), const),
            pl.BlockSpec((1, d_inner), const),
            pl.BlockSpec((V7X_LANES, d_inner), const),
        ],
        out_specs=(pl.BlockSpec((2 * T, d_inner), lambda b, k: (b * nk + k, 0)),
                   pl.BlockSpec((1, d_inner, D_STATE), per_b),
                   pl.BlockSpec((1, 8, d_inner + bcw), per_b)),
        scratch_shapes=[pltpu.VMEM((T + 8, d_inner), F32), pltpu.VMEM((T + 8, bcw), F32),
                        pltpu.VMEM((D_STATE, d_inner), F32),
                        pltpu.VMEM((T, n_in), F32), pltpu.VMEM((T, n_in), F32)],
        compiler_params=_cparams("parallel", "arbitrary"),
        name="ssd_fused",
    )(x, x, g.reshape(1, d).astype(F32), w_in, conv_w, conv_b, dt_bias_rep, a_log_rep, d_exp, norm_g, e3)


def _ssd_w_in_pad(w_in):
    heads = w_in.shape[1] - (w_in.shape[1] // V7X_LANES) * V7X_LANES
    main = w_in[:, :w_in.shape[1] - heads]
    dt = w_in[:, w_in.shape[1] - heads:]
    return jnp.concatenate([main] + [dt] * (V7X_MXU_COLS // heads), axis=1)


def _in_proj_tile(n):
    return max(t for t in range(V7X_MXU_COLS, IN_PROJ_MAX_TILE + 1, V7X_MXU_COLS) if n % t == 0)


def _ssd_params(conv_w, conv_b, dt_bias, a_log, d_skip, norm_g):
    heads = dt_bias.shape[0]
    rep = V7X_LANES // heads
    d_inner = heads * SSD_HEADDIM
    src = jnp.arange(V7X_LANES)[:, None]
    dst_head = jnp.arange(d_inner)[None, :] // SSD_HEADDIM
    e3 = ((src % heads == dst_head) & (src < 3 * heads)).astype(BF16)
    return (conv_w.astype(F32), conv_b.reshape(1, -1).astype(F32),
            jnp.tile(dt_bias.reshape(1, heads), (1, rep)).astype(F32),
            jnp.tile(a_log.reshape(1, heads), (1, rep)).astype(F32),
            jnp.repeat(d_skip, SSD_HEADDIM).reshape(1, d_inner).astype(F32),
            norm_g.reshape(1, d_inner).astype(F32), e3)


SLOT = V7X_LANES
VT_SLOT = V_HEAD + 16
LOG2E = math.log2(math.e)
FLASH_SCORES_AHEAD = 2


def _mla_proj_body(x_ref, g_ref, wd_ref, qn_ref, kvn_ref, wuq_ref, wuqr_ref, wuk_ref, wuvt_ref, cos_ref, sin_ref,
                   q_ref, qt_ref, kk_ref, vt_ref, lat_ref, kr_ref, *, heads, q_lora, kv_lora, rope, scale):
    xn = _rms(x_ref[...], g_ref[...]).astype(BF16)
    down = jnp.dot(xn, wd_ref[...], preferred_element_type=F32)
    cq = down[:, :q_lora]
    ckv = down[:, q_lora:q_lora + kv_lora]
    krs = down[:, q_lora + kv_lora:q_lora + kv_lora + SLOT]
    krr = down[:, q_lora + kv_lora + SLOT:]
    cos = cos_ref[...]
    sin = sin_ref[...]
    cqn = _rms(cq, qn_ref[...]).astype(BF16)
    qp = jnp.dot(cqn, wuq_ref[...], preferred_element_type=F32)
    qr = jnp.dot(cqn, wuqr_ref[...], preferred_element_type=F32)
    lat = _rms(ckv, kvn_ref[...])
    lat_ref[...] = lat
    kr_rot = krs * cos + krr * sin
    kr_ref[...] = kr_rot[:, QK_NOPE:QK_NOPE + rope]
    latb = lat.astype(BF16)
    kn = jnp.dot(latb, wuk_ref[...], preferred_element_type=F32)
    for h in range(heads):
        sl = slice(h * SLOT, (h + 1) * SLOT)
        q_h = (qp[:, sl] * cos + qr[:, sl] * sin) * scale
        q_ref[:, sl] = q_h.astype(BF16)
        qt_ref[sl, :] = q_h.T.astype(BF16)
        kk_ref[:, sl] = (kn[:, sl] + kr_rot).astype(BF16)
    vt = lax.dot_general(wuvt_ref[...], latb, (((1,), (1,)), ((), ())),
                         preferred_element_type=F32)
    fill_rows = lax.broadcasted_iota(jnp.int32, (VT_SLOT - V_HEAD, vt.shape[1]), 0)
    ones_then_zeros = jnp.where(fill_rows == 0, 1.0, 0.0).astype(BF16)
    for h in range(heads):
        vt_ref[h * VT_SLOT:h * VT_SLOT + V_HEAD, :] = vt[h * V_HEAD:(h + 1) * V_HEAD, :].astype(BF16)
        vt_ref[h * VT_SLOT + V_HEAD:(h + 1) * VT_SLOT, :] = ones_then_zeros


def _mla_proj(x, g, w, cos_tab, sin_tab, tab_index, *, tm=ATTN_TILE):
    m, d = x.shape
    heads = w["wuq"].shape[1] // SLOT
    q_lora = w["qn"].shape[-1]
    kv_lora = w["kvn"].shape[-1]
    rope = w["rope"]
    tm = _row_tile(m, tm)
    const = lambda i: (0, 0)
    rowb = lambda i: (i, 0)
    full = lambda a: pl.BlockSpec(a.shape, const)
    return pl.pallas_call(
        functools.partial(_mla_proj_body, heads=heads, q_lora=q_lora, kv_lora=kv_lora, rope=rope,
                          scale=w["scale"]),
        out_shape=(jax.ShapeDtypeStruct((m, heads * SLOT), BF16),
                   jax.ShapeDtypeStruct((heads * SLOT, m), BF16),
                   jax.ShapeDtypeStruct((m, heads * SLOT), BF16),
                   jax.ShapeDtypeStruct((heads * VT_SLOT, m), BF16),
                   jax.ShapeDtypeStruct((m, kv_lora), F32),
                   jax.ShapeDtypeStruct((m, rope), F32)),
        grid=(m // tm,),
        in_specs=[pl.BlockSpec((tm, d), rowb), pl.BlockSpec((1, d), const),
                  full(w["wd"]), full(w["qn"]), full(w["kvn"]), full(w["wuq"]), full(w["wuqr"]),
                  full(w["wuk"]), full(w["wuvt"]),
                  pl.BlockSpec((tm, SLOT), lambda i: (tab_index(i), 0)),
                  pl.BlockSpec((tm, SLOT), lambda i: (tab_index(i), 0))],
        out_specs=(pl.BlockSpec((tm, heads * SLOT), rowb), pl.BlockSpec((heads * SLOT, tm), lambda i: (0, i)),
                   pl.BlockSpec((tm, heads * SLOT), rowb), pl.BlockSpec((heads * VT_SLOT, tm), lambda i: (0, i)),
                   pl.BlockSpec((tm, kv_lora), rowb), pl.BlockSpec((tm, rope), rowb)),
        compiler_params=_cparams("parallel"),
        name="mla_proj",
    )(x, g.reshape(1, d).astype(F32), w["wd"], w["qn"], w["kvn"], w["wuq"], w["wuqr"], w["wuk"], w["wuvt"],
      cos_tab, sin_tab)


def _mla_weights(w_down, q_norm, kv_norm, w_uq, w_uk, w_uv):
    q_lora = q_norm.shape[0]
    kv_lora = kv_norm.shape[0]
    heads = w_uq.shape[1]
    qk = w_uq.shape[2]
    rope = qk - QK_NOPE
    half = rope // 2
    pad = SLOT - qk

    def slot_pair(wr):
        z_lo = jnp.zeros(wr.shape[:-1] + (QK_NOPE,), wr.dtype)
        z_hi = jnp.zeros(wr.shape[:-1] + (pad,), wr.dtype)
        plain = jnp.concatenate([z_lo, wr, z_hi], axis=-1)
        rot = jnp.concatenate([z_lo, -wr[..., half:], wr[..., :half], z_hi], axis=-1)
        return plain, rot

    w_kr = w_down[:, q_lora + kv_lora:]
    kr_plain, kr_rot = slot_pair(w_kr)
    wd = jnp.concatenate([w_down[:, :q_lora + kv_lora], kr_plain, kr_rot], axis=1)
    uq_nope = jnp.concatenate([w_uq[..., :QK_NOPE], jnp.zeros(w_uq.shape[:2] + (SLOT - QK_NOPE,), w_uq.dtype)], -1)
    uq_plain, uq_rot = slot_pair(w_uq[..., QK_NOPE:])
    wuq = (uq_nope + uq_plain).reshape(q_lora, heads * SLOT)
    wuqr = uq_rot.reshape(q_lora, heads * SLOT)
    wuk = jnp.concatenate([w_uk, jnp.zeros(w_uk.shape[:2] + (SLOT - QK_NOPE,), w_uk.dtype)], -1)
    wuk = wuk.reshape(kv_lora, heads * SLOT)
    wuvt = w_uv.reshape(kv_lora, heads * V_HEAD).T
    eye_r = jnp.zeros((SLOT, SLOT), w_uk.dtype).at[QK_NOPE + jnp.arange(rope), jnp.arange(rope)].set(1.0)
    uk_t = jnp.transpose(w_uk, (1, 2, 0))
    uk_t = jnp.concatenate([uk_t, jnp.zeros((heads, SLOT - QK_NOPE, kv_lora), w_uk.dtype)], axis=1)
    wabs = jnp.concatenate([uk_t, jnp.broadcast_to(eye_r, (heads, SLOT, SLOT))], axis=2)
    uv = jnp.transpose(w_uv, (1, 0, 2)).reshape(heads // 2, 2, kv_lora, V_HEAD)
    z = jnp.zeros((heads // 2, kv_lora, V_HEAD), w_uv.dtype)
    wuv_bd = jnp.concatenate([jnp.concatenate([uv[:, 0], z], axis=2), jnp.concatenate([z, uv[:, 1]], axis=2)], axis=1)
    return dict(wd=wd.astype(BF16), qn=q_norm.reshape(1, -1).astype(F32), kvn=kv_norm.reshape(1, -1).astype(F32),
                wuq=wuq.astype(BF16), wuqr=wuqr.astype(BF16), wuk=wuk.astype(BF16), wuvt=wuvt.astype(BF16),
                wabs=wabs.astype(BF16), wuv_bd=wuv_bd.astype(BF16), rope=rope,
                scale=LOG2E / math.sqrt(qk))


def _rope_tables(positions, rope):
    half = rope // 2
    inv = ROPE_THETA ** (-jnp.arange(half, dtype=F32) * (2.0 / rope))
    ang = positions.astype(F32)[:, None] * inv[None, :]
    c, s = jnp.cos(ang), jnp.sin(ang)
    p = positions.shape[0]
    ones = jnp.ones((p, QK_NOPE), F32)
    zeros = jnp.zeros((p, QK_NOPE), F32)
    zpad = jnp.zeros((p, SLOT - QK_NOPE - rope), F32)
    return (jnp.concatenate([ones, c, c, zpad], axis=1), jnp.concatenate([zeros, s, s, zpad], axis=1))


def _headmm_body(x_ref, w_ref, o_ref):
    o_ref[...] = jnp.dot(x_ref[...], w_ref[...], preferred_element_type=F32).astype(o_ref.dtype)


def _headmm(x, w, *, out_dtype=BF16):
    m = x.shape[0]
    g, kb, nb = w.shape
    assert x.shape[1] == g * kb
    return pl.pallas_call(
        _headmm_body,
        out_shape=jax.ShapeDtypeStruct((m, g * nb), out_dtype),
        grid=(g,),
        in_specs=[pl.BlockSpec((m, kb), lambda h: (0, h)), pl.BlockSpec((None, kb, nb), lambda h: (h, 0, 0))],
        out_specs=pl.BlockSpec((m, nb), lambda h: (0, h)),
        compiler_params=_cparams("parallel"),
        name="headmm",
    )(x, w)


def _flash_body(qi_ref, ki_ref, qt_ref, k_ref, vt_ref, o_ref, m_ref, acc_ref, *, heads, tq):
    p_idx = pl.program_id(1)
    qi = qi_ref[p_idx]
    kj = ki_ref[p_idx]

    @pl.when(kj == 0)
    def _():
        m_ref[...] = jnp.full(m_ref.shape, -jnp.inf, F32)
        acc_ref[...] = jnp.zeros(acc_ref.shape, F32)

    def step(blocks):
        key_i = lax.broadcasted_iota(jnp.int32, (tq, tq), 0)
        qry_i = lax.broadcasted_iota(jnp.int32, (tq, tq), 1)
        keep = key_i <= qry_i
        tasks = [(kb, diag, h) for kb, diag in blocks for h in range(heads)]

        def scores(task):
            kb, _, h = task
            k = k_ref[kb * tq:(kb + 1) * tq, h * SLOT:(h + 1) * SLOT]
            qt = qt_ref[h * SLOT:(h + 1) * SLOT, :]
            return jnp.dot(k, qt, preferred_element_type=F32)

        ahead = [scores(t) for t in tasks[:FLASH_SCORES_AHEAD]]
        for n, (kb, diag, h) in enumerate(tasks):
            s = ahead.pop(0)
            if n + FLASH_SCORES_AHEAD < len(tasks):
                ahead.append(scores(tasks[n + FLASH_SCORES_AHEAD]))
            if diag:
                s = jnp.where(keep, s, -jnp.inf)
            m_prev = m_ref[h:h + 1, :]
            m_new = jnp.maximum(m_prev, jnp.max(s, axis=0, keepdims=True))
            alpha = jnp.exp2(m_prev - m_new)
            p = jnp.exp2(s - m_new)
            m_ref[h:h + 1, :] = m_new
            vt = vt_ref[h * VT_SLOT:(h + 1) * VT_SLOT, kb * tq:(kb + 1) * tq]
            pv = jnp.dot(vt, p.astype(BF16), preferred_element_type=F32)
            acc_ref[h * VT_SLOT:(h + 1) * VT_SLOT, :] = acc_ref[h * VT_SLOT:(h + 1) * VT_SLOT, :] * alpha + pv

    def finish():
        outs = []
        for h in range(heads):
            l = acc_ref[h * VT_SLOT + V_HEAD:h * VT_SLOT + V_HEAD + 1, :]
            outs.append(acc_ref[h * VT_SLOT:h * VT_SLOT + V_HEAD, :] / l)
        o_ref[...] = jnp.concatenate(outs, axis=0).T.astype(o_ref.dtype)

    @pl.when(2 * kj + 1 < qi)
    def _():
        step([(0, False), (1, False)])

    @pl.when(2 * kj + 1 == qi)
    def _():
        step([(0, False), (1, True)])
        finish()

    @pl.when(2 * kj == qi)
    def _():
        step([(0, True)])
        finish()


def _flash(qt, kk, vt, *, nb, L, tq=ATTN_TILE):
    heads = kk.shape[1] // SLOT
    tq = _row_tile(L // 2, tq)
    nq = L // tq
    nk = nq // 2
    assert L == nk * 2 * tq
    pairs = [(i, j) for i in range(nq) for j in range(i // 2 + 1)]
    qi_tab = jnp.asarray([p[0] for p in pairs], jnp.int32)
    ki_tab = jnp.asarray([p[1] for p in pairs], jnp.int32)
    grid_spec = pltpu.PrefetchScalarGridSpec(
        num_scalar_prefetch=2,
        grid=(nb, len(pairs)),
        in_specs=[
            pl.BlockSpec((heads * SLOT, tq), lambda b, p, qi, ki: (0, b * nq + qi[p])),
            pl.BlockSpec((2 * tq, heads * SLOT), lambda b, p, qi, ki: (b * nk + ki[p], 0)),
            pl.BlockSpec((heads * VT_SLOT, 2 * tq), lambda b, p, qi, ki: (0, b * nk + ki[p])),
        ],
        out_specs=pl.BlockSpec((tq, heads * V_HEAD), lambda b, p, qi, ki: (b * nq + qi[p], 0)),
        scratch_shapes=[pltpu.VMEM((heads, tq), F32), pltpu.VMEM((heads * VT_SLOT, tq), F32)],
    )
    return pl.pallas_call(
        functools.partial(_flash_body, heads=heads, tq=tq),
        out_shape=jax.ShapeDtypeStruct((nb * L, heads * V_HEAD), BF16),
        grid_spec=grid_spec,
        compiler_params=_cparams("parallel", "arbitrary"),
        name="mla_flash",
    )(qi_tab, ki_tab, qt, kk, vt)


def _decode_body(pt_ref, q_ref, cnew_ref, rnew_ref, lat_hbm, ropet_hbm, o_ref,
                 cbuf, rbuf, cb, sem, *, layer, n_pages, page, chunk_pages, kv_lora, rope, heads, ls):
    b = pl.program_id(0)
    nb = pl.num_programs(0)
    slot = lax.rem(b, 2)
    rows = q_ref.shape[0]
    chunk = chunk_pages * page

    def page_copies(seq, sl):
        cps = []
        for i in range(n_pages):
            pg = pt_ref[seq, i]
            cps.append(pltpu.make_async_copy(lat_hbm.at[layer, pg], cbuf.at[sl, pl.ds(i * page, page), :], sem.at[0, sl]))
            cps.append(pltpu.make_async_copy(ropet_hbm.at[layer, pg], rbuf.at[sl, i], sem.at[1, sl]))
        return cps

    @pl.when(b == 0)
    def _():
        for cp in page_copies(0, 0):
            cp.start()

    @pl.when(b + 1 < nb)
    def _():
        for cp in page_copies(b + 1, 1 - slot):
            cp.start()

    for cp in page_copies(b, slot):
        cp.wait()

    q_lat = q_ref[:, :kv_lora]
    q_r = q_ref[:, kv_lora:kv_lora + rope]
    nt = (((1,), (1,)), ((), ()))
    n_chunks = n_pages // chunk_pages

    def scores(ck):
        c_b = cbuf[slot, ck * chunk:(ck + 1) * chunk, :].astype(BF16)
        cb[ck * chunk:(ck + 1) * chunk, :] = c_b
        r_b = jnp.concatenate([rbuf[slot, ck * chunk_pages + i].astype(BF16) for i in range(chunk_pages)],
                              axis=1)
        return (lax.dot_general(q_lat, c_b, nt, preferred_element_type=F32)
                + jnp.dot(q_r, r_b, preferred_element_type=F32))

    cn = cnew_ref[...].astype(BF16)
    rn = rnew_ref[...].astype(BF16)
    sn = (lax.dot_general(q_lat, cn, nt, preferred_element_type=F32)
          + lax.dot_general(q_r, rn, nt, preferred_element_type=F32))
    qpos = lax.broadcasted_iota(jnp.int32, (rows, ls), 0) // heads
    kpos = lax.broadcasted_iota(jnp.int32, (rows, ls), 1)
    sn = jnp.where(qpos >= kpos, sn, -jnp.inf)
    m = jnp.max(sn, axis=-1, keepdims=True)
    pn = jnp.exp2(sn - m)
    l = jnp.sum(pn, axis=-1, keepdims=True)
    acc = jnp.dot(pn.astype(BF16), cn, preferred_element_type=F32)

    s_next = scores(0)
    for ck in range(n_chunks):
        s = s_next
        if ck + 1 < n_chunks:
            s_next = scores(ck + 1)
        m_new = jnp.maximum(m, jnp.max(s, axis=-1, keepdims=True))
        alpha = jnp.exp2(m - m_new)
        p = jnp.exp2(s - m_new)
        l = alpha * l + jnp.sum(p, axis=-1, keepdims=True)
        acc = alpha * acc + jnp.dot(p.astype(BF16), cb[ck * chunk:(ck + 1) * chunk, :], preferred_element_type=F32)
        m = m_new
    o_ref[...] = (acc / l).astype(o_ref.dtype)


def _decode(q_ext, lat_new, kr_new, lat_pool, ropet_pool, page_table, *, layer, bs, ls, heads, new_row_offset,
            chunk_pages=DECODE_CHUNK_PAGES):
    kv_lora = lat_new.shape[1]
    rope = kr_new.shape[1]
    page = lat_pool.shape[2]
    n_pages = page_table.shape[1]
    chunk_pages = math.gcd(chunk_pages, n_pages)
    assert new_row_offset % ls == 0
    rows = ls * heads
    qw = q_ext.shape[1]
    grid_spec = pltpu.PrefetchScalarGridSpec(
        num_scalar_prefetch=1,
        grid=(bs,),
        in_specs=[pl.BlockSpec((rows, qw), lambda b, pt: (b, 0)),
                  pl.BlockSpec((ls, kv_lora), lambda b, pt: (new_row_offset // ls + b, 0)),
                  pl.BlockSpec((ls, rope), lambda b, pt: (new_row_offset // ls + b, 0)),
                  pl.BlockSpec(memory_space=pl.ANY),
                  pl.BlockSpec(memory_space=pl.ANY)],
        out_specs=pl.BlockSpec((rows, kv_lora), lambda b, pt: (b, 0)),
        scratch_shapes=[pltpu.VMEM((2, n_pages * page, kv_lora), F32),
                        pltpu.VMEM((2, n_pages, rope, page), F32),
                        pltpu.VMEM((n_pages * page, kv_lora), BF16),
                        pltpu.SemaphoreType.DMA((2, 2))],
    )
    return pl.pallas_call(
        functools.partial(_decode_body, layer=layer, n_pages=n_pages, page=page, chunk_pages=chunk_pages,
                          kv_lora=kv_lora, rope=rope, heads=heads, ls=ls),
        out_shape=jax.ShapeDtypeStruct((bs * rows, kv_lora), BF16),
        grid_spec=grid_spec,
        compiler_params=_cparams("arbitrary"),
        name="mla_decode",
    )(page_table, q_ext, lat_new, kr_new, lat_pool, ropet_pool)


def kernel(x_prompt, x_sample, mem_prompt, state_ssm, state_conv, cache_mla_latent, cache_mla_rope_k, cache_mem_k, cache_mem_v, page_table, norm_mix, norm_mem, norm_memkv, norm_ffn, norm_final, ssd_w_in, ssd_conv_w, ssd_conv_b, ssd_dt_bias, ssd_a_log, ssd_d, ssd_norm, ssd_w_out, mla_w_down, mla_q_norm, mla_kv_norm, mla_w_uq, mla_w_uk, mla_w_uv, mla_w_o, mem_w_q, mem_w_kv, mem_w_o, mlp_w_up, mlp_w_down):
    bp, lp, d = x_prompt.shape
    bs, ls, _ = x_sample.shape
    mp, ms = bp * lp, bs * ls
    depth = norm_mix.shape[0]
    n_mem = mem_prompt.shape[1]
    past_len = page_table.shape[1] * cache_mla_latent.shape[2]
    mla_heads = mla_w_uq.shape[2]
    rope = mla_w_uq.shape[3] - QK_NOPE
    ssd_heads = ssd_dt_bias.shape[1]
    d_inner = ssd_heads * SSD_HEADDIM

    x = None
    x_in = (x_prompt.reshape(mp, d), x_sample.reshape(ms, d))
    mem_rows = mem_prompt.reshape(bp * n_mem, d)
    ropet_pool = jnp.swapaxes(cache_mla_rope_k, 2, 3)
    ssm0 = state_ssm.reshape(state_ssm.shape[0], bs, d_inner, D_STATE)

    pos = jnp.concatenate([jnp.arange(lp), jnp.tile(past_len + jnp.arange(ls), bs)])
    cos_tab, sin_tab = _rope_tables(pos, rope)
    proj_tm = _row_tile(ms, ATTN_TILE)
    assert lp % proj_tm == 0
    npt, tpl = mp // proj_tm, lp // proj_tm
    tab_index = lambda i: jnp.where(i < npt, i % tpl, tpl + (i - npt))

    mem_rows_p = _row_tile(lp, ROW_TILE)
    mem_nb = _row_tile(bs, MEM_DEC_SEQS)

    p_ssm, p_conv, p_lat, p_rk, p_mk, p_mv = [], [], [], [], [], []
    s_conv, s_lat, s_rk = [], [], []
    s_ssm = None
    for i in range(depth):
        j = i // 2
        if i % 2 == 0:
            w_pad = _ssd_w_in_pad(ssd_w_in[j]).astype(BF16)
            prm = _ssd_params(ssd_conv_w[j], ssd_conv_b[j], ssd_dt_bias[j], ssd_a_log[j], ssd_d[j], ssd_norm[j])
            x_head, x_tail = x_in if x is None else (x, x[mp:])
            y_p, st_p, ct_p = _ssd_fused(x_head, norm_mix[i], w_pad, *prm, nb=bp, L=lp, T=math.gcd(SSD_CHUNK, lp))
            zx_s = _mm(x_tail, w_pad, g=norm_mix[i], tn=_in_proj_tile(w_pad.shape[1]), name="ssd_in")
            conv0 = jnp.pad(state_conv[j], ((0, 0), (8 - state_conv.shape[2], 0), (0, 0)))
            t_s = math.gcd(SSD_CHUNK, ls)
            y_s, s_ssm, ct_s = _ssd(zx_s, *prm, conv0, ssm0, row_offset=0, nb=bs, L=ls, T=t_s, layer=j,
                                    nseq=math.gcd(SSD_SAMPLE_SEQS, bs) if ls == t_s else 1,
                                    prev_states=None if s_ssm is None else s_ssm.reshape(-1, bs, d_inner, D_STATE))
            x = _mm(y_p, ssd_w_out[j].astype(BF16), x_tail=y_s, name="ssd_out",
                    **(dict(res=x_head, res_tail=x_tail) if x is None else dict(res=x)))
            kc = state_conv.shape[2]
            p_ssm.append(st_p.reshape(bp, ssd_heads, SSD_HEADDIM, D_STATE))
            p_conv.append(ct_p[:, 8 - kc:, :])
            s_conv.append(ct_s[:, 8 - kc:, :])
        else:
            w = _mla_weights(mla_w_down[j], mla_q_norm[j], mla_kv_norm[j], mla_w_uq[j], mla_w_uk[j], mla_w_uv[j])
            q, qt, kk, vt, lat, kr = _mla_proj(x, norm_mix[i], w, cos_tab, sin_tab, tab_index, tm=proj_tm)
            o_p = _flash(qt, kk, vt, nb=bp, L=lp)
            q_ext = _headmm(q[mp:], w["wabs"]).reshape(ms * mla_heads, -1)
            o_lat = _decode(q_ext, lat, kr, cache_mla_latent, ropet_pool, page_table, layer=j, bs=bs, ls=ls,
                            heads=mla_heads, new_row_offset=mp)
            o_s = _headmm(o_lat.reshape(ms, -1), w["wuv_bd"])
            x = _mm(o_p, mla_w_o[j].astype(BF16), x_tail=o_s, res=x, name="mla_out")
            p_lat.append(lat[:mp].reshape(bp, lp, -1))
            s_lat.append(lat[mp:].reshape(bs, ls, -1))
            p_rk.append(kr[:mp].reshape(bp, lp, -1))
            s_rk.append(kr[mp:].reshape(bs, ls, -1))
        kv = _mm(mem_rows, mem_w_kv[i].astype(BF16), g=norm_memkv[i], name="mem_kv")
        kp = kv[:, :d].reshape(bp, n_mem, d)
        vp = kv[:, d:].reshape(bp, n_mem, d)
        p_mk.append(kp.reshape(bp, n_mem, MEM_HEADS, d // MEM_HEADS))
        p_mv.append(vp.reshape(bp, n_mem, MEM_HEADS, d // MEM_HEADS))
        wq, wo = mem_w_q[i].astype(BF16), mem_w_o[i].astype(BF16)
        x_p = _mem_fused(x, norm_mem[i], wq, kp.astype(BF16), vp.astype(BF16), wo, n_seq=bp, L=lp, rows=mem_rows_p)
        x_s = x[mp:]
        qm_s = _mm(x_s, wq, g=norm_mem[i], out_dtype=BF16, name="mem_q")
        o_s = _memattn_dec(qm_s, cache_mem_k, cache_mem_v, layer=i, row_offset=0, n_seq=bs, rows=ls, nb=mem_nb)
        x_s = _mm(o_s, wo, res=x_s, name="mem_out")
        x = _mlp(x_p, norm_ffn[i], mlp_w_up[i].astype(BF16), mlp_w_down[i].astype(BF16), x_tail=x_s,
                 final_g=norm_final if i == depth - 1 else None)
    y_p, y_s = x
    return (y_p.reshape(bp, lp, d), y_s.reshape(bs, ls, d),
            jnp.stack(p_ssm), jnp.stack(p_conv), jnp.stack(p_lat), jnp.stack(p_rk), jnp.stack(p_mk), jnp.stack(p_mv),
            s_ssm.reshape(state_ssm.shape), jnp.stack(s_conv), jnp.stack(s_lat), jnp.stack(s_rk))
```

```python
import functools
import math

import jax
import jax.numpy as jnp
from jax import lax
from jax.experimental import pallas as pl
from jax.experimental.pallas import tpu as pltpu

F32 = jnp.float32
BF16 = jnp.bfloat16

EPS = 1e-6
ROPE_THETA = 10000.0

V7X_LANES = 128
V7X_MXU_COLS = 256
V7X_VMEM_BYTES = 64 * 1024 * 1024
V7X_VMEM_LIMIT_BYTES = V7X_VMEM_BYTES * 7 // 8

SSD_HEADDIM = 64
SSD_GROUPS = 8
D_STATE = 128
SSD_CHUNK = 128
QK_NOPE = 64
V_HEAD = 64
MEM_HEADS = 4

ROW_TILE = 1024
COL_TILE = 1024
IN_PROJ_MAX_TILE = 1536
ATTN_TILE = 512
DECODE_CHUNK_PAGES = 16
MEM_DEC_SEQS = 4
SSD_SAMPLE_SEQS = 2


def _cparams(*sem):
    return pltpu.CompilerParams(dimension_semantics=sem, vmem_limit_bytes=V7X_VMEM_LIMIT_BYTES)


def _rms(x, g):
    return x * lax.rsqrt(jnp.mean(x * x, axis=-1, keepdims=True) + EPS) * g


def _row_tile(m, pref):
    t = min(pref, m)
    assert m % t == 0, (m, t)
    return t


def _mm_body(*refs, norm, res, res_split, head_tiles):
    it = iter(refs)
    x_ref = next(it)
    t_ref = next(it) if head_tiles is not None else None
    g_ref = next(it) if norm else None
    w_ref = next(it)
    r_ref = next(it) if res else None
    rt_ref = next(it) if res_split else None
    o_ref = next(it)
    xn_ref = next(it)

    def stage(src_ref):
        x = src_ref[...].astype(F32)
        if norm:
            x = _rms(x, g_ref[...])
        xn_ref[...] = x.astype(BF16)

    first = pl.program_id(1) == 0
    if head_tiles is None:
        pl.when(first)(lambda: stage(x_ref))
    else:
        in_head = pl.program_id(0) < head_tiles
        pl.when(first & in_head)(lambda: stage(x_ref))
        pl.when(first & jnp.logical_not(in_head))(lambda: stage(t_ref))

    acc = jnp.dot(xn_ref[...], w_ref[...], preferred_element_type=F32)
    if res_split:
        in_head = pl.program_id(0) < head_tiles

        @pl.when(in_head)
        def _():
            o_ref[...] = (r_ref[...] + acc).astype(o_ref.dtype)

        @pl.when(jnp.logical_not(in_head))
        def _():
            o_ref[...] = (rt_ref[...] + acc).astype(o_ref.dtype)
    else:
        if res:
            acc = r_ref[...] + acc
        o_ref[...] = acc.astype(o_ref.dtype)


def _mm(x, w, *, x_tail=None, g=None, res=None, res_tail=None, out_dtype=F32, tm=ROW_TILE, tn=COL_TILE, name="mm"):
    m, k = x.shape
    k2, n = w.shape
    assert k == k2
    assert res_tail is None or (x_tail is not None and res is not None
                                and res.shape[0] == m and res_tail.shape[0] == x_tail.shape[0])
    head_tiles = None
    if x_tail is None:
        tm = _row_tile(m, tm)
        in_specs = [pl.BlockSpec((tm, k), lambda i, j: (i, 0))]
        args = [x]
    else:
        m_tail = x_tail.shape[0]
        tm = math.gcd(math.gcd(m, m_tail), tm)
        head_tiles = m // tm
        m = m + m_tail
        in_specs = [pl.BlockSpec((tm, k), lambda i, j: (jnp.minimum(i, head_tiles - 1), 0)),
                    pl.BlockSpec((tm, k), lambda i, j: (jnp.maximum(i - head_tiles, 0), 0))]
        args = [x, x_tail]
    tn = _row_tile(n, tn)
    if g is not None:
        in_specs.append(pl.BlockSpec((1, k), lambda i, j: (0, 0)))
        args.append(g.reshape(1, k).astype(F32))
    in_specs.append(pl.BlockSpec((k, tn), lambda i, j: (0, j)))
    args.append(w)
    if res_tail is not None:
        in_specs += [pl.BlockSpec((tm, tn), lambda i, j: (jnp.minimum(i, head_tiles - 1), j)),
                     pl.BlockSpec((tm, tn), lambda i, j: (jnp.maximum(i - head_tiles, 0), j))]
        args += [res, res_tail]
    elif res is not None:
        in_specs.append(pl.BlockSpec((tm, tn), lambda i, j: (i, j)))
        args.append(res)
    return pl.pallas_call(
        functools.partial(_mm_body, norm=g is not None, res=res is not None, res_split=res_tail is not None,
                          head_tiles=head_tiles),
        out_shape=jax.ShapeDtypeStruct((m, n), out_dtype),
        grid=(m // tm, n // tn),
        in_specs=in_specs,
        out_specs=pl.BlockSpec((tm, tn), lambda i, j: (i, j)),
        scratch_shapes=[pltpu.VMEM((tm, k), BF16)],
        compiler_params=_cparams("parallel", "arbitrary"),
        name=name,
    )(*args)


def _mlp_body(*refs, head_tiles, final_norm):
    x_ref = refs[0]
    t_ref = refs[1] if head_tiles is not None else None
    refs = refs[1 + int(head_tiles is not None):]
    g_ref, wu_ref, wd_ref = refs[:3]
    fg_ref = refs[3] if final_norm else None
    refs = refs[3 + int(final_norm):]
    o_ref = refs[0]
    ot_ref = refs[1] if final_norm else None
    xn_ref, acc_ref = refs[1 + int(final_norm):]
    k = pl.program_id(1)

    def stage(src_ref):
        x = src_ref[...]
        xn_ref[...] = _rms(x, g_ref[...]).astype(BF16)
        acc_ref[...] = x

    if head_tiles is None:
        pl.when(k == 0)(lambda: stage(x_ref))
    else:
        in_head = pl.program_id(0) < head_tiles
        pl.when((k == 0) & in_head)(lambda: stage(x_ref))
        pl.when((k == 0) & jnp.logical_not(in_head))(lambda: stage(t_ref))

    h = jnp.dot(xn_ref[...], wu_ref[...], preferred_element_type=F32)
    h = jnp.square(jnp.maximum(h, 0.0)).astype(BF16)
    acc_ref[...] += jnp.dot(h, wd_ref[...], preferred_element_type=F32)

    last = k == pl.num_programs(1) - 1
    if not final_norm:
        @pl.when(last)
        def _():
            o_ref[...] = acc_ref[...]
    else:
        in_head = pl.program_id(0) < head_tiles

        @pl.when(last & in_head)
        def _():
            o_ref[...] = _rms(acc_ref[...], fg_ref[...])

        @pl.when(last & jnp.logical_not(in_head))
        def _():
            ot_ref[...] = _rms(acc_ref[...], fg_ref[...])


def _mlp(x, g, w_up, w_down, *, x_tail=None, final_g=None, tm=ROW_TILE, tf=COL_TILE):
    m, d = x.shape
    ff = w_up.shape[1]
    head_tiles = None
    assert final_g is None or x_tail is not None
    if x_tail is None:
        tm = _row_tile(m, tm)
        row_specs, row_args = [pl.BlockSpec((tm, d), lambda i, k: (i, 0))], [x]
    else:
        m_tail = x_tail.shape[0]
        tm = math.gcd(math.gcd(m, m_tail), tm)
        head_tiles = m // tm
        m = m + m_tail
        row_specs = [pl.BlockSpec((tm, d), lambda i, k: (jnp.minimum(i, head_tiles - 1), 0)),
                     pl.BlockSpec((tm, d), lambda i, k: (jnp.maximum(i - head_tiles, 0), 0))]
        row_args = [x, x_tail]
    tf = _row_tile(ff, tf)
    if final_g is None:
        out_shape = jax.ShapeDtypeStruct((m, d), F32)
        out_specs = pl.BlockSpec((tm, d), lambda i, k: (i, 0))
        fg_specs, fg_args = [], []
    else:
        out_shape = (jax.ShapeDtypeStruct(x.shape, F32), jax.ShapeDtypeStruct(x_tail.shape, F32))
        out_specs = tuple(row_specs)
        fg_specs, fg_args = [pl.BlockSpec((1, d), lambda i, k: (0, 0))], [final_g.reshape(1, d).astype(F32)]
    return pl.pallas_call(
        functools.partial(_mlp_body, head_tiles=head_tiles, final_norm=final_g is not None),
        out_shape=out_shape,
        grid=(m // tm, ff // tf),
        in_specs=row_specs + [
            pl.BlockSpec((1, d), lambda i, k: (0, 0)),
            pl.BlockSpec((d, tf), lambda i, k: (0, k)),
            pl.BlockSpec((tf, d), lambda i, k: (k, 0)),
        ] + fg_specs,
        out_specs=out_specs,
        scratch_shapes=[pltpu.VMEM((tm, d), BF16), pltpu.VMEM((tm, d), F32)],
        compiler_params=_cparams("parallel" if final_g is None else "arbitrary", "arbitrary"),
        name="mlp",
    )(*row_args, g.reshape(1, d).astype(F32), w_up, w_down, *fg_args)


def _mem_fused_body(x_ref, g_ref, wq_ref, k_ref, v_ref, wo_ref, o_ref, *, heads, scale):
    x = x_ref[...]
    q = jnp.dot(_rms(x, g_ref[...]).astype(BF16), wq_ref[...], preferred_element_type=F32)
    dh = q.shape[-1] // heads
    outs = []
    for h in range(heads):
        q_h = q[:, h * dh:(h + 1) * dh].astype(BF16)
        k_h = k_ref[:, h * dh:(h + 1) * dh]
        v_h = v_ref[:, h * dh:(h + 1) * dh]
        s = lax.dot_general(q_h, k_h, (((1,), (1,)), ((), ())), preferred_element_type=F32) * scale
        s = s - jnp.max(s, axis=-1, keepdims=True)
        p = jnp.exp(s)
        p = p / jnp.sum(p, axis=-1, keepdims=True)
        outs.append(jnp.dot(p.astype(BF16), v_h, preferred_element_type=F32).astype(BF16))
    o = jnp.concatenate(outs, axis=1)
    o_ref[...] = x + jnp.dot(o, wo_ref[...], preferred_element_type=F32)


def _mem_fused(x, g, wq, k, v, wo, *, n_seq, L, rows):
    d = x.shape[-1]
    n_mem = k.shape[-2]
    per_seq = L // rows
    const = lambda s: (0, 0)
    weight = lambda: pl.BlockSpec((d, d), const, pipeline_mode=pl.Buffered(1))
    return pl.pallas_call(
        functools.partial(_mem_fused_body, heads=MEM_HEADS, scale=1.0 / math.sqrt(d // MEM_HEADS)),
        out_shape=jax.ShapeDtypeStruct((n_seq * L, d), F32),
        grid=(n_seq * per_seq,),
        in_specs=[pl.BlockSpec((rows, d), lambda s: (s, 0)),
                  pl.BlockSpec((1, d), const),
                  weight(),
                  pl.BlockSpec((None, n_mem, d), lambda s: (s // per_seq, 0, 0)),
                  pl.BlockSpec((None, n_mem, d), lambda s: (s // per_seq, 0, 0)),
                  weight()],
        out_specs=pl.BlockSpec((rows, d), lambda s: (s, 0)),
        compiler_params=_cparams("parallel"),
        name="mem_fused",
    )(x, g.reshape(1, d).astype(F32), wq, k, v, wo)


def _memattn_dec_body(q_ref, k_ref, v_ref, o_ref, *, nb, rows, heads, scale):
    dh = q_ref.shape[-1] // heads
    n_mem = k_ref.shape[1]
    qrow_head = lax.broadcasted_iota(jnp.int32, (heads * rows, n_mem * heads), 0) // rows
    key_head = lax.broadcasted_iota(jnp.int32, (heads * rows, n_mem * heads), 1) % heads
    own = qrow_head == key_head
    outs = []
    q_all = q_ref[...].astype(F32)
    for i in range(nb):
        qi = q_all[i * rows:(i + 1) * rows, :]
        qs = jnp.concatenate([qi[:, h * dh:(h + 1) * dh] for h in range(heads)], axis=0).astype(BF16)
        k2 = k_ref[i].reshape(n_mem * heads, dh).astype(BF16)
        v2 = v_ref[i].reshape(n_mem * heads, dh).astype(BF16)
        s = lax.dot_general(qs, k2, (((1,), (1,)), ((), ())), preferred_element_type=F32) * scale
        s = jnp.where(own, s, -jnp.inf)
        s = s - jnp.max(s, axis=-1, keepdims=True)
        p = jnp.exp(s)
        p = p / jnp.sum(p, axis=-1, keepdims=True)
        o = jnp.dot(p.astype(BF16), v2, preferred_element_type=F32)
        outs.append(jnp.concatenate([o[h * rows:(h + 1) * rows, :] for h in range(heads)], axis=1))
    o_ref[...] = jnp.concatenate(outs, axis=0).astype(o_ref.dtype)


def _memattn_dec(q, k, v, *, layer, row_offset, n_seq, rows, nb):
    d = q.shape[-1]
    _, _, n_mem, heads, dh = k.shape
    blk = nb * rows
    assert row_offset % blk == 0 and n_seq % nb == 0
    kv_spec = pl.BlockSpec((None, nb, n_mem, heads, dh), lambda s: (layer, s, 0, 0, 0))
    return pl.pallas_call(
        functools.partial(_memattn_dec_body, nb=nb, rows=rows, heads=heads, scale=1.0 / math.sqrt(dh)),
        out_shape=jax.ShapeDtypeStruct((n_seq * rows, d), BF16),
        grid=(n_seq // nb,),
        in_specs=[pl.BlockSpec((blk, d), lambda s: (row_offset // blk + s, 0)), kv_spec, kv_spec],
        out_specs=pl.BlockSpec((blk, d), lambda s: (s, 0)),
        compiler_params=_cparams("parallel"),
        name="memattn_dec",
    )(q, k, v)


def _split3_lanes(v, heads):
    lane = lax.broadcasted_iota(jnp.int32, v.shape, 1)
    hi = v.astype(BF16).astype(F32)
    r1 = v - hi
    mid = r1.astype(BF16).astype(F32)
    lo = r1 - mid
    out = jnp.where(lane < heads, hi, jnp.where(lane < 2 * heads, mid, jnp.where(lane < 3 * heads, lo, 0.0)))
    return out.astype(BF16)


def _ssd_init(conv0_ref, s0_ref, extx_ref, extbc_ref, st_ref):
    d_inner = extx_ref.shape[-1]
    if conv0_ref is not None:
        extx_ref[0:8, :] = conv0_ref[:, 0:d_inner]
        extbc_ref[0:8, :] = conv0_ref[:, d_inner:]
        st_ref[...] = s0_ref[...].T
    else:
        extx_ref[0:8, :] = jnp.zeros((8, d_inner), F32)
        extbc_ref[0:8, :] = jnp.zeros((8, extbc_ref.shape[-1]), F32)
        st_ref[...] = jnp.zeros(st_ref.shape, F32)


def _ssd_finish(sfin_ref, ctail_ref, extx_ref, extbc_ref, st_ref):
    d_inner = extx_ref.shape[-1]
    sfin_ref[...] = st_ref[...].T
    ctail_ref[:, 0:d_inner] = extx_ref[0:8, :]
    ctail_ref[:, d_inner:] = extbc_ref[0:8, :]


def _scan_chunk(z, x_raw, bc_raw, dt_raw, cw_ref, cb_ref, dtb_ref, alog_ref, dsk_ref, ng_ref, e3_ref,
                y_ref, y_row0, extx_ref, extbc_ref, st_ref, *, T, heads, groups):
    d_inner = extx_ref.shape[-1]
    hpg = heads // groups
    gw = d_inner // groups
    hd = d_inner // heads
    n = D_STATE
    gate = z * jax.nn.sigmoid(z)

    extx_ref[8:8 + T, :] = x_raw
    extbc_ref[8:8 + T, :] = bc_raw

    def conv(ext_ref, lo, width):
        cur = ext_ref[8:8 + T, :]
        tail = ext_ref[0:8, :]
        r8 = lax.broadcasted_iota(jnp.int32, (8, width), 0)
        kw = cw_ref.shape[0]
        acc = cb_ref[:, lo:lo + width] + cur * cw_ref[kw - 1:kw, lo:lo + width]
        for s in range(1, kw):
            rolled = pltpu.roll(cur, s, axis=0)
            head = jnp.where(r8 < s, pltpu.roll(tail, s, axis=0), rolled[0:8])
            shifted = head if T == 8 else jnp.concatenate([head, rolled[8:]], axis=0)
            acc = acc + shifted * cw_ref[kw - 1 - s:kw - s, lo:lo + width]
        return acc * jax.nn.sigmoid(acc)

    xc = conv(extx_ref, 0, d_inner)
    bcc = conv(extbc_ref, d_inner, extbc_ref.shape[-1])
    extx_ref[0:8, :] = extx_ref[T:T + 8, :]
    extbc_ref[0:8, :] = extbc_ref[T:T + 8, :]

    dtr = dt_raw + dtb_ref[...]
    dtv = jnp.maximum(dtr, 0.0) + jnp.log1p(jnp.exp(-jnp.abs(dtr)))
    a = dtv * (-jnp.exp(alog_ref[...]))
    row = lax.broadcasted_iota(jnp.int32, a.shape, 0)
    acs = a
    sh = 1
    while sh < T:
        acs = acs + jnp.where(row >= sh, pltpu.roll(acs, sh, axis=0), 0.0)
        sh *= 2
    a_last = acs[T - 1:T, :]
    e3 = e3_ref[...]

    per_head = [dtv, jnp.exp(a_last - acs), jnp.exp(acs)]
    if T % V7X_LANES == 0:
        stacked = jnp.concatenate([_split3_lanes(v, heads) for v in per_head], axis=0)
        expanded = jnp.dot(stacked, e3, preferred_element_type=F32)
        dt_e, dend_e, eacs_e = expanded[0:T], expanded[T:2 * T], expanded[2 * T:3 * T]
    else:
        dt_e, dend_e, eacs_e = [jnp.dot(_split3_lanes(v, heads), e3, preferred_element_type=F32) for v in per_head]
    xdt = xc * dt_e
    xdt_b = xdt.astype(BF16)
    xw_b = (xdt * dend_e).astype(BF16)
    cdec_e = eacs_e[T - 1:T, :]

    acs2 = acs * LOG2E
    if T < V7X_LANES:
        acs_p = jnp.concatenate([acs2, jnp.zeros((V7X_LANES - T, V7X_LANES), F32)], axis=0)
    else:
        acs_p = acs2
    acs_t = acs_p.T
    ti = lax.broadcasted_iota(jnp.int32, (T, T), 0)
    si = lax.broadcasted_iota(jnp.int32, (T, T), 1)
    causal = ti >= si
    lane_g = lax.broadcasted_iota(jnp.int32, (T, gw), 1)

    for g in range(groups):
        bg = bcc[:, g * n:(g + 1) * n]
        cg = bcc[:, groups * n + g * n: groups * n + (g + 1) * n].astype(BF16)
        bg_b = bg.astype(BF16)
        gmat = lax.dot_general(cg, bg_b, (((1,), (1,)), ((), ())), preferred_element_type=F32)
        st_g = st_ref[:, g * gw:(g + 1) * gw]
        y_g = jnp.dot(cg, st_g.astype(BF16), preferred_element_type=F32) * eacs_e[:, g * gw:(g + 1) * gw]
        xg = xdt_b[:, g * gw:(g + 1) * gw]
        m_heads, x_heads = [], []
        for j in range(hpg):
            h = g * hpg + j
            seg = acs2[:, h:h + 1] - acs_t[h:h + 1, 0:T]
            decay = jnp.exp2(jnp.where(causal, seg, -jnp.inf))
            m_heads.append((gmat * decay).astype(BF16))
            x_heads.append(jnp.where((lane_g >= j * hd) & (lane_g < (j + 1) * hd), xg, jnp.zeros_like(xg)))
        if T % V7X_LANES == 0:
            y_g = y_g + jnp.dot(jnp.concatenate(m_heads, axis=1), jnp.concatenate(x_heads, axis=0),
                                preferred_element_type=F32)
        else:
            for m_h, x_h in zip(m_heads, x_heads):
                y_g = y_g + jnp.dot(m_h, x_h, preferred_element_type=F32)
        cs_t = jnp.dot(bg.T.astype(BF16), xw_b[:, g * gw:(g + 1) * gw], preferred_element_type=F32)
        st_ref[:, g * gw:(g + 1) * gw] = st_g * cdec_e[:, g * gw:(g + 1) * gw] + cs_t
        y_g = y_g + xc[:, g * gw:(g + 1) * gw] * dsk_ref[:, g * gw:(g + 1) * gw]
        y_g = y_g * gate[:, g * gw:(g + 1) * gw]
        y_g = _rms(y_g, ng_ref[:, g * gw:(g + 1) * gw])
        y_ref[y_row0:y_row0 + T, g * gw:(g + 1) * gw] = y_g.astype(y_ref.dtype)


def _ssd_body(*refs, T, heads, groups, nseq, has_prev):
    (z_ref, x_ref, bc_ref, dt_ref, cw_ref, cb_ref, dtb_ref, alog_ref, dsk_ref, ng_ref, e3_ref,
     conv0_ref, s0_ref) = refs[:13]
    prev_ref = refs[13] if has_prev else None
    y_ref, sfin_ref, ctail_ref, extx_ref, extbc_ref, st_ref = refs[13 + int(has_prev):]
    c = pl.program_id(1)

    @pl.when(c == 0)
    def _():
        for i in range(nseq):
            _ssd_init(conv0_ref.at[i], s0_ref.at[i], extx_ref.at[i], extbc_ref.at[i], st_ref.at[i])

    for i in range(nseq):
        rows = slice(i * T, (i + 1) * T)
        _scan_chunk(z_ref[rows, :], x_ref[rows, :], bc_ref[rows, :], dt_ref[rows, :],
                    cw_ref, cb_ref, dtb_ref, alog_ref, dsk_ref, ng_ref, e3_ref,
                    y_ref, i * T, extx_ref.at[i], extbc_ref.at[i], st_ref.at[i], T=T, heads=heads, groups=groups)

    @pl.when(c == pl.num_programs(1) - 1)
    def _():
        for i in range(nseq):
            if has_prev:
                n_prev = prev_ref.shape[0]
                sfin_ref[0:n_prev, i] = prev_ref[:, i]
                sfin_i = sfin_ref.at[n_prev, i]
            else:
                sfin_i = sfin_ref.at[i]
            _ssd_finish(sfin_i, ctail_ref.at[i], extx_ref.at[i], extbc_ref.at[i], st_ref.at[i])


def _ssd_fused_body(xcur_ref, xnext_ref, g_ref, win_ref, cw_ref, cb_ref, dtb_ref, alog_ref, dsk_ref, ng_ref, e3_ref,
                    y_ref, sfin_ref, ctail_ref,
                    extx_ref, extbc_ref, st_ref, zxa_ref, zxb_ref, *, T, heads, groups):
    k = pl.program_id(1)
    d_inner = extx_ref.shape[-1]
    bcw = extbc_ref.shape[-1]

    def in_proj(rows):
        xn = _rms(rows, g_ref[...]).astype(BF16)
        return jnp.dot(xn, win_ref[...], preferred_element_type=F32)

    def scan(zx_ref, y_row0):
        _scan_chunk(zx_ref[:, 0:d_inner], zx_ref[:, d_inner:2 * d_inner],
                    zx_ref[:, 2 * d_inner:2 * d_inner + bcw],
                    zx_ref[:, 2 * d_inner + bcw:2 * d_inner + bcw + V7X_LANES],
                    cw_ref, cb_ref, dtb_ref, alog_ref, dsk_ref, ng_ref, e3_ref,
                    y_ref, y_row0, extx_ref, extbc_ref, st_ref, T=T, heads=heads, groups=groups)

    @pl.when(k == 0)
    def _():
        _ssd_init(None, None, extx_ref, extbc_ref, st_ref)
        zxa_ref[...] = in_proj(xcur_ref[0:T, :])

    zxb_ref[...] = in_proj(xcur_ref[T:2 * T, :])
    scan(zxa_ref, 0)
    zxa_ref[...] = in_proj(xnext_ref[0:T, :])
    scan(zxb_ref, T)
    pl.when(k == pl.num_programs(1) - 1)(
        functools.partial(_ssd_finish, sfin_ref.at[0], ctail_ref.at[0], extx_ref, extbc_ref, st_ref))


def _ssd(zx, conv_w, conv_b, dt_bias_rep, a_log_rep, d_exp, norm_g, e3, conv0, s0, *, row_offset, nb, L, T,
         layer, nseq=1, prev_states=None):
    heads = d_exp.shape[-1] // SSD_HEADDIM
    d_inner = d_exp.shape[-1]
    bcw = 2 * SSD_GROUPS * D_STATE
    assert bcw == d_inner, "column blocks are indexed in units of d_inner"
    nc = L // T
    rows = nseq * T
    assert nb % nseq == 0 and row_offset % rows == 0 and (nseq == 1 or nc == 1)
    rb0 = row_offset // rows
    rowmap = lambda col: (lambda b, c: (rb0 + b * nc + c, col))
    const = lambda b, c: (0, 0)
    per_b = lambda b, c: (b, 0, 0)
    dt_col = (2 * d_inner + bcw) // V7X_LANES
    if prev_states is None:
        prev_args, prev_specs = [], []
        sfin_shape = (nb, d_inner, D_STATE)
        sfin_spec = pl.BlockSpec((nseq, d_inner, D_STATE), per_b)
    else:
        n_prev = prev_states.shape[0]
        prev_args = [prev_states]
        prev_specs = [pl.BlockSpec((n_prev, nseq, d_inner, D_STATE), lambda b, c: (0, b, 0, 0))]
        sfin_shape = (n_prev + 1, nb, d_inner, D_STATE)
        sfin_spec = pl.BlockSpec((n_prev + 1, nseq, d_inner, D_STATE), lambda b, c: (0, b, 0, 0))
    y, sfin, ctail = pl.pallas_call(
        functools.partial(_ssd_body, T=T, heads=heads, groups=SSD_GROUPS, nseq=nseq,
                          has_prev=prev_states is not None),
        out_shape=(jax.ShapeDtypeStruct((nb * L, d_inner), BF16),
                   jax.ShapeDtypeStruct(sfin_shape, F32),
                   jax.ShapeDtypeStruct((nb, 8, d_inner + bcw), F32)),
        grid=(nb // nseq, nc),
        in_specs=[
            pl.BlockSpec((rows, d_inner), rowmap(0)),
            pl.BlockSpec((rows, d_inner), rowmap(1)),
            pl.BlockSpec((rows, bcw), rowmap(2)),
            pl.BlockSpec((rows, V7X_LANES), rowmap(dt_col)),
            pl.BlockSpec((4, d_inner + bcw), const),
            pl.BlockSpec((1, d_inner + bcw), const),
            pl.BlockSpec((1, V7X_LANES), const),
            pl.BlockSpec((1, V7X_LANES), const),
            pl.BlockSpec((1, d_inner), const),
            pl.BlockSpec((1, d_inner), const),
            pl.BlockSpec((V7X_LANES, d_inner), const),
            pl.BlockSpec((nseq, 8, d_inner + bcw), per_b),
            pl.BlockSpec((None, nseq, d_inner, D_STATE), lambda b, c: (layer, b, 0, 0)),
        ] + prev_specs,
        out_specs=(pl.BlockSpec((rows, d_inner), lambda b, c: (b * nc + c, 0)),
                   sfin_spec,
                   pl.BlockSpec((nseq, 8, d_inner + bcw), per_b)),
        scratch_shapes=[pltpu.VMEM((nseq, T + 8, d_inner), F32), pltpu.VMEM((nseq, T + 8, bcw), F32),
                        pltpu.VMEM((nseq, D_STATE, d_inner), F32)],
        compiler_params=_cparams("parallel", "arbitrary"),
        name="ssd",
    )(zx, zx, zx, zx, conv_w, conv_b, dt_bias_rep, a_log_rep, d_exp, norm_g, e3, conv0, s0, *prev_args)
    return y, sfin, ctail


def _ssd_fused(x, g, w_in, conv_w, conv_b, dt_bias_rep, a_log_rep, d_exp, norm_g, e3, *, nb, L, T):
    d = x.shape[1]
    heads = d_exp.shape[-1] // SSD_HEADDIM
    d_inner = d_exp.shape[-1]
    bcw = 2 * SSD_GROUPS * D_STATE
    n_in = w_in.shape[1]
    assert L % (2 * T) == 0
    nk = L // (2 * T)
    const = lambda b, k: (0, 0)
    per_b = lambda b, k: (b, 0, 0)
    return pl.pallas_call(
        functools.partial(_ssd_fused_body, T=T, heads=heads, groups=SSD_GROUPS),
        out_shape=(jax.ShapeDtypeStruct((nb * L, d_inner), BF16),
                   jax.ShapeDtypeStruct((nb, d_inner, D_STATE), F32),
                   jax.ShapeDtypeStruct((nb, 8, d_inner + bcw), F32)),
        grid=(nb, nk),
        in_specs=[
            pl.BlockSpec((2 * T, d), lambda b, k: (b * nk + k, 0)),
            pl.BlockSpec((2 * T, d), lambda b, k: (b * nk + jnp.minimum(k + 1, nk - 1), 0)),
            pl.BlockSpec((1, d), const),
            pl.BlockSpec((d, n_in), const, pipeline_mode=pl.Buffered(1)),
            pl.BlockSpec((4, d_inner + bcw), const),
            pl.BlockSpec((1, d_inner + bcw), const),
            pl.BlockSpec((1, V7X_LANES), const),
            pl.BlockSpec((1, V7X_LANES), const),
            pl.BlockSpec((1, d_inner), const),
            pl.BlockSpec((1, d_inner), const),
            pl.BlockSpec((V7X_LANES, d_inner), const),
        ],
        out_specs=(pl.BlockSpec((2 * T, d_inner), lambda b, k: (b * nk + k, 0)),
                   pl.BlockSpec((1, d_inner, D_STATE), per_b),
                   pl.BlockSpec((1, 8, d_inner + bcw), per_b)),
        scratch_shapes=[pltpu.VMEM((T + 8, d_inner), F32), pltpu.VMEM((T + 8, bcw), F32),
                        pltpu.VMEM((D_STATE, d_inner), F32),
                        pltpu.VMEM((T, n_in), F32), pltpu.VMEM((T, n_in), F32)],
        compiler_params=_cparams("parallel", "arbitrary"),
        name="ssd_fused",
    )(x, x, g.reshape(1, d).astype(F32), w_in, conv_w, conv_b, dt_bias_rep, a_log_rep, d_exp, norm_g, e3)


def _ssd_w_in_pad(w_in):
    heads = w_in.shape[1] - (w_in.shape[1] // V7X_LANES) * V7X_LANES
    main = w_in[:, :w_in.shape[1] - heads]
    dt = w_in[:, w_in.shape[1] - heads:]
    return jnp.concatenate([main] + [dt] * (V7X_MXU_COLS // heads), axis=1)


def _in_proj_tile(n):
    return max(t for t in range(V7X_MXU_COLS, IN_PROJ_MAX_TILE + 1, V7X_MXU_COLS) if n % t == 0)


def _ssd_params(conv_w, conv_b, dt_bias, a_log, d_skip, norm_g):
    heads = dt_bias.shape[0]
    rep = V7X_LANES // heads
    d_inner = heads * SSD_HEADDIM
    src = jnp.arange(V7X_LANES)[:, None]
    dst_head = jnp.arange(d_inner)[None, :] // SSD_HEADDIM
    e3 = ((src % heads == dst_head) & (src < 3 * heads)).astype(BF16)
    return (conv_w.astype(F32), conv_b.reshape(1, -1).astype(F32),
            jnp.tile(dt_bias.reshape(1, heads), (1, rep)).astype(F32),
            jnp.tile(a_log.reshape(1, heads), (1, rep)).astype(F32),
            jnp.repeat(d_skip, SSD_HEADDIM).reshape(1, d_inner).astype(F32),
            norm_g.reshape(1, d_inner).astype(F32), e3)


SLOT = V7X_LANES
VT_SLOT = V_HEAD + 16
LOG2E = math.log2(math.e)
FLASH_SCORES_AHEAD = 2


def _mla_proj_body(x_ref, g_ref, wd_ref, qn_ref, kvn_ref, wuq_ref, wuqr_ref, wuk_ref, wuvt_ref, cos_ref, sin_ref,
                   q_ref, qt_ref, kk_ref, vt_ref, lat_ref, kr_ref, *, heads, q_lora, kv_lora, rope, scale):
    xn = _rms(x_ref[...], g_ref[...]).astype(BF16)
    down = jnp.dot(xn, wd_ref[...], preferred_element_type=F32)
    cq = down[:, :q_lora]
    ckv = down[:, q_lora:q_lora + kv_lora]
    krs = down[:, q_lora + kv_lora:q_lora + kv_lora + SLOT]
    krr = down[:, q_lora + kv_lora + SLOT:]
    cos = cos_ref[...]
    sin = sin_ref[...]
    cqn = _rms(cq, qn_ref[...]).astype(BF16)
    qp = jnp.dot(cqn, wuq_ref[...], preferred_element_type=F32)
    qr = jnp.dot(cqn, wuqr_ref[...], preferred_element_type=F32)
    lat = _rms(ckv, kvn_ref[...])
    lat_ref[...] = lat
    kr_rot = krs * cos + krr * sin
    kr_ref[...] = kr_rot[:, QK_NOPE:QK_NOPE + rope]
    latb = lat.astype(BF16)
    kn = jnp.dot(latb, wuk_ref[...], preferred_element_type=F32)
    for h in range(heads):
        sl = slice(h * SLOT, (h + 1) * SLOT)
        q_h = (qp[:, sl] * cos + qr[:, sl] * sin) * scale
        q_ref[:, sl] = q_h.astype(BF16)
        qt_ref[sl, :] = q_h.T.astype(BF16)
        kk_ref[:, sl] = (kn[:, sl] + kr_rot).astype(BF16)
    vt = lax.dot_general(wuvt_ref[...], latb, (((1,), (1,)), ((), ())),
                         preferred_element_type=F32)
    fill_rows = lax.broadcasted_iota(jnp.int32, (VT_SLOT - V_HEAD, vt.shape[1]), 0)
    ones_then_zeros = jnp.where(fill_rows == 0, 1.0, 0.0).astype(BF16)
    for h in range(heads):
        vt_ref[h * VT_SLOT:h * VT_SLOT + V_HEAD, :] = vt[h * V_HEAD:(h + 1) * V_HEAD, :].astype(BF16)
        vt_ref[h * VT_SLOT + V_HEAD:(h + 1) * VT_SLOT, :] = ones_then_zeros


def _mla_proj(x, g, w, cos_tab, sin_tab, tab_index, *, tm=ATTN_TILE):
    m, d = x.shape
    heads = w["wuq"].shape[1] // SLOT
    q_lora = w["qn"].shape[-1]
    kv_lora = w["kvn"].shape[-1]
    rope = w["rope"]
    tm = _row_tile(m, tm)
    const = lambda i: (0, 0)
    rowb = lambda i: (i, 0)
    full = lambda a: pl.BlockSpec(a.shape, const)
    return pl.pallas_call(
        functools.partial(_mla_proj_body, heads=heads, q_lora=q_lora, kv_lora=kv_lora, rope=rope,
                          scale=w["scale"]),
        out_shape=(jax.ShapeDtypeStruct((m, heads * SLOT), BF16),
                   jax.ShapeDtypeStruct((heads * SLOT, m), BF16),
                   jax.ShapeDtypeStruct((m, heads * SLOT), BF16),
                   jax.ShapeDtypeStruct((heads * VT_SLOT, m), BF16),
                   jax.ShapeDtypeStruct((m, kv_lora), F32),
                   jax.ShapeDtypeStruct((m, rope), F32)),
        grid=(m // tm,),
        in_specs=[pl.BlockSpec((tm, d), rowb), pl.BlockSpec((1, d), const),
                  full(w["wd"]), full(w["qn"]), full(w["kvn"]), full(w["wuq"]), full(w["wuqr"]),
                  full(w["wuk"]), full(w["wuvt"]),
                  pl.BlockSpec((tm, SLOT), lambda i: (tab_index(i), 0)),
                  pl.BlockSpec((tm, SLOT), lambda i: (tab_index(i), 0))],
        out_specs=(pl.BlockSpec((tm, heads * SLOT), rowb), pl.BlockSpec((heads * SLOT, tm), lambda i: (0, i)),
                   pl.BlockSpec((tm, heads * SLOT), rowb), pl.BlockSpec((heads * VT_SLOT, tm), lambda i: (0, i)),
                   pl.BlockSpec((tm, kv_lora), rowb), pl.BlockSpec((tm, rope), rowb)),
        compiler_params=_cparams("parallel"),
        name="mla_proj",
    )(x, g.reshape(1, d).astype(F32), w["wd"], w["qn"], w["kvn"], w["wuq"], w["wuqr"], w["wuk"], w["wuvt"],
      cos_tab, sin_tab)


def _mla_weights(w_down, q_norm, kv_norm, w_uq, w_uk, w_uv):
    q_lora = q_norm.shape[0]
    kv_lora = kv_norm.shape[0]
    heads = w_uq.shape[1]
    qk = w_uq.shape[2]
    rope = qk - QK_NOPE
    half = rope // 2
    pad = SLOT - qk

    def slot_pair(wr):
        z_lo = jnp.zeros(wr.shape[:-1] + (QK_NOPE,), wr.dtype)
        z_hi = jnp.zeros(wr.shape[:-1] + (pad,), wr.dtype)
        plain = jnp.concatenate([z_lo, wr, z_hi], axis=-1)
        rot = jnp.concatenate([z_lo, -wr[..., half:], wr[..., :half], z_hi], axis=-1)
        return plain, rot

    w_kr = w_down[:, q_lora + kv_lora:]
    kr_plain, kr_rot = slot_pair(w_kr)
    wd = jnp.concatenate([w_down[:, :q_lora + kv_lora], kr_plain, kr_rot], axis=1)
    uq_nope = jnp.concatenate([w_uq[..., :QK_NOPE], jnp.zeros(w_uq.shape[:2] + (SLOT - QK_NOPE,), w_uq.dtype)], -1)
    uq_plain, uq_rot = slot_pair(w_uq[..., QK_NOPE:])
    wuq = (uq_nope + uq_plain).reshape(q_lora, heads * SLOT)
    wuqr = uq_rot.reshape(q_lora, heads * SLOT)
    wuk = jnp.concatenate([w_uk, jnp.zeros(w_uk.shape[:2] + (SLOT - QK_NOPE,), w_uk.dtype)], -1)
    wuk = wuk.reshape(kv_lora, heads * SLOT)
    wuvt = w_uv.reshape(kv_lora, heads * V_HEAD).T
    eye_r = jnp.zeros((SLOT, SLOT), w_uk.dtype).at[QK_NOPE + jnp.arange(rope), jnp.arange(rope)].set(1.0)
    uk_t = jnp.transpose(w_uk, (1, 2, 0))
    uk_t = jnp.concatenate([uk_t, jnp.zeros((heads, SLOT - QK_NOPE, kv_lora), w_uk.dtype)], axis=1)
    wabs = jnp.concatenate([uk_t, jnp.broadcast_to(eye_r, (heads, SLOT, SLOT))], axis=2)
    uv = jnp.transpose(w_uv, (1, 0, 2)).reshape(heads // 2, 2, kv_lora, V_HEAD)
    z = jnp.zeros((heads // 2, kv_lora, V_HEAD), w_uv.dtype)
    wuv_bd = jnp.concatenate([jnp.concatenate([uv[:, 0], z], axis=2), jnp.concatenate([z, uv[:, 1]], axis=2)], axis=1)
    return dict(wd=wd.astype(BF16), qn=q_norm.reshape(1, -1).astype(F32), kvn=kv_norm.reshape(1, -1).astype(F32),
                wuq=wuq.astype(BF16), wuqr=wuqr.astype(BF16), wuk=wuk.astype(BF16), wuvt=wuvt.astype(BF16),
                wabs=wabs.astype(BF16), wuv_bd=wuv_bd.astype(BF16), rope=rope,
                scale=LOG2E / math.sqrt(qk))


def _rope_tables(positions, rope):
    half = rope // 2
    inv = ROPE_THETA ** (-jnp.arange(half, dtype=F32) * (2.0 / rope))
    ang = positions.astype(F32)[:, None] * inv[None, :]
    c, s = jnp.cos(ang), jnp.sin(ang)
    p = positions.shape[0]
    ones = jnp.ones((p, QK_NOPE), F32)
    zeros = jnp.zeros((p, QK_NOPE), F32)
    zpad = jnp.zeros((p, SLOT - QK_NOPE - rope), F32)
    return (jnp.concatenate([ones, c, c, zpad], axis=1), jnp.concatenate([zeros, s, s, zpad], axis=1))


def _headmm_body(x_ref, w_ref, o_ref):
    o_ref[...] = jnp.dot(x_ref[...], w_ref[...], preferred_element_type=F32).astype(o_ref.dtype)


def _headmm(x, w, *, out_dtype=BF16):
    m = x.shape[0]
    g, kb, nb = w.shape
    assert x.shape[1] == g * kb
    return pl.pallas_call(
        _headmm_body,
        out_shape=jax.ShapeDtypeStruct((m, g * nb), out_dtype),
        grid=(g,),
        in_specs=[pl.BlockSpec((m, kb), lambda h: (0, h)), pl.BlockSpec((None, kb, nb), lambda h: (h, 0, 0))],
        out_specs=pl.BlockSpec((m, nb), lambda h: (0, h)),
        compiler_params=_cparams("parallel"),
        name="headmm",
    )(x, w)


def _flash_body(qi_ref, ki_ref, qt_ref, k_ref, vt_ref, x_ref, wo_ref, o_ref, m_ref, acc_ref, *, heads, tq):
    p_idx = pl.program_id(1)
    qi = qi_ref[p_idx]
    kj = ki_ref[p_idx]

    @pl.when(kj == 0)
    def _():
        m_ref[...] = jnp.full(m_ref.shape, -jnp.inf, F32)
        acc_ref[...] = jnp.zeros(acc_ref.shape, F32)

    def step(blocks):
        key_i = lax.broadcasted_iota(jnp.int32, (tq, tq), 0)
        qry_i = lax.broadcasted_iota(jnp.int32, (tq, tq), 1)
        keep = key_i <= qry_i
        tasks = [(kb, diag, h) for kb, diag in blocks for h in range(heads)]

        def scores(task):
            kb, _, h = task
            k = k_ref[kb * tq:(kb + 1) * tq, h * SLOT:(h + 1) * SLOT]
            qt = qt_ref[h * SLOT:(h + 1) * SLOT, :]
            return jnp.dot(k, qt, preferred_element_type=F32)

        ahead = [scores(t) for t in tasks[:FLASH_SCORES_AHEAD]]
        for n, (kb, diag, h) in enumerate(tasks):
            s = ahead.pop(0)
            if n + FLASH_SCORES_AHEAD < len(tasks):
                ahead.append(scores(tasks[n + FLASH_SCORES_AHEAD]))
            if diag:
                s = jnp.where(keep, s, -jnp.inf)
            m_prev = m_ref[h:h + 1, :]
            m_new = jnp.maximum(m_prev, jnp.max(s, axis=0, keepdims=True))
            alpha = jnp.exp2(m_prev - m_new)
            p = jnp.exp2(s - m_new)
            m_ref[h:h + 1, :] = m_new
            vt = vt_ref[h * VT_SLOT:(h + 1) * VT_SLOT, kb * tq:(kb + 1) * tq]
            pv = jnp.dot(vt, p.astype(BF16), preferred_element_type=F32)
            acc_ref[h * VT_SLOT:(h + 1) * VT_SLOT, :] = acc_ref[h * VT_SLOT:(h + 1) * VT_SLOT, :] * alpha + pv

    def finish():
        outs = []
        for h in range(heads):
            l = acc_ref[h * VT_SLOT + V_HEAD:h * VT_SLOT + V_HEAD + 1, :]
            outs.append(acc_ref[h * VT_SLOT:h * VT_SLOT + V_HEAD, :] / l)
        o = jnp.concatenate(outs, axis=0).T.astype(BF16)
        o_ref[...] = x_ref[...] + jnp.dot(o, wo_ref[...], preferred_element_type=F32)

    @pl.when(2 * kj + 1 < qi)
    def _():
        step([(0, False), (1, False)])

    @pl.when(2 * kj + 1 == qi)
    def _():
        step([(0, False), (1, True)])
        finish()

    @pl.when(2 * kj == qi)
    def _():
        step([(0, True)])
        finish()


def _flash(qt, kk, vt, x, wo, *, nb, L, tq=ATTN_TILE):
    heads = kk.shape[1] // SLOT
    d = x.shape[1]
    tq = _row_tile(L // 2, tq)
    nq = L // tq
    nk = nq // 2
    assert L == nk * 2 * tq
    pairs = [(i, j) for i in range(nq) for j in range(i // 2 + 1)]
    qi_tab = jnp.asarray([p[0] for p in pairs], jnp.int32)
    ki_tab = jnp.asarray([p[1] for p in pairs], jnp.int32)
    grid_spec = pltpu.PrefetchScalarGridSpec(
        num_scalar_prefetch=2,
        grid=(nb, len(pairs)),
        in_specs=[
            pl.BlockSpec((heads * SLOT, tq), lambda b, p, qi, ki: (0, b * nq + qi[p])),
            pl.BlockSpec((2 * tq, heads * SLOT), lambda b, p, qi, ki: (b * nk + ki[p], 0)),
            pl.BlockSpec((heads * VT_SLOT, 2 * tq), lambda b, p, qi, ki: (0, b * nk + ki[p])),
            pl.BlockSpec((tq, d), lambda b, p, qi, ki: (b * nq + qi[p], 0)),
            pl.BlockSpec(wo.shape, lambda b, p, qi, ki: (0, 0), pipeline_mode=pl.Buffered(1)),
        ],
        out_specs=pl.BlockSpec((tq, d), lambda b, p, qi, ki: (b * nq + qi[p], 0)),
        scratch_shapes=[pltpu.VMEM((heads, tq), F32), pltpu.VMEM((heads * VT_SLOT, tq), F32)],
    )
    return pl.pallas_call(
        functools.partial(_flash_body, heads=heads, tq=tq),
        out_shape=jax.ShapeDtypeStruct((nb * L, d), F32),
        grid_spec=grid_spec,
        compiler_params=_cparams("parallel", "arbitrary"),
        name="mla_flash",
    )(qi_tab, ki_tab, qt, kk, vt, x, wo)


def _decode_body(pt_ref, q_ref, cnew_ref, rnew_ref, lat_hbm, ropet_hbm, o_ref,
                 cbuf, rbuf, cb, sem, *, layer, n_pages, page, chunk_pages, kv_lora, rope, heads, ls):
    b = pl.program_id(0)
    nb = pl.num_programs(0)
    slot = lax.rem(b, 2)
    rows = q_ref.shape[0]
    chunk = chunk_pages * page

    def page_copies(seq, sl):
        cps = []
        for i in range(n_pages):
            pg = pt_ref[seq, i]
            cps.append(pltpu.make_async_copy(lat_hbm.at[layer, pg], cbuf.at[sl, pl.ds(i * page, page), :], sem.at[0, sl]))
            cps.append(pltpu.make_async_copy(ropet_hbm.at[layer, pg], rbuf.at[sl, i], sem.at[1, sl]))
        return cps

    @pl.when(b == 0)
    def _():
        for cp in page_copies(0, 0):
            cp.start()

    @pl.when(b + 1 < nb)
    def _():
        for cp in page_copies(b + 1, 1 - slot):
            cp.start()

    for cp in page_copies(b, slot):
        cp.wait()

    q_lat = q_ref[:, :kv_lora]
    q_r = q_ref[:, kv_lora:kv_lora + rope]
    nt = (((1,), (1,)), ((), ()))
    n_chunks = n_pages // chunk_pages

    def scores(ck):
        c_b = cbuf[slot, ck * chunk:(ck + 1) * chunk, :].astype(BF16)
        cb[ck * chunk:(ck + 1) * chunk, :] = c_b
        r_b = jnp.concatenate([rbuf[slot, ck * chunk_pages + i].astype(BF16) for i in range(chunk_pages)],
                              axis=1)
        return (lax.dot_general(q_lat, c_b, nt, preferred_element_type=F32)
                + jnp.dot(q_r, r_b, preferred_element_type=F32))

    cn = cnew_ref[...].astype(BF16)
    rn = rnew_ref[...].astype(BF16)
    sn = (lax.dot_general(q_lat, cn, nt, preferred_element_type=F32)
          + lax.dot_general(q_r, rn, nt, preferred_element_type=F32))
    qpos = lax.broadcasted_iota(jnp.int32, (rows, ls), 0) // heads
    kpos = lax.broadcasted_iota(jnp.int32, (rows, ls), 1)
    sn = jnp.where(qpos >= kpos, sn, -jnp.inf)
    m = jnp.max(sn, axis=-1, keepdims=True)
    pn = jnp.exp2(sn - m)
    l = jnp.sum(pn, axis=-1, keepdims=True)
    acc = jnp.dot(pn.astype(BF16), cn, preferred_element_type=F32)

    s_next = scores(0)
    for ck in range(n_chunks):
        s = s_next
        if ck + 1 < n_chunks:
            s_next = scores(ck + 1)
        m_new = jnp.maximum(m, jnp.max(s, axis=-1, keepdims=True))
        alpha = jnp.exp2(m - m_new)
        p = jnp.exp2(s - m_new)
        l = alpha * l + jnp.sum(p, axis=-1, keepdims=True)
        acc = alpha * acc + jnp.dot(p.astype(BF16), cb[ck * chunk:(ck + 1) * chunk, :], preferred_element_type=F32)
        m = m_new
    o_ref[...] = (acc / l).astype(o_ref.dtype)


def _decode(q_ext, lat_new, kr_new, lat_pool, ropet_pool, page_table, *, layer, bs, ls, heads, new_row_offset,
            chunk_pages=DECODE_CHUNK_PAGES):
    kv_lora = lat_new.shape[1]
    rope = kr_new.shape[1]
    page = lat_pool.shape[2]
    n_pages = page_table.shape[1]
    chunk_pages = math.gcd(chunk_pages, n_pages)
    assert new_row_offset % ls == 0
    rows = ls * heads
    qw = q_ext.shape[1]
    grid_spec = pltpu.PrefetchScalarGridSpec(
        num_scalar_prefetch=1,
        grid=(bs,),
        in_specs=[pl.BlockSpec((rows, qw), lambda b, pt: (b, 0)),
                  pl.BlockSpec((ls, kv_lora), lambda b, pt: (new_row_offset // ls + b, 0)),
                  pl.BlockSpec((ls, rope), lambda b, pt: (new_row_offset // ls + b, 0)),
                  pl.BlockSpec(memory_space=pl.ANY),
                  pl.BlockSpec(memory_space=pl.ANY)],
        out_specs=pl.BlockSpec((rows, kv_lora), lambda b, pt: (b, 0)),
        scratch_shapes=[pltpu.VMEM((2, n_pages * page, kv_lora), F32),
                        pltpu.VMEM((2, n_pages, rope, page), F32),
                        pltpu.VMEM((n_pages * page, kv_lora), BF16),
                        pltpu.SemaphoreType.DMA((2, 2))],
    )
    return pl.pallas_call(
        functools.partial(_decode_body, layer=layer, n_pages=n_pages, page=page, chunk_pages=chunk_pages,
                          kv_lora=kv_lora, rope=rope, heads=heads, ls=ls),
        out_shape=jax.ShapeDtypeStruct((bs * rows, kv_lora), BF16),
        grid_spec=grid_spec,
        compiler_params=_cparams("arbitrary"),
        name="mla_decode",
    )(page_table, q_ext, lat_new, kr_new, lat_pool, ropet_pool)


def kernel(x_prompt, x_sample, mem_prompt, state_ssm, state_conv, cache_mla_latent, cache_mla_rope_k, cache_mem_k, cache_mem_v, page_table, norm_mix, norm_mem, norm_memkv, norm_ffn, norm_final, ssd_w_in, ssd_conv_w, ssd_conv_b, ssd_dt_bias, ssd_a_log, ssd_d, ssd_norm, ssd_w_out, mla_w_down, mla_q_norm, mla_kv_norm, mla_w_uq, mla_w_uk, mla_w_uv, mla_w_o, mem_w_q, mem_w_kv, mem_w_o, mlp_w_up, mlp_w_down):
    bp, lp, d = x_prompt.shape
    bs, ls, _ = x_sample.shape
    mp, ms = bp * lp, bs * ls
    depth = norm_mix.shape[0]
    n_mem = mem_prompt.shape[1]
    past_len = page_table.shape[1] * cache_mla_latent.shape[2]
    mla_heads = mla_w_uq.shape[2]
    rope = mla_w_uq.shape[3] - QK_NOPE
    ssd_heads = ssd_dt_bias.shape[1]
    d_inner = ssd_heads * SSD_HEADDIM

    x = None
    x_in = (x_prompt.reshape(mp, d), x_sample.reshape(ms, d))
    mem_rows = mem_prompt.reshape(bp * n_mem, d)
    ropet_pool = jnp.swapaxes(cache_mla_rope_k, 2, 3)
    ssm0 = state_ssm.reshape(state_ssm.shape[0], bs, d_inner, D_STATE)

    pos = jnp.concatenate([jnp.arange(lp), jnp.tile(past_len + jnp.arange(ls), bs)])
    cos_tab, sin_tab = _rope_tables(pos, rope)
    proj_tm = _row_tile(ms, ATTN_TILE)
    assert lp % proj_tm == 0
    npt, tpl = mp // proj_tm, lp // proj_tm
    tab_index = lambda i: jnp.where(i < npt, i % tpl, tpl + (i - npt))

    mem_rows_p = _row_tile(lp, ROW_TILE)
    mem_nb = _row_tile(bs, MEM_DEC_SEQS)

    p_ssm, p_conv, p_lat, p_rk, p_mk, p_mv = [], [], [], [], [], []
    s_conv, s_lat, s_rk = [], [], []
    s_ssm = None
    for i in range(depth):
        j = i // 2
        if i % 2 == 0:
            w_pad = _ssd_w_in_pad(ssd_w_in[j]).astype(BF16)
            prm = _ssd_params(ssd_conv_w[j], ssd_conv_b[j], ssd_dt_bias[j], ssd_a_log[j], ssd_d[j], ssd_norm[j])
            x_head, x_tail = x_in if x is None else (x, x[mp:])
            y_p, st_p, ct_p = _ssd_fused(x_head, norm_mix[i], w_pad, *prm, nb=bp, L=lp, T=math.gcd(SSD_CHUNK, lp))
            zx_s = _mm(x_tail, w_pad, g=norm_mix[i], tn=_in_proj_tile(w_pad.shape[1]), name="ssd_in")
            conv0 = jnp.pad(state_conv[j], ((0, 0), (8 - state_conv.shape[2], 0), (0, 0)))
            t_s = math.gcd(SSD_CHUNK, ls)
            y_s, s_ssm, ct_s = _ssd(zx_s, *prm, conv0, ssm0, row_offset=0, nb=bs, L=ls, T=t_s, layer=j,
                                    nseq=math.gcd(SSD_SAMPLE_SEQS, bs) if ls == t_s else 1,
                                    prev_states=None if s_ssm is None else s_ssm.reshape(-1, bs, d_inner, D_STATE))
            x = _mm(y_p, ssd_w_out[j].astype(BF16), x_tail=y_s, name="ssd_out",
                    **(dict(res=x_head, res_tail=x_tail) if x is None else dict(res=x)))
            x_head, x_tail = x, x[mp:]
            kc = state_conv.shape[2]
            p_ssm.append(st_p.reshape(bp, ssd_heads, SSD_HEADDIM, D_STATE))
            p_conv.append(ct_p[:, 8 - kc:, :])
            s_conv.append(ct_s[:, 8 - kc:, :])
        else:
            w = _mla_weights(mla_w_down[j], mla_q_norm[j], mla_kv_norm[j], mla_w_uq[j], mla_w_uk[j], mla_w_uv[j])
            assert x is not None, "the first layer is an SSD layer"
            q, qt, kk, vt, lat, kr = _mla_proj(x, norm_mix[i], w, cos_tab, sin_tab, tab_index, tm=proj_tm)
            w_o = mla_w_o[j].astype(BF16)
            x_head = _flash(qt, kk, vt, x, w_o, nb=bp, L=lp)
            q_ext = _headmm(q[mp:], w["wabs"]).reshape(ms * mla_heads, -1)
            o_lat = _decode(q_ext, lat, kr, cache_mla_latent, ropet_pool, page_table, layer=j, bs=bs, ls=ls,
                            heads=mla_heads, new_row_offset=mp)
            o_s = _headmm(o_lat.reshape(ms, -1), w["wuv_bd"])
            x_tail = _mm(o_s, w_o, res=x[mp:], name="mla_out")
            p_lat.append(lat[:mp].reshape(bp, lp, -1))
            s_lat.append(lat[mp:].reshape(bs, ls, -1))
            p_rk.append(kr[:mp].reshape(bp, lp, -1))
            s_rk.append(kr[mp:].reshape(bs, ls, -1))
        kv = _mm(mem_rows, mem_w_kv[i].astype(BF16), g=norm_memkv[i], name="mem_kv")
        kp = kv[:, :d].reshape(bp, n_mem, d)
        vp = kv[:, d:].reshape(bp, n_mem, d)
        p_mk.append(kp.reshape(bp, n_mem, MEM_HEADS, d // MEM_HEADS))
        p_mv.append(vp.reshape(bp, n_mem, MEM_HEADS, d // MEM_HEADS))
        wq, wo = mem_w_q[i].astype(BF16), mem_w_o[i].astype(BF16)
        x_p = _mem_fused(x_head, norm_mem[i], wq, kp.astype(BF16), vp.astype(BF16), wo, n_seq=bp, L=lp,
                         rows=mem_rows_p)
        qm_s = _mm(x_tail, wq, g=norm_mem[i], out_dtype=BF16, name="mem_q")
        o_s = _memattn_dec(qm_s, cache_mem_k, cache_mem_v, layer=i, row_offset=0, n_seq=bs, rows=ls, nb=mem_nb)
        x_s = _mm(o_s, wo, res=x_tail, name="mem_out")
        x = _mlp(x_p, norm_ffn[i], mlp_w_up[i].astype(BF16), mlp_w_down[i].astype(BF16), x_tail=x_s,
                 final_g=norm_final if i == depth - 1 else None)
    y_p, y_s = x
    return (y_p.reshape(bp, lp, d), y_s.reshape(bs, ls, d),
            jnp.stack(p_ssm), jnp.stack(p_conv), jnp.stack(p_lat), jnp.stack(p_rk), jnp.stack(p_mk), jnp.stack(p_mv),
            s_ssm.reshape(state_ssm.shape), jnp.stack(s_conv), jnp.stack(s_lat), jnp.stack(s_rk))
```

```python
import functools
import math

import jax
import jax.numpy as jnp
from jax import lax
from jax.experimental import pallas as pl
from jax.experimental.pallas import tpu as pltpu

F32 = jnp.float32
BF16 = jnp.bfloat16

EPS = 1e-6
ROPE_THETA = 10000.0

V7X_LANES = 128
V7X_MXU_COLS = 256
V7X_VMEM_BYTES = 64 * 1024 * 1024
V7X_VMEM_LIMIT_BYTES = V7X_VMEM_BYTES * 7 // 8

SSD_HEADDIM = 64
SSD_GROUPS = 8
D_STATE = 128
SSD_CHUNK = 128
QK_NOPE = 64
V_HEAD = 64
MEM_HEADS = 4

ROW_TILE = 1024
COL_TILE = 1024
IN_PROJ_MAX_TILE = 1536
ATTN_TILE = 512
DECODE_CHUNK_PAGES = 16
MEM_DEC_SEQS = 4
SSD_SAMPLE_SEQS = 2
SSD_FUSED_CHUNKS = 4


def _cparams(*sem):
    return pltpu.CompilerParams(dimension_semantics=sem, vmem_limit_bytes=V7X_VMEM_LIMIT_BYTES)


def _rms(x, g):
    return x * lax.rsqrt(jnp.mean(x * x, axis=-1, keepdims=True) + EPS) * g


def _row_tile(m, pref):
    t = min(pref, m)
    assert m % t == 0, (m, t)
    return t


def _mm_body(*refs, norm, res, res_split, head_tiles):
    it = iter(refs)
    x_ref = next(it)
    t_ref = next(it) if head_tiles is not None else None
    g_ref = next(it) if norm else None
    w_ref = next(it)
    r_ref = next(it) if res else None
    rt_ref = next(it) if res_split else None
    o_ref = next(it)
    xn_ref = next(it)

    def stage(src_ref):
        x = src_ref[...].astype(F32)
        if norm:
            x = _rms(x, g_ref[...])
        xn_ref[...] = x.astype(BF16)

    first = pl.program_id(1) == 0
    if head_tiles is None:
        pl.when(first)(lambda: stage(x_ref))
    else:
        in_head = pl.program_id(0) < head_tiles
        pl.when(first & in_head)(lambda: stage(x_ref))
        pl.when(first & jnp.logical_not(in_head))(lambda: stage(t_ref))

    acc = jnp.dot(xn_ref[...], w_ref[...], preferred_element_type=F32)
    if res_split:
        in_head = pl.program_id(0) < head_tiles

        @pl.when(in_head)
        def _():
            o_ref[...] = (r_ref[...] + acc).astype(o_ref.dtype)

        @pl.when(jnp.logical_not(in_head))
        def _():
            o_ref[...] = (rt_ref[...] + acc).astype(o_ref.dtype)
    else:
        if res:
            acc = r_ref[...] + acc
        o_ref[...] = acc.astype(o_ref.dtype)


def _mm(x, w, *, x_tail=None, g=None, res=None, res_tail=None, out_dtype=F32, tm=ROW_TILE, tn=COL_TILE, name="mm"):
    m, k = x.shape
    k2, n = w.shape
    assert k == k2
    assert res_tail is None or (x_tail is not None and res is not None
                                and res.shape[0] == m and res_tail.shape[0] == x_tail.shape[0])
    head_tiles = None
    if x_tail is None:
        tm = _row_tile(m, tm)
        in_specs = [pl.BlockSpec((tm, k), lambda i, j: (i, 0))]
        args = [x]
    else:
        m_tail = x_tail.shape[0]
        tm = math.gcd(math.gcd(m, m_tail), tm)
        head_tiles = m // tm
        m = m + m_tail
        in_specs = [pl.BlockSpec((tm, k), lambda i, j: (jnp.minimum(i, head_tiles - 1), 0)),
                    pl.BlockSpec((tm, k), lambda i, j: (jnp.maximum(i - head_tiles, 0), 0))]
        args = [x, x_tail]
    tn = _row_tile(n, tn)
    if g is not None:
        in_specs.append(pl.BlockSpec((1, k), lambda i, j: (0, 0)))
        args.append(g.reshape(1, k).astype(F32))
    in_specs.append(pl.BlockSpec((k, tn), lambda i, j: (0, j)))
    args.append(w)
    if res_tail is not None:
        in_specs += [pl.BlockSpec((tm, tn), lambda i, j: (jnp.minimum(i, head_tiles - 1), j)),
                     pl.BlockSpec((tm, tn), lambda i, j: (jnp.maximum(i - head_tiles, 0), j))]
        args += [res, res_tail]
    elif res is not None:
        in_specs.append(pl.BlockSpec((tm, tn), lambda i, j: (i, j)))
        args.append(res)
    return pl.pallas_call(
        functools.partial(_mm_body, norm=g is not None, res=res is not None, res_split=res_tail is not None,
                          head_tiles=head_tiles),
        out_shape=jax.ShapeDtypeStruct((m, n), out_dtype),
        grid=(m // tm, n // tn),
        in_specs=in_specs,
        out_specs=pl.BlockSpec((tm, tn), lambda i, j: (i, j)),
        scratch_shapes=[pltpu.VMEM((tm, k), BF16)],
        compiler_params=_cparams("parallel", "arbitrary"),
        name=name,
    )(*args)


def _mlp_body(*refs, head_tiles, final_norm):
    x_ref = refs[0]
    t_ref = refs[1] if head_tiles is not None else None
    refs = refs[1 + int(head_tiles is not None):]
    g_ref, wu_ref, wd_ref = refs[:3]
    fg_ref = refs[3] if final_norm else None
    refs = refs[3 + int(final_norm):]
    o_ref = refs[0]
    ot_ref = refs[1] if final_norm else None
    xn_ref, acc_ref = refs[1 + int(final_norm):]
    k = pl.program_id(1)

    def stage(src_ref):
        x = src_ref[...]
        xn_ref[...] = _rms(x, g_ref[...]).astype(BF16)
        acc_ref[...] = x

    if head_tiles is None:
        pl.when(k == 0)(lambda: stage(x_ref))
    else:
        in_head = pl.program_id(0) < head_tiles
        pl.when((k == 0) & in_head)(lambda: stage(x_ref))
        pl.when((k == 0) & jnp.logical_not(in_head))(lambda: stage(t_ref))

    h = jnp.dot(xn_ref[...], wu_ref[...], preferred_element_type=F32)
    h = jnp.square(jnp.maximum(h, 0.0)).astype(BF16)
    acc_ref[...] += jnp.dot(h, wd_ref[...], preferred_element_type=F32)

    last = k == pl.num_programs(1) - 1
    if not final_norm:
        @pl.when(last)
        def _():
            o_ref[...] = acc_ref[...]
    else:
        in_head = pl.program_id(0) < head_tiles

        @pl.when(last & in_head)
        def _():
            o_ref[...] = _rms(acc_ref[...], fg_ref[...])

        @pl.when(last & jnp.logical_not(in_head))
        def _():
            ot_ref[...] = _rms(acc_ref[...], fg_ref[...])


def _mlp(x, g, w_up, w_down, *, x_tail=None, final_g=None, tm=ROW_TILE, tf=COL_TILE):
    m, d = x.shape
    ff = w_up.shape[1]
    head_tiles = None
    assert final_g is None or x_tail is not None
    if x_tail is None:
        tm = _row_tile(m, tm)
        row_specs, row_args = [pl.BlockSpec((tm, d), lambda i, k: (i, 0))], [x]
    else:
        m_tail = x_tail.shape[0]
        tm = math.gcd(math.gcd(m, m_tail), tm)
        head_tiles = m // tm
        m = m + m_tail
        row_specs = [pl.BlockSpec((tm, d), lambda i, k: (jnp.minimum(i, head_tiles - 1), 0)),
                     pl.BlockSpec((tm, d), lambda i, k: (jnp.maximum(i - head_tiles, 0), 0))]
        row_args = [x, x_tail]
    tf = _row_tile(ff, tf)
    if final_g is None:
        out_shape = jax.ShapeDtypeStruct((m, d), F32)
        out_specs = pl.BlockSpec((tm, d), lambda i, k: (i, 0))
        fg_specs, fg_args = [], []
    else:
        out_shape = (jax.ShapeDtypeStruct(x.shape, F32), jax.ShapeDtypeStruct(x_tail.shape, F32))
        out_specs = tuple(row_specs)
        fg_specs, fg_args = [pl.BlockSpec((1, d), lambda i, k: (0, 0))], [final_g.reshape(1, d).astype(F32)]
    return pl.pallas_call(
        functools.partial(_mlp_body, head_tiles=head_tiles, final_norm=final_g is not None),
        out_shape=out_shape,
        grid=(m // tm, ff // tf),
        in_specs=row_specs + [
            pl.BlockSpec((1, d), lambda i, k: (0, 0)),
            pl.BlockSpec((d, tf), lambda i, k: (0, k)),
            pl.BlockSpec((tf, d), lambda i, k: (k, 0)),
        ] + fg_specs,
        out_specs=out_specs,
        scratch_shapes=[pltpu.VMEM((tm, d), BF16), pltpu.VMEM((tm, d), F32)],
        compiler_params=_cparams("parallel" if final_g is None else "arbitrary", "arbitrary"),
        name="mlp",
    )(*row_args, g.reshape(1, d).astype(F32), w_up, w_down, *fg_args)


def _mem_fused_body(x_ref, g_ref, wq_ref, k_ref, v_ref, wo_ref, o_ref, *, heads, scale):
    x = x_ref[...]
    q = jnp.dot(_rms(x, g_ref[...]).astype(BF16), wq_ref[...], preferred_element_type=F32)
    dh = q.shape[-1] // heads
    outs = []
    for h in range(heads):
        q_h = q[:, h * dh:(h + 1) * dh].astype(BF16)
        k_h = k_ref[:, h * dh:(h + 1) * dh]
        v_h = v_ref[:, h * dh:(h + 1) * dh]
        s = lax.dot_general(q_h, k_h, (((1,), (1,)), ((), ())), preferred_element_type=F32) * scale
        s = s - jnp.max(s, axis=-1, keepdims=True)
        p = jnp.exp(s)
        p = p / jnp.sum(p, axis=-1, keepdims=True)
        outs.append(jnp.dot(p.astype(BF16), v_h, preferred_element_type=F32).astype(BF16))
    o = jnp.concatenate(outs, axis=1)
    o_ref[...] = x + jnp.dot(o, wo_ref[...], preferred_element_type=F32)


def _mem_fused(x, g, wq, k, v, wo, *, n_seq, L, rows):
    d = x.shape[-1]
    n_mem = k.shape[-2]
    per_seq = L // rows
    const = lambda s: (0, 0)
    weight = lambda: pl.BlockSpec((d, d), const, pipeline_mode=pl.Buffered(1))
    return pl.pallas_call(
        functools.partial(_mem_fused_body, heads=MEM_HEADS, scale=1.0 / math.sqrt(d // MEM_HEADS)),
        out_shape=jax.ShapeDtypeStruct((n_seq * L, d), F32),
        grid=(n_seq * per_seq,),
        in_specs=[pl.BlockSpec((rows, d), lambda s: (s, 0)),
                  pl.BlockSpec((1, d), const),
                  weight(),
                  pl.BlockSpec((None, n_mem, d), lambda s: (s // per_seq, 0, 0)),
                  pl.BlockSpec((None, n_mem, d), lambda s: (s // per_seq, 0, 0)),
                  weight()],
        out_specs=pl.BlockSpec((rows, d), lambda s: (s, 0)),
        compiler_params=_cparams("parallel"),
        name="mem_fused",
    )(x, g.reshape(1, d).astype(F32), wq, k, v, wo)


def _memattn_dec_body(q_ref, k_ref, v_ref, o_ref, *, nb, rows, heads, scale):
    dh = q_ref.shape[-1] // heads
    n_mem = k_ref.shape[1]
    qrow_head = lax.broadcasted_iota(jnp.int32, (heads * rows, n_mem * heads), 0) // rows
    key_head = lax.broadcasted_iota(jnp.int32, (heads * rows, n_mem * heads), 1) % heads
    own = qrow_head == key_head
    outs = []
    q_all = q_ref[...].astype(F32)
    for i in range(nb):
        qi = q_all[i * rows:(i + 1) * rows, :]
        qs = jnp.concatenate([qi[:, h * dh:(h + 1) * dh] for h in range(heads)], axis=0).astype(BF16)
        k2 = k_ref[i].reshape(n_mem * heads, dh).astype(BF16)
        v2 = v_ref[i].reshape(n_mem * heads, dh).astype(BF16)
        s = lax.dot_general(qs, k2, (((1,), (1,)), ((), ())), preferred_element_type=F32) * scale
        s = jnp.where(own, s, -jnp.inf)
        s = s - jnp.max(s, axis=-1, keepdims=True)
        p = jnp.exp(s)
        p = p / jnp.sum(p, axis=-1, keepdims=True)
        o = jnp.dot(p.astype(BF16), v2, preferred_element_type=F32)
        outs.append(jnp.concatenate([o[h * rows:(h + 1) * rows, :] for h in range(heads)], axis=1))
    o_ref[...] = jnp.concatenate(outs, axis=0).astype(o_ref.dtype)


def _memattn_dec(q, k, v, *, layer, row_offset, n_seq, rows, nb):
    d = q.shape[-1]
    _, _, n_mem, heads, dh = k.shape
    blk = nb * rows
    assert row_offset % blk == 0 and n_seq % nb == 0
    kv_spec = pl.BlockSpec((None, nb, n_mem, heads, dh), lambda s: (layer, s, 0, 0, 0))
    return pl.pallas_call(
        functools.partial(_memattn_dec_body, nb=nb, rows=rows, heads=heads, scale=1.0 / math.sqrt(dh)),
        out_shape=jax.ShapeDtypeStruct((n_seq * rows, d), BF16),
        grid=(n_seq // nb,),
        in_specs=[pl.BlockSpec((blk, d), lambda s: (row_offset // blk + s, 0)), kv_spec, kv_spec],
        out_specs=pl.BlockSpec((blk, d), lambda s: (s, 0)),
        compiler_params=_cparams("parallel"),
        name="memattn_dec",
    )(q, k, v)


def _split3_lanes(v, heads):
    lane = lax.broadcasted_iota(jnp.int32, v.shape, 1)
    hi = v.astype(BF16).astype(F32)
    r1 = v - hi
    mid = r1.astype(BF16).astype(F32)
    lo = r1 - mid
    out = jnp.where(lane < heads, hi, jnp.where(lane < 2 * heads, mid, jnp.where(lane < 3 * heads, lo, 0.0)))
    return out.astype(BF16)


def _ssd_init(conv0_ref, s0_ref, extx_ref, extbc_ref, st_ref):
    d_inner = extx_ref.shape[-1]
    if conv0_ref is not None:
        extx_ref[0:8, :] = conv0_ref[:, 0:d_inner]
        extbc_ref[0:8, :] = conv0_ref[:, d_inner:]
        st_ref[...] = s0_ref[...].T
    else:
        extx_ref[0:8, :] = jnp.zeros((8, d_inner), F32)
        extbc_ref[0:8, :] = jnp.zeros((8, extbc_ref.shape[-1]), F32)
        st_ref[...] = jnp.zeros(st_ref.shape, F32)


def _ssd_finish(sfin_ref, ctail_ref, extx_ref, extbc_ref, st_ref):
    d_inner = extx_ref.shape[-1]
    sfin_ref[...] = st_ref[...].T
    ctail_ref[:, 0:d_inner] = extx_ref[0:8, :]
    ctail_ref[:, d_inner:] = extbc_ref[0:8, :]


def _scan_chunk(z, x_raw, bc_raw, dt_raw, cw_ref, cb_ref, dtb_ref, alog_ref, dsk_ref, ng_ref, e3_ref,
                y_ref, y_row0, extx_ref, extbc_ref, st_ref, *, T, heads, groups):
    d_inner = extx_ref.shape[-1]
    hpg = heads // groups
    gw = d_inner // groups
    hd = d_inner // heads
    n = D_STATE
    gate = z * jax.nn.sigmoid(z)

    extx_ref[8:8 + T, :] = x_raw
    extbc_ref[8:8 + T, :] = bc_raw

    def conv(ext_ref, lo, width):
        cur = ext_ref[8:8 + T, :]
        tail = ext_ref[0:8, :]
        r8 = lax.broadcasted_iota(jnp.int32, (8, width), 0)
        kw = cw_ref.shape[0]
        acc = cb_ref[:, lo:lo + width] + cur * cw_ref[kw - 1:kw, lo:lo + width]
        for s in range(1, kw):
            rolled = pltpu.roll(cur, s, axis=0)
            head = jnp.where(r8 < s, pltpu.roll(tail, s, axis=0), rolled[0:8])
            shifted = head if T == 8 else jnp.concatenate([head, rolled[8:]], axis=0)
            acc = acc + shifted * cw_ref[kw - 1 - s:kw - s, lo:lo + width]
        return acc * jax.nn.sigmoid(acc)

    xc = conv(extx_ref, 0, d_inner)
    bcc = conv(extbc_ref, d_inner, extbc_ref.shape[-1])
    extx_ref[0:8, :] = extx_ref[T:T + 8, :]
    extbc_ref[0:8, :] = extbc_ref[T:T + 8, :]

    dtr = dt_raw + dtb_ref[...]
    dtv = jnp.maximum(dtr, 0.0) + jnp.log1p(jnp.exp(-jnp.abs(dtr)))
    a = dtv * (-jnp.exp(alog_ref[...]))
    row = lax.broadcasted_iota(jnp.int32, a.shape, 0)
    acs = a
    sh = 1
    while sh < T:
        acs = acs + jnp.where(row >= sh, pltpu.roll(acs, sh, axis=0), 0.0)
        sh *= 2
    a_last = acs[T - 1:T, :]
    e3 = e3_ref[...]

    per_head = [dtv, jnp.exp(a_last - acs), jnp.exp(acs)]
    if T % V7X_LANES == 0:
        stacked = jnp.concatenate([_split3_lanes(v, heads) for v in per_head], axis=0)
        expanded = jnp.dot(stacked, e3, preferred_element_type=F32)
        dt_e, dend_e, eacs_e = expanded[0:T], expanded[T:2 * T], expanded[2 * T:3 * T]
    else:
        dt_e, dend_e, eacs_e = [jnp.dot(_split3_lanes(v, heads), e3, preferred_element_type=F32) for v in per_head]
    xdt = xc * dt_e
    xdt_b = xdt.astype(BF16)
    xw_b = (xdt * dend_e).astype(BF16)
    cdec_e = eacs_e[T - 1:T, :]

    acs2 = acs * LOG2E
    if T < V7X_LANES:
        acs_p = jnp.concatenate([acs2, jnp.zeros((V7X_LANES - T, V7X_LANES), F32)], axis=0)
    else:
        acs_p = acs2
    acs_t = acs_p.T
    ti = lax.broadcasted_iota(jnp.int32, (T, T), 0)
    si = lax.broadcasted_iota(jnp.int32, (T, T), 1)
    causal = ti >= si
    lane_g = lax.broadcasted_iota(jnp.int32, (T, gw), 1)

    for g in range(groups):
        bg = bcc[:, g * n:(g + 1) * n]
        cg = bcc[:, groups * n + g * n: groups * n + (g + 1) * n].astype(BF16)
        bg_b = bg.astype(BF16)
        gmat = lax.dot_general(cg, bg_b, (((1,), (1,)), ((), ())), preferred_element_type=F32)
        st_g = st_ref[:, g * gw:(g + 1) * gw]
        y_g = jnp.dot(cg, st_g.astype(BF16), preferred_element_type=F32) * eacs_e[:, g * gw:(g + 1) * gw]
        xg = xdt_b[:, g * gw:(g + 1) * gw]
        m_heads, x_heads = [], []
        for j in range(hpg):
            h = g * hpg + j
            seg = acs2[:, h:h + 1] - acs_t[h:h + 1, 0:T]
            decay = jnp.exp2(jnp.where(causal, seg, -jnp.inf))
            m_heads.append((gmat * decay).astype(BF16))
            x_heads.append(jnp.where((lane_g >= j * hd) & (lane_g < (j + 1) * hd), xg, jnp.zeros_like(xg)))
        if T % V7X_LANES == 0:
            y_g = y_g + jnp.dot(jnp.concatenate(m_heads, axis=1), jnp.concatenate(x_heads, axis=0),
                                preferred_element_type=F32)
        else:
            for m_h, x_h in zip(m_heads, x_heads):
                y_g = y_g + jnp.dot(m_h, x_h, preferred_element_type=F32)
        cs_t = jnp.dot(bg.T.astype(BF16), xw_b[:, g * gw:(g + 1) * gw], preferred_element_type=F32)
        st_ref[:, g * gw:(g + 1) * gw] = st_g * cdec_e[:, g * gw:(g + 1) * gw] + cs_t
        y_g = y_g + xc[:, g * gw:(g + 1) * gw] * dsk_ref[:, g * gw:(g + 1) * gw]
        y_g = y_g * gate[:, g * gw:(g + 1) * gw]
        y_g = _rms(y_g, ng_ref[:, g * gw:(g + 1) * gw])
        y_ref[y_row0:y_row0 + T, g * gw:(g + 1) * gw] = y_g.astype(y_ref.dtype)


def _ssd_body(*refs, T, heads, groups, nseq, has_prev):
    (z_ref, x_ref, bc_ref, dt_ref, cw_ref, cb_ref, dtb_ref, alog_ref, dsk_ref, ng_ref, e3_ref,
     conv0_ref, s0_ref) = refs[:13]
    prev_ref = refs[13] if has_prev else None
    y_ref, sfin_ref, ctail_ref, extx_ref, extbc_ref, st_ref = refs[13 + int(has_prev):]
    c = pl.program_id(1)

    @pl.when(c == 0)
    def _():
        for i in range(nseq):
            _ssd_init(conv0_ref.at[i], s0_ref.at[i], extx_ref.at[i], extbc_ref.at[i], st_ref.at[i])

    for i in range(nseq):
        rows = slice(i * T, (i + 1) * T)
        _scan_chunk(z_ref[rows, :], x_ref[rows, :], bc_ref[rows, :], dt_ref[rows, :],
                    cw_ref, cb_ref, dtb_ref, alog_ref, dsk_ref, ng_ref, e3_ref,
                    y_ref, i * T, extx_ref.at[i], extbc_ref.at[i], st_ref.at[i], T=T, heads=heads, groups=groups)

    @pl.when(c == pl.num_programs(1) - 1)
    def _():
        for i in range(nseq):
            if has_prev:
                n_prev = prev_ref.shape[0]
                sfin_ref[0:n_prev, i] = prev_ref[:, i]
                sfin_i = sfin_ref.at[n_prev, i]
            else:
                sfin_i = sfin_ref.at[i]
            _ssd_finish(sfin_i, ctail_ref.at[i], extx_ref.at[i], extbc_ref.at[i], st_ref.at[i])


def _ssd_fused_body(xcur_ref, xnext_ref, g_ref, win_ref, cw_ref, cb_ref, dtb_ref, alog_ref, dsk_ref, ng_ref, e3_ref,
                    y_ref, sfin_ref, ctail_ref,
                    extx_ref, extbc_ref, st_ref, zxa_ref, zxb_ref, *, T, heads, groups, chunks):
    k = pl.program_id(1)
    d_inner = extx_ref.shape[-1]
    bcw = extbc_ref.shape[-1]

    def in_proj(rows):
        xn = _rms(rows, g_ref[...]).astype(BF16)
        return jnp.dot(xn, win_ref[...], preferred_element_type=F32)

    def scan(zx_ref, y_row0):
        _scan_chunk(zx_ref[:, 0:d_inner], zx_ref[:, d_inner:2 * d_inner],
                    zx_ref[:, 2 * d_inner:2 * d_inner + bcw],
                    zx_ref[:, 2 * d_inner + bcw:2 * d_inner + bcw + V7X_LANES],
                    cw_ref, cb_ref, dtb_ref, alog_ref, dsk_ref, ng_ref, e3_ref,
                    y_ref, y_row0, extx_ref, extbc_ref, st_ref, T=T, heads=heads, groups=groups)

    @pl.when(k == 0)
    def _():
        _ssd_init(None, None, extx_ref, extbc_ref, st_ref)
        zxa_ref[...] = in_proj(xcur_ref[0:T, :])

    bufs = (zxa_ref, zxb_ref)
    for c in range(chunks):
        if c + 1 < chunks:
            bufs[(c + 1) % 2][...] = in_proj(xcur_ref[(c + 1) * T:(c + 2) * T, :])
        else:
            bufs[(c + 1) % 2][...] = in_proj(xnext_ref[0:T, :])
        scan(bufs[c % 2], c * T)
    pl.when(k == pl.num_programs(1) - 1)(
        functools.partial(_ssd_finish, sfin_ref.at[0], ctail_ref.at[0], extx_ref, extbc_ref, st_ref))


def _ssd(zx, conv_w, conv_b, dt_bias_rep, a_log_rep, d_exp, norm_g, e3, conv0, s0, *, row_offset, nb, L, T,
         layer, nseq=1, prev_states=None):
    heads = d_exp.shape[-1] // SSD_HEADDIM
    d_inner = d_exp.shape[-1]
    bcw = 2 * SSD_GROUPS * D_STATE
    assert bcw == d_inner, "column blocks are indexed in units of d_inner"
    nc = L // T
    rows = nseq * T
    assert nb % nseq == 0 and row_offset % rows == 0 and (nseq == 1 or nc == 1)
    rb0 = row_offset // rows
    rowmap = lambda col: (lambda b, c: (rb0 + b * nc + c, col))
    const = lambda b, c: (0, 0)
    per_b = lambda b, c: (b, 0, 0)
    dt_col = (2 * d_inner + bcw) // V7X_LANES
    if prev_states is None:
        prev_args, prev_specs = [], []
        sfin_shape = (nb, d_inner, D_STATE)
        sfin_spec = pl.BlockSpec((nseq, d_inner, D_STATE), per_b)
    else:
        n_prev = prev_states.shape[0]
        prev_args = [prev_states]
        prev_specs = [pl.BlockSpec((n_prev, nseq, d_inner, D_STATE), lambda b, c: (0, b, 0, 0))]
        sfin_shape = (n_prev + 1, nb, d_inner, D_STATE)
        sfin_spec = pl.BlockSpec((n_prev + 1, nseq, d_inner, D_STATE), lambda b, c: (0, b, 0, 0))
    y, sfin, ctail = pl.pallas_call(
        functools.partial(_ssd_body, T=T, heads=heads, groups=SSD_GROUPS, nseq=nseq,
                          has_prev=prev_states is not None),
        out_shape=(jax.ShapeDtypeStruct((nb * L, d_inner), BF16),
                   jax.ShapeDtypeStruct(sfin_shape, F32),
                   jax.ShapeDtypeStruct((nb, 8, d_inner + bcw), F32)),
        grid=(nb // nseq, nc),
        in_specs=[
            pl.BlockSpec((rows, d_inner), rowmap(0)),
            pl.BlockSpec((rows, d_inner), rowmap(1)),
            pl.BlockSpec((rows, bcw), rowmap(2)),
            pl.BlockSpec((rows, V7X_LANES), rowmap(dt_col)),
            pl.BlockSpec((4, d_inner + bcw), const),
            pl.BlockSpec((1, d_inner + bcw), const),
            pl.BlockSpec((1, V7X_LANES), const),
            pl.BlockSpec((1, V7X_LANES), const),
            pl.BlockSpec((1, d_inner), const),
            pl.BlockSpec((1, d_inner), const),
            pl.BlockSpec((V7X_LANES, d_inner), const),
            pl.BlockSpec((nseq, 8, d_inner + bcw), per_b),
            pl.BlockSpec((None, nseq, d_inner, D_STATE), lambda b, c: (layer, b, 0, 0)),
        ] + prev_specs,
        out_specs=(pl.BlockSpec((rows, d_inner), lambda b, c: (b * nc + c, 0)),
                   sfin_spec,
                   pl.BlockSpec((nseq, 8, d_inner + bcw), per_b)),
        scratch_shapes=[pltpu.VMEM((nseq, T + 8, d_inner), F32), pltpu.VMEM((nseq, T + 8, bcw), F32),
                        pltpu.VMEM((nseq, D_STATE, d_inner), F32)],
        compiler_params=_cparams("parallel", "arbitrary"),
        name="ssd",
    )(zx, zx, zx, zx, conv_w, conv_b, dt_bias_rep, a_log_rep, d_exp, norm_g, e3, conv0, s0, *prev_args)
    return y, sfin, ctail


def _ssd_fused(x, g, w_in, conv_w, conv_b, dt_bias_rep, a_log_rep, d_exp, norm_g, e3, *, nb, L, T):
    d = x.shape[1]
    heads = d_exp.shape[-1] // SSD_HEADDIM
    d_inner = d_exp.shape[-1]
    bcw = 2 * SSD_GROUPS * D_STATE
    n_in = w_in.shape[1]
    chunks = max(c for c in range(2, SSD_FUSED_CHUNKS + 1, 2) if L % (c * T) == 0)
    rows = chunks * T
    nk = L // rows
    const = lambda b, k: (0, 0)
    per_b = lambda b, k: (b, 0, 0)
    return pl.pallas_call(
        functools.partial(_ssd_fused_body, T=T, heads=heads, groups=SSD_GROUPS, chunks=chunks),
        out_shape=(jax.ShapeDtypeStruct((nb * L, d_inner), BF16),
                   jax.ShapeDtypeStruct((nb, d_inner, D_STATE), F32),
                   jax.ShapeDtypeStruct((nb, 8, d_inner + bcw), F32)),
        grid=(nb, nk),
        in_specs=[
            pl.BlockSpec((rows, d), lambda b, k: (b * nk + k, 0)),
            pl.BlockSpec((rows, d), lambda b, k: (b * nk + jnp.minimum(k + 1, nk - 1), 0)),
            pl.BlockSpec((1, d), const),
            pl.BlockSpec((d, n_in), const, pipeline_mode=pl.Buffered(1)),
            pl.BlockSpec((4, d_inner + bcw), const),
            pl.BlockSpec((1, d_inner + bcw), const),
            pl.BlockSpec((1, V7X_LANES), const),
            pl.BlockSpec((1, V7X_LANES), const),
            pl.BlockSpec((1, d_inner), const),
            pl.BlockSpec((1, d_inner), const),
            pl.BlockSpec((V7X_LANES, d_inner), const),
        ],
        out_specs=(pl.BlockSpec((rows, d_inner), lambda b, k: (b * nk + k, 0)),
                   pl.BlockSpec((1, d_inner, D_STATE), per_b),
                   pl.BlockSpec((1, 8, d_inner + bcw), per_b)),
        scratch_shapes=[pltpu.VMEM((T + 8, d_inner), F32), pltpu.VMEM((T + 8, bcw), F32),
                        pltpu.VMEM((D_STATE, d_inner), F32),
                        pltpu.VMEM((T, n_in), F32), pltpu.VMEM((T, n_in), F32)],
        compiler_params=_cparams("parallel", "arbitrary"),
        name="ssd_fused",
    )(x, x, g.reshape(1, d).astype(F32), w_in, conv_w, conv_b, dt_bias_rep, a_log_rep, d_exp, norm_g, e3)


def _ssd_w_in_pad(w_in):
    heads = w_in.shape[1] - (w_in.shape[1] // V7X_LANES) * V7X_LANES
    main = w_in[:, :w_in.shape[1] - heads]
    dt = w_in[:, w_in.shape[1] - heads:]
    return jnp.concatenate([main] + [dt] * (V7X_MXU_COLS // heads), axis=1)


def _in_proj_tile(n):
    return max(t for t in range(V7X_MXU_COLS, IN_PROJ_MAX_TILE + 1, V7X_MXU_COLS) if n % t == 0)


def _ssd_params(conv_w, conv_b, dt_bias, a_log, d_skip, norm_g):
    heads = dt_bias.shape[0]
    rep = V7X_LANES // heads
    d_inner = heads * SSD_HEADDIM
    src = jnp.arange(V7X_LANES)[:, None]
    dst_head = jnp.arange(d_inner)[None, :] // SSD_HEADDIM
    e3 = ((src % heads == dst_head) & (src < 3 * heads)).astype(BF16)
    return (conv_w.astype(F32), conv_b.reshape(1, -1).astype(F32),
            jnp.tile(dt_bias.reshape(1, heads), (1, rep)).astype(F32),
            jnp.tile(a_log.reshape(1, heads), (1, rep)).astype(F32),
            jnp.repeat(d_skip, SSD_HEADDIM).reshape(1, d_inner).astype(F32),
            norm_g.reshape(1, d_inner).astype(F32), e3)


SLOT = V7X_LANES
VT_SLOT = V_HEAD + 16
LOG2E = math.log2(math.e)
FLASH_SCORES_AHEAD = 2


def _mla_proj_body(x_ref, g_ref, wd_ref, qn_ref, kvn_ref, wuq_ref, wuqr_ref, wuk_ref, wuvt_ref, cos_ref, sin_ref,
                   q_ref, qt_ref, kk_ref, vt_ref, lat_ref, kr_ref, *, heads, q_lora, kv_lora, rope, scale):
    xn = _rms(x_ref[...], g_ref[...]).astype(BF16)
    down = jnp.dot(xn, wd_ref[...], preferred_element_type=F32)
    cq = down[:, :q_lora]
    ckv = down[:, q_lora:q_lora + kv_lora]
    krs = down[:, q_lora + kv_lora:q_lora + kv_lora + SLOT]
    krr = down[:, q_lora + kv_lora + SLOT:]
    cos = cos_ref[...]
    sin = sin_ref[...]
    cqn = _rms(cq, qn_ref[...]).astype(BF16)
    qp = jnp.dot(cqn, wuq_ref[...], preferred_element_type=F32)
    qr = jnp.dot(cqn, wuqr_ref[...], preferred_element_type=F32)
    lat = _rms(ckv, kvn_ref[...])
    lat_ref[...] = lat
    kr_rot = krs * cos + krr * sin
    kr_ref[...] = kr_rot[:, QK_NOPE:QK_NOPE + rope]
    latb = lat.astype(BF16)
    kn = jnp.dot(latb, wuk_ref[...], preferred_element_type=F32)
    for h in range(heads):
        sl = slice(h * SLOT, (h + 1) * SLOT)
        q_h = (qp[:, sl] * cos + qr[:, sl] * sin) * scale
        q_ref[:, sl] = q_h.astype(BF16)
        qt_ref[sl, :] = q_h.T.astype(BF16)
        kk_ref[:, sl] = (kn[:, sl] + kr_rot).astype(BF16)
    vt = lax.dot_general(wuvt_ref[...], latb, (((1,), (1,)), ((), ())),
                         preferred_element_type=F32)
    fill_rows = lax.broadcasted_iota(jnp.int32, (VT_SLOT - V_HEAD, vt.shape[1]), 0)
    ones_then_zeros = jnp.where(fill_rows == 0, 1.0, 0.0).astype(BF16)
    for h in range(heads):
        vt_ref[h * VT_SLOT:h * VT_SLOT + V_HEAD, :] = vt[h * V_HEAD:(h + 1) * V_HEAD, :].astype(BF16)
        vt_ref[h * VT_SLOT + V_HEAD:(h + 1) * VT_SLOT, :] = ones_then_zeros


def _mla_proj(x, g, w, cos_tab, sin_tab, tab_index, *, tm=ATTN_TILE):
    m, d = x.shape
    heads = w["wuq"].shape[1] // SLOT
    q_lora = w["qn"].shape[-1]
    kv_lora = w["kvn"].shape[-1]
    rope = w["rope"]
    tm = _row_tile(m, tm)
    const = lambda i: (0, 0)
    rowb = lambda i: (i, 0)
    full = lambda a: pl.BlockSpec(a.shape, const)
    return pl.pallas_call(
        functools.partial(_mla_proj_body, heads=heads, q_lora=q_lora, kv_lora=kv_lora, rope=rope,
                          scale=w["scale"]),
        out_shape=(jax.ShapeDtypeStruct((m, heads * SLOT), BF16),
                   jax.ShapeDtypeStruct((heads * SLOT, m), BF16),
                   jax.ShapeDtypeStruct((m, heads * SLOT), BF16),
                   jax.ShapeDtypeStruct((heads * VT_SLOT, m), BF16),
                   jax.ShapeDtypeStruct((m, kv_lora), F32),
                   jax.ShapeDtypeStruct((m, rope), F32)),
        grid=(m // tm,),
        in_specs=[pl.BlockSpec((tm, d), rowb), pl.BlockSpec((1, d), const),
                  full(w["wd"]), full(w["qn"]), full(w["kvn"]), full(w["wuq"]), full(w["wuqr"]),
                  full(w["wuk"]), full(w["wuvt"]),
                  pl.BlockSpec((tm, SLOT), lambda i: (tab_index(i), 0)),
                  pl.BlockSpec((tm, SLOT), lambda i: (tab_index(i), 0))],
        out_specs=(pl.BlockSpec((tm, heads * SLOT), rowb), pl.BlockSpec((heads * SLOT, tm), lambda i: (0, i)),
                   pl.BlockSpec((tm, heads * SLOT), rowb), pl.BlockSpec((heads * VT_SLOT, tm), lambda i: (0, i)),
                   pl.BlockSpec((tm, kv_lora), rowb), pl.BlockSpec((tm, rope), rowb)),
        compiler_params=_cparams("parallel"),
        name="mla_proj",
    )(x, g.reshape(1, d).astype(F32), w["wd"], w["qn"], w["kvn"], w["wuq"], w["wuqr"], w["wuk"], w["wuvt"],
      cos_tab, sin_tab)


def _mla_weights(w_down, q_norm, kv_norm, w_uq, w_uk, w_uv):
    q_lora = q_norm.shape[0]
    kv_lora = kv_norm.shape[0]
    heads = w_uq.shape[1]
    qk = w_uq.shape[2]
    rope = qk - QK_NOPE
    half = rope // 2
    pad = SLOT - qk

    def slot_pair(wr):
        z_lo = jnp.zeros(wr.shape[:-1] + (QK_NOPE,), wr.dtype)
        z_hi = jnp.zeros(wr.shape[:-1] + (pad,), wr.dtype)
        plain = jnp.concatenate([z_lo, wr, z_hi], axis=-1)
        rot = jnp.concatenate([z_lo, -wr[..., half:], wr[..., :half], z_hi], axis=-1)
        return plain, rot

    w_kr = w_down[:, q_lora + kv_lora:]
    kr_plain, kr_rot = slot_pair(w_kr)
    wd = jnp.concatenate([w_down[:, :q_lora + kv_lora], kr_plain, kr_rot], axis=1)
    uq_nope = jnp.concatenate([w_uq[..., :QK_NOPE], jnp.zeros(w_uq.shape[:2] + (SLOT - QK_NOPE,), w_uq.dtype)], -1)
    uq_plain, uq_rot = slot_pair(w_uq[..., QK_NOPE:])
    wuq = (uq_nope + uq_plain).reshape(q_lora, heads * SLOT)
    wuqr = uq_rot.reshape(q_lora, heads * SLOT)
    wuk = jnp.concatenate([w_uk, jnp.zeros(w_uk.shape[:2] + (SLOT - QK_NOPE,), w_uk.dtype)], -1)
    wuk = wuk.reshape(kv_lora, heads * SLOT)
    wuvt = w_uv.reshape(kv_lora, heads * V_HEAD).T
    eye_r = jnp.zeros((SLOT, SLOT), w_uk.dtype).at[QK_NOPE + jnp.arange(rope), jnp.arange(rope)].set(1.0)
    uk_t = jnp.transpose(w_uk, (1, 2, 0))
    uk_t = jnp.concatenate([uk_t, jnp.zeros((heads, SLOT - QK_NOPE, kv_lora), w_uk.dtype)], axis=1)
    wabs = jnp.concatenate([uk_t, jnp.broadcast_to(eye_r, (heads, SLOT, SLOT))], axis=2)
    uv = jnp.transpose(w_uv, (1, 0, 2)).reshape(heads // 2, 2, kv_lora, V_HEAD)
    z = jnp.zeros((heads // 2, kv_lora, V_HEAD), w_uv.dtype)
    wuv_bd = jnp.concatenate([jnp.concatenate([uv[:, 0], z], axis=2), jnp.concatenate([z, uv[:, 1]], axis=2)], axis=1)
    return dict(wd=wd.astype(BF16), qn=q_norm.reshape(1, -1).astype(F32), kvn=kv_norm.reshape(1, -1).astype(F32),
                wuq=wuq.astype(BF16), wuqr=wuqr.astype(BF16), wuk=wuk.astype(BF16), wuvt=wuvt.astype(BF16),
                wabs=wabs.astype(BF16), wuv_bd=wuv_bd.astype(BF16), rope=rope,
                scale=LOG2E / math.sqrt(qk))


def _rope_tables(positions, rope):
    half = rope // 2
    inv = ROPE_THETA ** (-jnp.arange(half, dtype=F32) * (2.0 / rope))
    ang = positions.astype(F32)[:, None] * inv[None, :]
    c, s = jnp.cos(ang), jnp.sin(ang)
    p = positions.shape[0]
    ones = jnp.ones((p, QK_NOPE), F32)
    zeros = jnp.zeros((p, QK_NOPE), F32)
    zpad = jnp.zeros((p, SLOT - QK_NOPE - rope), F32)
    return (jnp.concatenate([ones, c, c, zpad], axis=1), jnp.concatenate([zeros, s, s, zpad], axis=1))


def _headmm_body(x_ref, w_ref, o_ref):
    o_ref[...] = jnp.dot(x_ref[...], w_ref[...], preferred_element_type=F32).astype(o_ref.dtype)


def _headmm(x, w, *, out_dtype=BF16):
    m = x.shape[0]
    g, kb, nb = w.shape
    assert x.shape[1] == g * kb
    return pl.pallas_call(
        _headmm_body,
        out_shape=jax.ShapeDtypeStruct((m, g * nb), out_dtype),
        grid=(g,),
        in_specs=[pl.BlockSpec((m, kb), lambda h: (0, h)), pl.BlockSpec((None, kb, nb), lambda h: (h, 0, 0))],
        out_specs=pl.BlockSpec((m, nb), lambda h: (0, h)),
        compiler_params=_cparams("parallel"),
        name="headmm",
    )(x, w)


def _flash_body(qi_ref, ki_ref, qt_ref, k_ref, vt_ref, o_ref, m_ref, acc_ref, *, heads, tq):
    p_idx = pl.program_id(1)
    qi = qi_ref[p_idx]
    kj = ki_ref[p_idx]

    @pl.when(kj == 0)
    def _():
        m_ref[...] = jnp.full(m_ref.shape, -jnp.inf, F32)
        acc_ref[...] = jnp.zeros(acc_ref.shape, F32)

    def step(blocks):
        key_i = lax.broadcasted_iota(jnp.int32, (tq, tq), 0)
        qry_i = lax.broadcasted_iota(jnp.int32, (tq, tq), 1)
        keep = key_i <= qry_i
        tasks = [(kb, diag, h) for kb, diag in blocks for h in range(heads)]

        def scores(task):
            kb, _, h = task
            k = k_ref[kb * tq:(kb + 1) * tq, h * SLOT:(h + 1) * SLOT]
            qt = qt_ref[h * SLOT:(h + 1) * SLOT, :]
            return jnp.dot(k, qt, preferred_element_type=F32)

        ahead = [scores(t) for t in tasks[:FLASH_SCORES_AHEAD]]
        for n, (kb, diag, h) in enumerate(tasks):
            s = ahead.pop(0)
            if n + FLASH_SCORES_AHEAD < len(tasks):
                ahead.append(scores(tasks[n + FLASH_SCORES_AHEAD]))
            if diag:
                s = jnp.where(keep, s, -jnp.inf)
            m_prev = m_ref[h:h + 1, :]
            m_new = jnp.maximum(m_prev, jnp.max(s, axis=0, keepdims=True))
            alpha = jnp.exp2(m_prev - m_new)
            p = jnp.exp2(s - m_new)
            m_ref[h:h + 1, :] = m_new
            vt = vt_ref[h * VT_SLOT:(h + 1) * VT_SLOT, kb * tq:(kb + 1) * tq]
            pv = jnp.dot(vt, p.astype(BF16), preferred_element_type=F32)
            acc_ref[h * VT_SLOT:(h + 1) * VT_SLOT, :] = acc_ref[h * VT_SLOT:(h + 1) * VT_SLOT, :] * alpha + pv

    def finish():
        outs = []
        for h in range(heads):
            l = acc_ref[h * VT_SLOT + V_HEAD:h * VT_SLOT + V_HEAD + 1, :]
            outs.append(acc_ref[h * VT_SLOT:h * VT_SLOT + V_HEAD, :] / l)
        o_ref[...] = jnp.concatenate(outs, axis=0).T.astype(o_ref.dtype)

    @pl.when(2 * kj + 1 < qi)
    def _():
        step([(0, False), (1, False)])

    @pl.when(2 * kj + 1 == qi)
    def _():
        step([(0, False), (1, True)])
        finish()

    @pl.when(2 * kj == qi)
    def _():
        step([(0, True)])
        finish()


def _flash(qt, kk, vt, *, nb, L, tq=ATTN_TILE):
    heads = kk.shape[1] // SLOT
    tq = _row_tile(L // 2, tq)
    nq = L // tq
    nk = nq // 2
    assert L == nk * 2 * tq
    pairs = [(i, j) for i in range(nq) for j in range(i // 2 + 1)]
    qi_tab = jnp.asarray([p[0] for p in pairs], jnp.int32)
    ki_tab = jnp.asarray([p[1] for p in pairs], jnp.int32)
    grid_spec = pltpu.PrefetchScalarGridSpec(
        num_scalar_prefetch=2,
        grid=(nb, len(pairs)),
        in_specs=[
            pl.BlockSpec((heads * SLOT, tq), lambda b, p, qi, ki: (0, b * nq + qi[p])),
            pl.BlockSpec((2 * tq, heads * SLOT), lambda b, p, qi, ki: (b * nk + ki[p], 0)),
            pl.BlockSpec((heads * VT_SLOT, 2 * tq), lambda b, p, qi, ki: (0, b * nk + ki[p])),
        ],
        out_specs=pl.BlockSpec((tq, heads * V_HEAD), lambda b, p, qi, ki: (b * nq + qi[p], 0)),
        scratch_shapes=[pltpu.VMEM((heads, tq), F32), pltpu.VMEM((heads * VT_SLOT, tq), F32)],
    )
    return pl.pallas_call(
        functools.partial(_flash_body, heads=heads, tq=tq),
        out_shape=jax.ShapeDtypeStruct((nb * L, heads * V_HEAD), BF16),
        grid_spec=grid_spec,
        compiler_params=_cparams("parallel", "arbitrary"),
        name="mla_flash",
    )(qi_tab, ki_tab, qt, kk, vt)


def _decode_body(pt_ref, q_ref, cnew_ref, rnew_ref, lat_hbm, ropet_hbm, o_ref,
                 cbuf, rbuf, cb, sem, *, layer, n_pages, page, chunk_pages, kv_lora, rope, heads, ls):
    b = pl.program_id(0)
    nb = pl.num_programs(0)
    slot = lax.rem(b, 2)
    rows = q_ref.shape[0]
    chunk = chunk_pages * page

    def page_copies(seq, sl):
        cps = []
        for i in range(n_pages):
            pg = pt_ref[seq, i]
            cps.append(pltpu.make_async_copy(lat_hbm.at[layer, pg], cbuf.at[sl, pl.ds(i * page, page), :], sem.at[0, sl]))
            cps.append(pltpu.make_async_copy(ropet_hbm.at[layer, pg], rbuf.at[sl, i], sem.at[1, sl]))
        return cps

    @pl.when(b == 0)
    def _():
        for cp in page_copies(0, 0):
            cp.start()

    @pl.when(b + 1 < nb)
    def _():
        for cp in page_copies(b + 1, 1 - slot):
            cp.start()

    for cp in page_copies(b, slot):
        cp.wait()

    q_lat = q_ref[:, :kv_lora]
    q_r = q_ref[:, kv_lora:kv_lora + rope]
    nt = (((1,), (1,)), ((), ()))
    n_chunks = n_pages // chunk_pages

    def scores(ck):
        c_b = cbuf[slot, ck * chunk:(ck + 1) * chunk, :].astype(BF16)
        cb[ck * chunk:(ck + 1) * chunk, :] = c_b
        r_b = jnp.concatenate([rbuf[slot, ck * chunk_pages + i].astype(BF16) for i in range(chunk_pages)],
                              axis=1)
        return (lax.dot_general(q_lat, c_b, nt, preferred_element_type=F32)
                + jnp.dot(q_r, r_b, preferred_element_type=F32))

    cn = cnew_ref[...].astype(BF16)
    rn = rnew_ref[...].astype(BF16)
    sn = (lax.dot_general(q_lat, cn, nt, preferred_element_type=F32)
          + lax.dot_general(q_r, rn, nt, preferred_element_type=F32))
    qpos = lax.broadcasted_iota(jnp.int32, (rows, ls), 0) // heads
    kpos = lax.broadcasted_iota(jnp.int32, (rows, ls), 1)
    sn = jnp.where(qpos >= kpos, sn, -jnp.inf)
    m = jnp.max(sn, axis=-1, keepdims=True)
    pn = jnp.exp2(sn - m)
    l = jnp.sum(pn, axis=-1, keepdims=True)
    acc = jnp.dot(pn.astype(BF16), cn, preferred_element_type=F32)

    s_next = scores(0)
    for ck in range(n_chunks):
        s = s_next
        if ck + 1 < n_chunks:
            s_next = scores(ck + 1)
        m_new = jnp.maximum(m, jnp.max(s, axis=-1, keepdims=True))
        alpha = jnp.exp2(m - m_new)
        p = jnp.exp2(s - m_new)
        l = alpha * l + jnp.sum(p, axis=-1, keepdims=True)
        acc = alpha * acc + jnp.dot(p.astype(BF16), cb[ck * chunk:(ck + 1) * chunk, :], preferred_element_type=F32)
        m = m_new
    o_ref[...] = (acc / l).astype(o_ref.dtype)


def _decode(q_ext, lat_new, kr_new, lat_pool, ropet_pool, page_table, *, layer, bs, ls, heads, new_row_offset,
            chunk_pages=DECODE_CHUNK_PAGES):
    kv_lora = lat_new.shape[1]
    rope = kr_new.shape[1]
    page = lat_pool.shape[2]
    n_pages = page_table.shape[1]
    chunk_pages = math.gcd(chunk_pages, n_pages)
    assert new_row_offset % ls == 0
    rows = ls * heads
    qw = q_ext.shape[1]
    grid_spec = pltpu.PrefetchScalarGridSpec(
        num_scalar_prefetch=1,
        grid=(bs,),
        in_specs=[pl.BlockSpec((rows, qw), lambda b, pt: (b, 0)),
                  pl.BlockSpec((ls, kv_lora), lambda b, pt: (new_row_offset // ls + b, 0)),
                  pl.BlockSpec((ls, rope), lambda b, pt: (new_row_offset // ls + b, 0)),
                  pl.BlockSpec(memory_space=pl.ANY),
                  pl.BlockSpec(memory_space=pl.ANY)],
        out_specs=pl.BlockSpec((rows, kv_lora), lambda b, pt: (b, 0)),
        scratch_shapes=[pltpu.VMEM((2, n_pages * page, kv_lora), F32),
                        pltpu.VMEM((2, n_pages, rope, page), F32),
                        pltpu.VMEM((n_pages * page, kv_lora), BF16),
                        pltpu.SemaphoreType.DMA((2, 2))],
    )
    return pl.pallas_call(
        functools.partial(_decode_body, layer=layer, n_pages=n_pages, page=page, chunk_pages=chunk_pages,
                          kv_lora=kv_lora, rope=rope, heads=heads, ls=ls),
        out_shape=jax.ShapeDtypeStruct((bs * rows, kv_lora), BF16),
        grid_spec=grid_spec,
        compiler_params=_cparams("arbitrary"),
        name="mla_decode",
    )(page_table, q_ext, lat_new, kr_new, lat_pool, ropet_pool)


def kernel(x_prompt, x_sample, mem_prompt, state_ssm, state_conv, cache_mla_latent, cache_mla_rope_k, cache_mem_k, cache_mem_v, page_table, norm_mix, norm_mem, norm_memkv, norm_ffn, norm_final, ssd_w_in, ssd_conv_w, ssd_conv_b, ssd_dt_bias, ssd_a_log, ssd_d, ssd_norm, ssd_w_out, mla_w_down, mla_q_norm, mla_kv_norm, mla_w_uq, mla_w_uk, mla_w_uv, mla_w_o, mem_w_q, mem_w_kv, mem_w_o, mlp_w_up, mlp_w_down):
    bp, lp, d = x_prompt.shape
    bs, ls, _ = x_sample.shape
    mp, ms = bp * lp, bs * ls
    depth = norm_mix.shape[0]
    n_mem = mem_prompt.shape[1]
    past_len = page_table.shape[1] * cache_mla_latent.shape[2]
    mla_heads = mla_w_uq.shape[2]
    rope = mla_w_uq.shape[3] - QK_NOPE
    ssd_heads = ssd_dt_bias.shape[1]
    d_inner = ssd_heads * SSD_HEADDIM

    x = None
    x_in = (x_prompt.reshape(mp, d), x_sample.reshape(ms, d))
    mem_rows = mem_prompt.reshape(bp * n_mem, d)
    ropet_pool = jnp.swapaxes(cache_mla_rope_k, 2, 3)
    ssm0 = state_ssm.reshape(state_ssm.shape[0], bs, d_inner, D_STATE)

    pos = jnp.concatenate([jnp.arange(lp), jnp.tile(past_len + jnp.arange(ls), bs)])
    cos_tab, sin_tab = _rope_tables(pos, rope)
    proj_tm = _row_tile(ms, ATTN_TILE)
    assert lp % proj_tm == 0
    npt, tpl = mp // proj_tm, lp // proj_tm
    tab_index = lambda i: jnp.where(i < npt, i % tpl, tpl + (i - npt))

    mem_rows_p = _row_tile(lp, ROW_TILE)
    mem_nb = _row_tile(bs, MEM_DEC_SEQS)

    p_ssm, p_conv, p_lat, p_rk, p_mk, p_mv = [], [], [], [], [], []
    s_conv, s_lat, s_rk = [], [], []
    s_ssm = None
    for i in range(depth):
        j = i // 2
        if i % 2 == 0:
            w_pad = _ssd_w_in_pad(ssd_w_in[j]).astype(BF16)
            prm = _ssd_params(ssd_conv_w[j], ssd_conv_b[j], ssd_dt_bias[j], ssd_a_log[j], ssd_d[j], ssd_norm[j])
            x_head, x_tail = x_in if x is None else (x, x[mp:])
            y_p, st_p, ct_p = _ssd_fused(x_head, norm_mix[i], w_pad, *prm, nb=bp, L=lp, T=math.gcd(SSD_CHUNK, lp))
            zx_s = _mm(x_tail, w_pad, g=norm_mix[i], tn=_in_proj_tile(w_pad.shape[1]), name="ssd_in")
            conv0 = jnp.pad(state_conv[j], ((0, 0), (8 - state_conv.shape[2], 0), (0, 0)))
            t_s = math.gcd(SSD_CHUNK, ls)
            y_s, s_ssm, ct_s = _ssd(zx_s, *prm, conv0, ssm0, row_offset=0, nb=bs, L=ls, T=t_s, layer=j,
                                    nseq=math.gcd(SSD_SAMPLE_SEQS, bs) if ls == t_s else 1,
                                    prev_states=None if s_ssm is None else s_ssm.reshape(-1, bs, d_inner, D_STATE))
            x = _mm(y_p, ssd_w_out[j].astype(BF16), x_tail=y_s, name="ssd_out",
                    **(dict(res=x_head, res_tail=x_tail) if x is None else dict(res=x)))
            kc = state_conv.shape[2]
            p_ssm.append(st_p.reshape(bp, ssd_heads, SSD_HEADDIM, D_STATE))
            p_conv.append(ct_p[:, 8 - kc:, :])
            s_conv.append(ct_s[:, 8 - kc:, :])
        else:
            w = _mla_weights(mla_w_down[j], mla_q_norm[j], mla_kv_norm[j], mla_w_uq[j], mla_w_uk[j], mla_w_uv[j])
            q, qt, kk, vt, lat, kr = _mla_proj(x, norm_mix[i], w, cos_tab, sin_tab, tab_index, tm=proj_tm)
            o_p = _flash(qt, kk, vt, nb=bp, L=lp)
            q_ext = _headmm(q[mp:], w["wabs"]).reshape(ms * mla_heads, -1)
            o_lat = _decode(q_ext, lat, kr, cache_mla_latent, ropet_pool, page_table, layer=j, bs=bs, ls=ls,
                            heads=mla_heads, new_row_offset=mp)
            o_s = _headmm(o_lat.reshape(ms, -1), w["wuv_bd"])
            x = _mm(o_p, mla_w_o[j].astype(BF16), x_tail=o_s, res=x, name="mla_out")
            p_lat.append(lat[:mp].reshape(bp, lp, -1))
            s_lat.append(lat[mp:].reshape(bs, ls, -1))
            p_rk.append(kr[:mp].reshape(bp, lp, -1))
            s_rk.append(kr[mp:].reshape(bs, ls, -1))
        kv = _mm(mem_rows, mem_w_kv[i].astype(BF16), g=norm_memkv[i], name="mem_kv")
        kp = kv[:, :d].reshape(bp, n_mem, d)
        vp = kv[:, d:].reshape(bp, n_mem, d)
        p_mk.append(kp.reshape(bp, n_mem, MEM_HEADS, d // MEM_HEADS))
        p_mv.append(vp.reshape(bp, n_mem, MEM_HEADS, d // MEM_HEADS))
        wq, wo = mem_w_q[i].astype(BF16), mem_w_o[i].astype(BF16)
        x_p = _mem_fused(x, norm_mem[i], wq, kp.astype(BF16), vp.astype(BF16), wo, n_seq=bp, L=lp, rows=mem_rows_p)
        x_s = x[mp:]
        qm_s = _mm(x_s, wq, g=norm_mem[i], out_dtype=BF16, name="mem_q")
        o_s = _memattn_dec(qm_s, cache_mem_k, cache_mem_v, layer=i, row_offset=0, n_seq=bs, rows=ls, nb=mem_nb)
        x_s = _mm(o_s, wo, res=x_s, name="mem_out")
        x = _mlp(x_p, norm_ffn[i], mlp_w_up[i].astype(BF16), mlp_w_down[i].astype(BF16), x_tail=x_s,
                 final_g=norm_final if i == depth - 1 else None)
    y_p, y_s = x
    return (y_p.reshape(bp, lp, d), y_s.reshape(bs, ls, d),
            jnp.stack(p_ssm), jnp.stack(p_conv), jnp.stack(p_lat), jnp.stack(p_rk), jnp.stack(p_mk), jnp.stack(p_mv),
            s_ssm.reshape(state_ssm.shape), jnp.stack(s_conv), jnp.stack(s_lat), jnp.stack(s_rk))
```
